```python
import jax, jax.numpy as jnp
from jax import lax
import numpy as np

D_MODEL = 1024
BATCH = 4
SEQ = 4096
DEPTH = 2
DEC_BATCH = 32
DEC_SEQ = 8
PAST_LEN = 8192
PAGE_SIZE = 128

HEAD_DIM = 64
MLSTM_HEADS = 4
MLSTM_DQK = 64
MLSTM_DV = 128
MLSTM_WIDTH = MLSTM_HEADS * MLSTM_DV
MLSTM_CHUNK = 64
MOBA_HEADS = 8
MOBA_WIDTH = MOBA_HEADS * HEAD_DIM
MOBA_BLOCK = 256
MOBA_TOPK = 3
MOBA_Q_BLOCK = 128
FOX_HEADS = D_MODEL // HEAD_DIM
FOX_WIDTH = FOX_HEADS * HEAD_DIM
FOX_Q_BLOCK = 128
FFN_HIDDEN = ((8 * D_MODEL + 3 * 256 - 1) // (3 * 256)) * 256
N_EVEN = (DEPTH + 1) // 2
N_ODD = DEPTH // 2
EVEN_SPLITS = (MLSTM_HEADS * MLSTM_DQK, MLSTM_HEADS * MLSTM_DQK, MLSTM_WIDTH, MLSTM_WIDTH,
               MLSTM_HEADS, MLSTM_HEADS, MOBA_WIDTH, MOBA_WIDTH, MOBA_WIDTH)
EVEN_IN = 2 * MLSTM_HEADS * MLSTM_DQK + 2 * MLSTM_WIDTH + 2 * MLSTM_HEADS + 3 * MOBA_WIDTH
EVEN_OUT = MLSTM_WIDTH + MOBA_WIDTH
ODD_SPLITS = (FOX_WIDTH, FOX_WIDTH, FOX_WIDTH, FOX_HEADS)
ODD_IN = 3 * FOX_WIDTH + FOX_HEADS
RMS_EPS = 1e-6
F32 = jnp.float32

kernel_name = 'hybrid_mlstm_moba_fox_step'


def _split(z, sizes):
    out, start = [], 0
    for s in sizes:
        out.append(z[..., start:start + s])
        start += s
    return out


def rmsnorm(x, g):
    xf = x.astype(F32)
    y = xf * lax.rsqrt(jnp.mean(xf * xf, axis=-1, keepdims=True) + RMS_EPS)
    return (y * g.astype(F32)).astype(x.dtype)


def alibi_slopes(n):
    return jnp.asarray(np.array([2.0 ** (-8.0 * (i + 1) / n) for i in range(n)], dtype=np.float32))


def swiglu(x, wg, wu, wd):
    return (jax.nn.silu(x @ wg) * (x @ wu)) @ wd


def gather_pages(pool, page_table):
    g = pool[page_table]
    return g.reshape((g.shape[0], g.shape[1] * g.shape[2]) + g.shape[3:])


def mlstm_chunkwise(q, k, v, ig, logf, C0, n0, m0):
    B, T, H, DK = q.shape
    DV = v.shape[-1]
    L = MLSTM_CHUNK if T % MLSTM_CHUNK == 0 else T
    nc = T // L
    qc = q.astype(F32).reshape(B, nc, L, H, DK).transpose(1, 0, 3, 2, 4)
    kc = k.astype(F32).reshape(B, nc, L, H, DK).transpose(1, 0, 3, 2, 4)
    vc = v.astype(F32).reshape(B, nc, L, H, DV).transpose(1, 0, 3, 2, 4)
    ic = ig.astype(F32).reshape(B, nc, L, H).transpose(1, 0, 3, 2)
    fc = logf.astype(F32).reshape(B, nc, L, H).transpose(1, 0, 3, 2)
    causal = jnp.tril(jnp.ones((L, L), dtype=bool))

    def step(carry, xs):
        C, n, m = carry
        qb, kb, vb, ib, fb = xs
        b = jnp.cumsum(fb, axis=-1)
        dmat = jnp.where(causal, b[..., :, None] - b[..., None, :] + ib[..., None, :], -jnp.inf)
        inter = b + m[..., None]
        mt = jnp.maximum(inter, jnp.max(dmat, axis=-1))
        w_intra = jnp.exp(dmat - mt[..., None])
        w_inter = jnp.exp(inter - mt)
        qk = jnp.einsum('bhtd,bhsd->bhts', qb, kb) * w_intra
        num = w_inter[..., None] * jnp.einsum('bhtd,bhdv->bhtv', qb, C) + jnp.einsum('bhts,bhsv->bhtv', qk, vb)
        den = w_inter * jnp.einsum('bhtd,bhd->bht', qb, n) + jnp.sum(qk, axis=-1)
        h = num / jnp.maximum(jnp.abs(den), jnp.exp(-mt))[..., None]
        m_new = mt[..., -1]
        decay = jnp.exp(b[..., -1] + m - m_new)
        w_s = jnp.exp(b[..., -1:] - b + ib - m_new[..., None])
        kw = kb * w_s[..., None]
        C_new = decay[..., None, None] * C + jnp.einsum('bhsd,bhsv->bhdv', kw, vb)
        n_new = decay[..., None] * n + jnp.sum(kw, axis=2)
        return (C_new, n_new, m_new), h

    (C, n, m), hs = lax.scan(step, (C0.astype(F32), n0.astype(F32), m0.astype(F32)), (qc, kc, vc, ic, fc))
    hs = hs.transpose(1, 0, 3, 2, 4).reshape(B, T, H, DV)
    return hs, C, n, m


def moba_attention(q, k, v, slopes):
    B, Tq, H, D = q.shape
    L = k.shape[1]
    NB = -(-L // MOBA_BLOCK)
    pad = NB * MOBA_BLOCK - L
    k = jnp.pad(k, ((0, 0), (0, pad), (0, 0), (0, 0)))
    v = jnp.pad(v, ((0, 0), (0, pad), (0, 0), (0, 0)))
    kb = k.reshape(B, NB, MOBA_BLOCK, H, D)
    vb = v.reshape(B, NB, MOBA_BLOCK, H, D)
    kmean = jnp.mean(kb.astype(F32), axis=2)
    qpos = (L - Tq) + jnp.arange(Tq)
    own = qpos // MOBA_BLOCK
    gate = jnp.einsum('bthd,bnhd->bthn', q.astype(F32), kmean)
    past_ok = jnp.arange(NB)[None, :] < own[:, None]
    gate = jnp.where(past_ok[None, :, None, :], gate, -jnp.inf)
    n_sel = min(MOBA_TOPK, NB)
    _, top = lax.top_k(gate, n_sel)
    rank_ok = jnp.arange(n_sel)[None, :] < own[:, None]
    sel = jnp.concatenate([top.astype(jnp.int32),
                           jnp.broadcast_to(own[None, :, None, None].astype(jnp.int32), (B, Tq, H, 1))], axis=-1)
    ok = jnp.concatenate([jnp.broadcast_to(rank_ok[None, :, None, :], (B, Tq, H, n_sel)),
                          jnp.ones((B, Tq, H, 1), dtype=bool)], axis=-1)
    S = n_sel + 1
    qc = MOBA_Q_BLOCK if Tq % MOBA_Q_BLOCK == 0 else Tq
    nc = Tq // qc
    q_ch = q.reshape(B * nc, qc, H, D)
    sel_ch = sel.reshape(B * nc, qc, H, S)
    ok_ch = ok.reshape(B * nc, qc, H, S)
    pos_ch = jnp.tile(qpos.reshape(nc, qc), (B, 1))
    b_ch = jnp.repeat(jnp.arange(B), nc)
    kbh = kb.transpose(0, 3, 1, 2, 4)
    vbh = vb.transpose(0, 3, 1, 2, 4)
    head = jnp.arange(H)[None, :, None]
    offs = jnp.arange(MOBA_BLOCK)
    scale = D ** -0.5

    def block(args):
        qb, sb, okb, pb, bi = args
        kg = kbh[bi][head, sb]
        vg = vbh[bi][head, sb]
        s = jnp.einsum('qhd,qhnjd->qhnj', qb, kg, preferred_element_type=F32) * scale
        dist = pb[:, None, None, None] - (sb[..., None] * MOBA_BLOCK + offs)
        s = s - slopes[None, :, None, None] * dist.astype(F32)
        s = jnp.where(okb[..., None] & (dist >= 0), s, -jnp.inf)
        p = jax.nn.softmax(s.reshape(qc, H, S * MOBA_BLOCK), axis=-1).reshape(qc, H, S, MOBA_BLOCK)
        return jnp.einsum('qhnj,qhnjd->qhd', p.astype(vg.dtype), vg, preferred_element_type=F32).astype(q.dtype)

    out = lax.map(block, (q_ch, sel_ch, ok_ch, pos_ch, b_ch))
    return out.reshape(B, Tq, H, D)


def fox_attention(q, k, v, logf):
    B, Tq, H, D = q.shape
    L = k.shape[1]
    c = jnp.cumsum(logf.astype(F32), axis=1)
    c_k = c.transpose(0, 2, 1)
    qpos = (L - Tq) + jnp.arange(Tq)
    cq = c[:, L - Tq:]
    qc = FOX_Q_BLOCK if Tq % FOX_Q_BLOCK == 0 else Tq
    nc = Tq // qc
    q_ch = q.reshape(B, nc, qc, H, D).transpose(1, 0, 2, 3, 4)
    cq_ch = cq.reshape(B, nc, qc, H).transpose(1, 0, 3, 2)
    pos_ch = qpos.reshape(nc, qc)
    kpos = jnp.arange(L)
    scale = D ** -0.5

    def block(args):
        qb, cb, pb = args
        s = jnp.einsum('bqhd,bkhd->bhqk', qb, k, preferred_element_type=F32) * scale
        s = s + cb[..., None] - c_k[:, :, None, :]
        s = jnp.where(kpos[None, None, None, :] <= pb[None, None, :, None], s, -jnp.inf)
        p = jax.nn.softmax(s, axis=-1)
        return jnp.einsum('bhqk,bkhd->bqhd', p.astype(v.dtype), v, preferred_element_type=F32).astype(q.dtype)

    out = lax.map(block, (q_ch, cq_ch, pos_ch))
    return out.transpose(1, 0, 2, 3, 4).reshape(B, Tq, H, D)


def even_mix(h, w_in, b_ig, b_fg, g_head, w_out, slopes, C0, n0, m0, k_past=None, v_past=None):
    B, T, _ = h.shape
    q1, k1, v1, o1, i1, f1, q2, k2, v2 = _split(h @ w_in, EVEN_SPLITS)
    q1 = q1.reshape(B, T, MLSTM_HEADS, MLSTM_DQK) * (MLSTM_DQK ** -0.5)
    k1 = k1.reshape(B, T, MLSTM_HEADS, MLSTM_DQK)
    v1 = v1.reshape(B, T, MLSTM_HEADS, MLSTM_DV)
    ig = (i1 + b_ig).astype(F32)
    logf = jax.nn.log_sigmoid((f1 + b_fg).astype(F32))
    hm, C, n, m = mlstm_chunkwise(q1, k1, v1, ig, logf, C0, n0, m0)
    hm = hm * lax.rsqrt(jnp.mean(hm * hm, axis=-1, keepdims=True) + RMS_EPS)
    hm = hm.reshape(B, T, MLSTM_WIDTH) * g_head.astype(F32)
    y1 = (jax.nn.sigmoid(o1.astype(F32)) * hm).astype(h.dtype)
    q2 = q2.reshape(B, T, MOBA_HEADS, HEAD_DIM)
    k2 = k2.reshape(B, T, MOBA_HEADS, HEAD_DIM)
    v2 = v2.reshape(B, T, MOBA_HEADS, HEAD_DIM)
    if k_past is None:
        k_all, v_all = k2, v2
    else:
        k_all = jnp.concatenate([k_past.astype(k2.dtype), k2], axis=1)
        v_all = jnp.concatenate([v_past.astype(v2.dtype), v2], axis=1)
    y2 = moba_attention(q2, k_all, v_all, slopes).reshape(B, T, MOBA_WIDTH)
    y = jnp.concatenate([y1, y2], axis=-1) @ w_out
    return y, (k2, v2, C.astype(h.dtype), n.astype(h.dtype), m.astype(h.dtype))


def odd_mix(h, w_in, b_f, w_out, k_past=None, v_past=None, lf_past=None):
    B, T, _ = h.shape
    q, k, v, f = _split(h @ w_in, ODD_SPLITS)
    q = q.reshape(B, T, FOX_HEADS, HEAD_DIM)
    k = k.reshape(B, T, FOX_HEADS, HEAD_DIM)
    v = v.reshape(B, T, FOX_HEADS, HEAD_DIM)
    logf = jax.nn.log_sigmoid((f + b_f).astype(F32))
    if k_past is None:
        k_all, v_all, lf_all = k, v, logf
    else:
        k_all = jnp.concatenate([k_past.astype(k.dtype), k], axis=1)
        v_all = jnp.concatenate([v_past.astype(v.dtype), v], axis=1)
        lf_all = jnp.concatenate([lf_past.astype(F32), logf], axis=1)
    y = fox_attention(q, k_all, v_all, lf_all).reshape(B, T, FOX_WIDTH) @ w_out
    return y, (k, v, logf.astype(h.dtype))


def setup_inputs(seed: int = 0) -> dict:
    key = jax.random.key(seed)
    ks = iter(jax.random.split(key, 32))

    def nrm(shape, scale):
        return jax.random.normal(next(ks), shape, jnp.float32) * scale

    n_pages = PAST_LEN // PAGE_SIZE
    n_used = DEC_BATCH * n_pages
    n_phys = n_used + n_used // 4
    page_table = jax.random.permutation(next(ks), n_phys)[:n_used].reshape(DEC_BATCH, n_pages).astype(jnp.int32)
    return {
        'x_prompt': nrm((BATCH, SEQ, D_MODEL), 1.0),
        'x_sample': nrm((DEC_BATCH, DEC_SEQ, D_MODEL), 1.0),
        'cache_moba_k': nrm((N_EVEN, n_phys, PAGE_SIZE, MOBA_HEADS, HEAD_DIM), 1.0),
        'cache_moba_v': nrm((N_EVEN, n_phys, PAGE_SIZE, MOBA_HEADS, HEAD_DIM), 1.0),
        'state_mlstm_C': nrm((N_EVEN, DEC_BATCH, MLSTM_HEADS, MLSTM_DQK, MLSTM_DV), 0.5),
        'state_mlstm_n': nrm((N_EVEN, DEC_BATCH, MLSTM_HEADS, MLSTM_DQK), 0.5),
        'state_mlstm_m': nrm((N_EVEN, DEC_BATCH, MLSTM_HEADS), 1.0),
        'cache_fox_k': nrm((N_ODD, n_phys, PAGE_SIZE, FOX_HEADS, HEAD_DIM), 1.0),
        'cache_fox_v': nrm((N_ODD, n_phys, PAGE_SIZE, FOX_HEADS, HEAD_DIM), 1.0),
        'cache_fox_logf': jax.nn.log_sigmoid(nrm((N_ODD, n_phys, PAGE_SIZE, FOX_HEADS), 1.0) + 3.0),
        'page_table': page_table,
        'norm_mix_g': 1.0 + nrm((DEPTH, D_MODEL), 0.02),
        'norm_ffn_g': 1.0 + nrm((DEPTH, D_MODEL), 0.02),
        'norm_final_g': 1.0 + nrm((D_MODEL,), 0.02),
        'even_w_in': nrm((N_EVEN, D_MODEL, EVEN_IN), D_MODEL ** -0.5),
        'even_b_ig': nrm((N_EVEN, MLSTM_HEADS), 0.1),
        'even_b_fg': jnp.linspace(3.0, 6.0, MLSTM_HEADS)[None, :] + nrm((N_EVEN, MLSTM_HEADS), 0.1),
        'even_head_norm_g': 1.0 + nrm((N_EVEN, MLSTM_WIDTH), 0.02),
        'even_w_out': nrm((N_EVEN, EVEN_OUT, D_MODEL), EVEN_OUT ** -0.5),
        'odd_w_in': nrm((N_ODD, D_MODEL, ODD_IN), D_MODEL ** -0.5),
        'odd_b_f': jnp.linspace(1.0, 4.0, FOX_HEADS)[None, :] + nrm((N_ODD, FOX_HEADS), 0.1),
        'odd_w_out': nrm((N_ODD, FOX_WIDTH, D_MODEL), FOX_WIDTH ** -0.5),
        'ffn_w_gate': nrm((DEPTH, D_MODEL, FFN_HIDDEN), D_MODEL ** -0.5),
        'ffn_w_up': nrm((DEPTH, D_MODEL, FFN_HIDDEN), D_MODEL ** -0.5),
        'ffn_w_down': nrm((DEPTH, FFN_HIDDEN, D_MODEL), FFN_HIDDEN ** -0.5),
    }


def reference(x_prompt, x_sample, cache_moba_k, cache_moba_v, state_mlstm_C, state_mlstm_n, state_mlstm_m,
              cache_fox_k, cache_fox_v, cache_fox_logf, page_table, norm_mix_g, norm_ffn_g, norm_final_g,
              even_w_in, even_b_ig, even_b_fg, even_head_norm_g, even_w_out, odd_w_in, odd_b_f, odd_w_out,
              ffn_w_gate, ffn_w_up, ffn_w_down):
    slopes = alibi_slopes(MOBA_HEADS)
    Bp = x_prompt.shape[0]
    hp, hs = x_prompt, x_sample
    pe, se, po, so = [], [], [], []
    for layer in range(DEPTH):
        li = layer // 2
        xp = rmsnorm(hp, norm_mix_g[layer])
        xs = rmsnorm(hs, norm_mix_g[layer])
        if layer % 2 == 0:
            C0 = jnp.zeros((Bp, MLSTM_HEADS, MLSTM_DQK, MLSTM_DV), F32)
            n0 = jnp.zeros((Bp, MLSTM_HEADS, MLSTM_DQK), F32)
            m0 = jnp.zeros((Bp, MLSTM_HEADS), F32)
            yp, st_p = even_mix(xp, even_w_in[li], even_b_ig[li], even_b_fg[li], even_head_norm_g[li],
                                even_w_out[li], slopes, C0, n0, m0)
            ys, st_s = even_mix(xs, even_w_in[li], even_b_ig[li], even_b_fg[li], even_head_norm_g[li],
                                even_w_out[li], slopes, state_mlstm_C[li], state_mlstm_n[li], state_mlstm_m[li],
                                gather_pages(cache_moba_k[li], page_table), gather_pages(cache_moba_v[li], page_table))
            pe.append(st_p)
            se.append(st_s)
        else:
            yp, st_p = odd_mix(xp, odd_w_in[li], odd_b_f[li], odd_w_out[li])
            ys, st_s = odd_mix(xs, odd_w_in[li], odd_b_f[li], odd_w_out[li],
                               gather_pages(cache_fox_k[li], page_table), gather_pages(cache_fox_v[li], page_table),
                               gather_pages(cache_fox_logf[li], page_table))
            po.append(st_p)
            so.append(st_s)
        hp = hp + yp
        hs = hs + ys
        hp = hp + swiglu(rmsnorm(hp, norm_ffn_g[layer]), ffn_w_gate[layer], ffn_w_up[layer], ffn_w_down[layer])
        hs = hs + swiglu(rmsnorm(hs, norm_ffn_g[layer]), ffn_w_gate[layer], ffn_w_up[layer], ffn_w_down[layer])
    y_prompt = rmsnorm(hp, norm_final_g)
    y_sample = rmsnorm(hs, norm_final_g)
    p_moba_k = jnp.stack([s[0] for s in pe])
    p_moba_v = jnp.stack([s[1] for s in pe])
    p_mlstm_C = jnp.stack([s[2] for s in pe])
    p_mlstm_n = jnp.stack([s[3] for s in pe])
    p_mlstm_m = jnp.stack([s[4] for s in pe])
    p_fox_k = jnp.stack([s[0] for s in po])
    p_fox_v = jnp.stack([s[1] for s in po])
    p_fox_logf = jnp.stack([s[2] for s in po])
    s_moba_k = jnp.stack([s[0] for s in se])
    s_moba_v = jnp.stack([s[1] for s in se])
    s_mlstm_C = jnp.stack([s[2] for s in se])
    s_mlstm_n = jnp.stack([s[3] for s in se])
    s_mlstm_m = jnp.stack([s[4] for s in se])
    s_fox_k = jnp.stack([s[0] for s in so])
    s_fox_v = jnp.stack([s[1] for s in so])
    s_fox_logf = jnp.stack([s[2] for s in so])
    return (y_prompt, y_sample, p_moba_k, p_moba_v, p_mlstm_C, p_mlstm_n, p_mlstm_m, p_fox_k, p_fox_v, p_fox_logf,
            s_moba_k, s_moba_v, s_mlstm_C, s_mlstm_n, s_mlstm_m, s_fox_k, s_fox_v, s_fox_logf)
```

```python
import functools

import jax
import jax.numpy as jnp
import numpy as np
from jax import lax
from jax.experimental import pallas as pl
from jax.experimental.pallas import tpu as pltpu

F32 = jnp.float32
BF16 = jnp.bfloat16
HIGHEST = lax.Precision.HIGHEST

LANES = 128
SUBLANES = 8

HEAD_DIM = 64
MLSTM_HEADS = 4
MLSTM_DQK = 64
MLSTM_DV = 128
MLSTM_CHUNK = 64
MOBA_HEADS = 8
MOBA_BLOCK = 256
MOBA_TOPK = 3
FOX_HEADS = 16
RMS_EPS = 1e-6
GATE_PAD = LANES

NEG_INF = float("-inf")


def _log_sigmoid(x):
    return -(jnp.maximum(-x, 0.0) + jnp.log1p(jnp.exp(-jnp.abs(x))))


def _rmsnorm(x, g):
    return x * lax.rsqrt(jnp.mean(x * x, axis=-1, keepdims=True) + RMS_EPS) * g


def _dot(a, b):
    return jnp.dot(a, b, preferred_element_type=F32)


def _dot_nt(a, b, precision=None):
    return lax.dot_general(a, b, (((1,), (1,)), ((), ())), precision=precision, preferred_element_type=F32)


def _dot_tn(a, b):
    return lax.dot_general(a, b, (((0,), (0,)), ((), ())), preferred_element_type=F32)


def _iota(shape, dim):
    return lax.broadcasted_iota(jnp.int32, shape, dim)


def _norm_proj_body(x_ref, g_ref, *refs, n_w):
    w_refs, o_refs = refs[:n_w], refs[n_w:]
    xb = _rmsnorm(x_ref[...], g_ref[...]).astype(BF16)
    for w_ref, o_ref in zip(w_refs, o_refs):
        o_ref[...] = _dot(xb, w_ref[...])


def norm_proj(x, g, ws, tm):
    m, d = x.shape
    assert m % tm == 0
    in_specs = [pl.BlockSpec((tm, d), lambda i: (i, 0)), pl.BlockSpec((1, d), lambda i: (0, 0))]
    in_specs += [pl.BlockSpec(w.shape, lambda i: (0, 0)) for w in ws]
    out_specs = [pl.BlockSpec((tm, w.shape[1]), lambda i: (i, 0)) for w in ws]
    out_shape = [jax.ShapeDtypeStruct((m, w.shape[1]), F32) for w in ws]
    return pl.pallas_call(
        functools.partial(_norm_proj_body, n_w=len(ws)),
        grid=(m // tm,), in_specs=in_specs, out_specs=out_specs, out_shape=out_shape,
        name="norm_proj",
    )(x, g.reshape(1, d), *ws)


def _proj_res_body(h_ref, *refs, n_a):
    a_refs, w_refs, o_ref = refs[:n_a], refs[n_a:2 * n_a], refs[2 * n_a]
    acc = h_ref[...]
    for a_ref, w_ref in zip(a_refs, w_refs):
        acc = acc + _dot(a_ref[...].astype(BF16), w_ref[...])
    o_ref[...] = acc


def proj_res(h, acts, ws, tm):
    m, d = h.shape
    assert m % tm == 0
    in_specs = [pl.BlockSpec((tm, d), lambda i: (i, 0))]
    in_specs += [pl.BlockSpec((tm, a.shape[1]), lambda i: (i, 0)) for a in acts]
    in_specs += [pl.BlockSpec(w.shape, lambda i: (0, 0)) for w in ws]
    return pl.pallas_call(
        functools.partial(_proj_res_body, n_a=len(acts)),
        grid=(m // tm,), in_specs=in_specs,
        out_specs=pl.BlockSpec((tm, d), lambda i: (i, 0)),
        out_shape=jax.ShapeDtypeStruct((m, d), F32),
        name="proj_res",
    )(h, *acts, *ws)


def _ffn_body(x_ref, g_ref, gf_ref, wg_ref, wu_ref, wd_ref, o_ref, xn_scr, acc_scr, *, final_norm):
    k = pl.program_id(1)

    @pl.when(k == 0)
    def _():
        xn_scr[...] = _rmsnorm(x_ref[...], g_ref[...]).astype(BF16)
        acc_scr[...] = jnp.zeros_like(acc_scr)

    xn = xn_scr[...]
    gate = _dot(xn, wg_ref[...])
    up = _dot(xn, wu_ref[...])
    act = (gate * jax.nn.sigmoid(gate) * up).astype(BF16)
    acc_scr[...] += _dot(act, wd_ref[...])

    @pl.when(k == pl.num_programs(1) - 1)
    def _():
        y = x_ref[...] + acc_scr[...]
        o_ref[...] = _rmsnorm(y, gf_ref[...]) if final_norm else y


def ffn(h, g, g_final, wg, wu, wd, tm, th, final_norm):
    m, d = h.shape
    hidden = wg.shape[1]
    assert m % tm == 0 and hidden % th == 0
    return pl.pallas_call(
        functools.partial(_ffn_body, final_norm=final_norm),
        grid=(m // tm, hidden // th),
        in_specs=[
            pl.BlockSpec((tm, d), lambda i, k: (i, 0)),
            pl.BlockSpec((1, d), lambda i, k: (0, 0)),
            pl.BlockSpec((1, d), lambda i, k: (0, 0)),
            pl.BlockSpec((d, th), lambda i, k: (0, k)),
            pl.BlockSpec((d, th), lambda i, k: (0, k)),
            pl.BlockSpec((th, d), lambda i, k: (k, 0)),
        ],
        out_specs=pl.BlockSpec((tm, d), lambda i, k: (i, 0)),
        out_shape=jax.ShapeDtypeStruct((m, d), F32),
        scratch_shapes=[pltpu.VMEM((tm, d), BF16), pltpu.VMEM((tm, d), F32)],
        compiler_params=pltpu.CompilerParams(dimension_semantics=("arbitrary", "arbitrary")),
        name="ffn",
    )(h, g.reshape(1, d), g_final.reshape(1, d), wg, wu, wd)


def _mlstm_body(q_ref, k_ref, v_ref, o_ref, g_ref, gb_ref, gh_ref, c0_ref, n0_ref, m0_ref,
                y_ref, c_ref, n_ref, m_ref, c_scr, n_scr, m_scr, *, chunk, n_sub):
    j = pl.program_id(1)
    nh, dk, dv = MLSTM_HEADS, MLSTM_DQK, MLSTM_DV

    @pl.when(j == 0)
    def _():
        c_scr[...] = c0_ref[...]
        n_scr[...] = n0_ref[...]
        m_scr[...] = m0_ref[...]

    rows = chunk * n_sub
    gpre = g_ref[...] + gb_ref[...]
    lane = _iota((rows, GATE_PAD), 1)
    act = jnp.where(lane < nh, gpre, _log_sigmoid(gpre))
    r_i = _iota((chunk, chunk), 0)
    c_i = _iota((chunk, chunk), 1)
    causal = r_i >= c_i
    eye = r_i == c_i
    tril = causal.astype(F32)

    def to_row(col):
        return jnp.sum(jnp.where(eye, col, 0.0), axis=0, keepdims=True)

    for s in range(n_sub):
        rs = slice(s * chunk, (s + 1) * chunk)
        a = act[rs]
        bcum = jnp.dot(tril, a, precision=HIGHEST, preferred_element_type=F32)
        for h in range(nh):
            q = q_ref[rs, h * dk:(h + 1) * dk] * (dk ** -0.5)
            k = k_ref[rs, h * dk:(h + 1) * dk]
            v = v_ref[rs, h * dv:(h + 1) * dv]
            qb, kb, vb = q.astype(BF16), k.astype(BF16), v.astype(BF16)
            ig_c = a[:, h:h + 1]
            b_c = bcum[:, nh + h:nh + h + 1]
            ig_r, b_r = to_row(ig_c), to_row(b_c)
            m_prev = m_scr[h]
            c_prev = c_scr[h]
            n_prev = n_scr[h]
            dmat = jnp.where(causal, b_c - b_r + ig_r, NEG_INF)
            inter = b_c + m_prev
            mt = jnp.maximum(inter, jnp.max(dmat, axis=-1, keepdims=True))
            w_intra = jnp.exp(dmat - mt)
            w_inter = jnp.exp(inter - mt)
            qk = _dot_nt(qb, kb) * w_intra
            num = w_inter * _dot(qb, c_prev.astype(BF16)) + _dot(qk.astype(BF16), vb)
            den = w_inter * jnp.sum(q * n_prev, axis=-1, keepdims=True) + jnp.sum(qk, axis=-1, keepdims=True)
            hh = num / jnp.maximum(jnp.abs(den), jnp.exp(-mt))
            m_new = mt[chunk - 1:chunk, :]
            b_last = b_c[chunk - 1:chunk, :]
            decay = jnp.exp(b_last + m_prev - m_new)
            w_s = jnp.exp(b_last - b_c + ig_c - m_new)
            kw = k * w_s
            c_scr[h] = decay * c_prev + _dot_tn(kw.astype(BF16), vb)
            n_scr[h] = decay * n_prev + jnp.sum(kw, axis=0, keepdims=True)
            m_scr[h] = m_new
            hn = hh * lax.rsqrt(jnp.mean(hh * hh, axis=-1, keepdims=True) + RMS_EPS)
            hn = hn * gh_ref[:, h * dv:(h + 1) * dv]
            y_ref[rs, h * dv:(h + 1) * dv] = jax.nn.sigmoid(o_ref[rs, h * dv:(h + 1) * dv]) * hn

    @pl.when(j == pl.num_programs(1) - 1)
    def _():
        c_ref[...] = c_scr[...]
        n_ref[...] = n_scr[...]
        m_ref[...] = m_scr[...]


def mlstm(q, k, v, o, gates, gate_bias, g_head, c0, n0, m0, batch, seq, n_sub):
    nh, dk, dv = MLSTM_HEADS, MLSTM_DQK, MLSTM_DV
    chunk = MLSTM_CHUNK if seq % MLSTM_CHUNK == 0 else seq
    rows = chunk * n_sub
    assert seq % rows == 0 and rows % SUBLANES == 0
    steps = seq // rows
    row_spec = lambda w: pl.BlockSpec((rows, w), lambda b, j: (b * steps + j, 0))
    st4 = lambda s: pl.BlockSpec((None,) + s, lambda b, j: (b, 0, 0, 0))
    y, c, n, m = pl.pallas_call(
        functools.partial(_mlstm_body, chunk=chunk, n_sub=n_sub),
        grid=(batch, steps),
        in_specs=[row_spec(nh * dk), row_spec(nh * dk), row_spec(nh * dv), row_spec(nh * dv), row_spec(GATE_PAD),
                  pl.BlockSpec((1, GATE_PAD), lambda b, j: (0, 0)),
                  pl.BlockSpec((1, nh * dv), lambda b, j: (0, 0)),
                  st4((nh, dk, dv)), st4((nh, 1, dk)), st4((nh, 1, 1))],
        out_specs=[row_spec(nh * dv), st4((nh, dk, dv)), st4((nh, 1, dk)), st4((nh, 1, 1))],
        out_shape=[jax.ShapeDtypeStruct((batch * seq, nh * dv), F32),
                   jax.ShapeDtypeStruct((batch, nh, dk, dv), F32),
                   jax.ShapeDtypeStruct((batch, nh, 1, dk), F32),
                   jax.ShapeDtypeStruct((batch, nh, 1, 1), F32)],
        scratch_shapes=[pltpu.VMEM((nh, dk, dv), F32), pltpu.VMEM((nh, 1, dk), F32), pltpu.VMEM((nh, 1, 1), F32)],
        compiler_params=pltpu.CompilerParams(dimension_semantics=("arbitrary", "arbitrary")),
        name="mlstm",
    )(q, k, v, o, gates, gate_bias, g_head.reshape(1, nh * dv),
      c0, n0.reshape(batch, nh, 1, dk), m0.reshape(batch, nh, 1, 1))
    return y, c, n.reshape(batch, nh, dk), m.reshape(batch, nh)


def _topk_blocks(gate, n_blocks, n_valid):
    idx = _iota(gate.shape, 1)
    rank = jnp.zeros(gate.shape, F32)
    for c in range(n_blocks):
        col = gate[:, c:c + 1]
        beats = (col > gate) | ((col == gate) & (c < idx))
        candidate = jnp.where(c < n_valid, 1.0, 0.0)
        rank = rank + jnp.where(beats, candidate, 0.0)
    return jnp.where((rank < MOBA_TOPK) & (idx < n_valid), 1.0, 0.0)


def _moba_body(q_ref, k_ref, v_ref, sl_ref, o_ref, kb_scr, vb_scr, km_scr, *, n_blocks):
    i = pl.program_id(2)
    blk, d = MOBA_BLOCK, HEAD_DIM
    heads = LANES // d

    @pl.when(i == 0)
    def _():
        for n in range(n_blocks):
            km_scr[n:n + 1, :] = jnp.mean(k_ref[n * blk:(n + 1) * blk, :], axis=0, keepdims=True)
        for hh in range(heads):
            kb_scr[hh] = k_ref[:, hh * d:(hh + 1) * d].astype(BF16)
            vb_scr[hh] = v_ref[:, hh * d:(hh + 1) * d].astype(BF16)

    rel = (_iota((blk, blk), 0) - _iota((blk, blk), 1)).astype(F32)
    lane_b = _iota((blk, n_blocks), 1)
    outs = []
    for hh in range(heads):
        q = q_ref[:, hh * d:(hh + 1) * d]
        qb = (q * (d ** -0.5)).astype(BF16)
        slope = sl_ref[:, hh * d:hh * d + 1]
        gate = _dot_nt(q, km_scr[:, hh * d:(hh + 1) * d], precision=HIGHEST)
        sel = _topk_blocks(gate, n_blocks, i)

        own = pl.multiple_of(i * blk, blk)
        s = _dot_nt(qb, kb_scr[hh, pl.ds(own, blk), :]) - slope * rel
        s = jnp.where(rel >= 0.0, s, NEG_INF)
        m0 = jnp.max(s, axis=-1, keepdims=True)
        p = jnp.exp(s - m0)
        l0 = jnp.sum(p, axis=-1, keepdims=True)
        acc0 = _dot(p.astype(BF16), vb_scr[hh, pl.ds(own, blk), :])

        def past(j, carry, qb=qb, slope=slope, sel=sel, hh=hh):
            m, l, acc = carry
            start = pl.multiple_of(j * blk, blk)
            keep = jnp.max(jnp.where(lane_b == j, sel, 0.0), axis=-1, keepdims=True) > 0.0
            dist = rel + ((i - j) * blk).astype(F32)
            s = _dot_nt(qb, kb_scr[hh, pl.ds(start, blk), :]) - slope * dist
            s = jnp.where(keep, s, NEG_INF)
            m_new = jnp.maximum(m, jnp.max(s, axis=-1, keepdims=True))
            alpha = jnp.exp(m - m_new)
            p = jnp.exp(s - m_new)
            l = alpha * l + jnp.sum(p, axis=-1, keepdims=True)
            acc = alpha * acc + _dot(p.astype(BF16), vb_scr[hh, pl.ds(start, blk), :])
            return m_new, l, acc

        m, l, acc = lax.fori_loop(0, i, past, (m0, l0, acc0))
        outs.append(acc / l)
    o_ref[...] = jnp.concatenate(outs, axis=-1)


def _alibi_lanes(n_heads):
    slopes = np.array([2.0 ** (-8.0 * (i + 1) / n_heads) for i in range(n_heads)], dtype=np.float32)
    return jnp.asarray(np.repeat(slopes, HEAD_DIM).reshape(1, n_heads * HEAD_DIM))


def moba_prompt(q, k, v, batch, seq):
    width = MOBA_HEADS * HEAD_DIM
    assert seq % MOBA_BLOCK == 0
    n_blocks = seq // MOBA_BLOCK
    heads = LANES // HEAD_DIM
    return pl.pallas_call(
        functools.partial(_moba_body, n_blocks=n_blocks),
        grid=(batch, width // LANES, n_blocks),
        in_specs=[pl.BlockSpec((MOBA_BLOCK, LANES), lambda b, hp, i: (b * n_blocks + i, hp)),
                  pl.BlockSpec((seq, LANES), lambda b, hp, i: (b, hp)),
                  pl.BlockSpec((seq, LANES), lambda b, hp, i: (b, hp)),
                  pl.BlockSpec((1, LANES), lambda b, hp, i: (0, hp))],
        out_specs=pl.BlockSpec((MOBA_BLOCK, LANES), lambda b, hp, i: (b * n_blocks + i, hp)),
        out_shape=jax.ShapeDtypeStruct((batch * seq, width), F32),
        scratch_shapes=[pltpu.VMEM((heads, seq, HEAD_DIM), BF16), pltpu.VMEM((heads, seq, HEAD_DIM), BF16),
                        pltpu.VMEM((n_blocks, LANES), F32)],
        compiler_params=pltpu.CompilerParams(dimension_semantics=("arbitrary", "arbitrary", "arbitrary")),
        name="moba_prompt",
    )(q, k, v, _alibi_lanes(MOBA_HEADS))


def _fox_gate_body(f_ref, b_ref, lf_ref, cc_ref, cr_ref, carry_scr, *, tiles_per_seq):
    i = pl.program_id(0)
    tm = f_ref.shape[0]

    @pl.when(i % tiles_per_seq == 0)
    def _():
        carry_scr[...] = jnp.zeros_like(carry_scr)

    lf = _log_sigmoid(f_ref[...] + b_ref[...])
    lf_ref[...] = lf
    tril = (_iota((tm, tm), 0) >= _iota((tm, tm), 1)).astype(F32)
    c = jnp.dot(tril, lf, precision=HIGHEST, preferred_element_type=F32) + carry_scr[...]
    carry_scr[...] = c[tm - 1:tm, :]
    cc_ref[...] = c
    cr_ref[...] = c.T[:FOX_HEADS, :]


def fox_gates(f, bias, seq, tm):
    m = f.shape[0]
    assert seq % tm == 0 and m % seq == 0
    return pl.pallas_call(
        functools.partial(_fox_gate_body, tiles_per_seq=seq // tm),
        grid=(m // tm,),
        in_specs=[pl.BlockSpec((tm, GATE_PAD), lambda i: (i, 0)), pl.BlockSpec((1, GATE_PAD), lambda i: (0, 0))],
        out_specs=[pl.BlockSpec((tm, GATE_PAD), lambda i: (i, 0)), pl.BlockSpec((tm, GATE_PAD), lambda i: (i, 0)),
                   pl.BlockSpec((FOX_HEADS, tm), lambda i: (0, i))],
        out_shape=[jax.ShapeDtypeStruct((m, GATE_PAD), F32), jax.ShapeDtypeStruct((m, GATE_PAD), F32),
                   jax.ShapeDtypeStruct((FOX_HEADS, m), F32)],
        scratch_shapes=[pltpu.VMEM((1, GATE_PAD), F32)],
        compiler_params=pltpu.CompilerParams(dimension_semantics=("arbitrary",)),
        name="fox_gates",
    )(f, bias)


FOX_TILE = 256


def _fox_body(q_ref, k_ref, v_ref, cc_ref, cr_ref, o_ref, kb_scr, vb_scr):
    hp = pl.program_id(1)
    i = pl.program_id(2)
    blk, d = FOX_TILE, HEAD_DIM
    heads = LANES // d

    @pl.when(i == 0)
    def _():
        for hh in range(heads):
            kb_scr[hh] = k_ref[:, hh * d:(hh + 1) * d].astype(BF16)
            vb_scr[hh] = v_ref[:, hh * d:(hh + 1) * d].astype(BF16)

    causal = _iota((blk, blk), 0) >= _iota((blk, blk), 1)
    lane = _iota((blk, GATE_PAD), 1)
    cc = cc_ref[...]
    outs = []
    for hh in range(heads):
        qb = (q_ref[:, hh * d:(hh + 1) * d] * (d ** -0.5)).astype(BF16)
        c_t = jnp.sum(jnp.where(lane == hp * heads + hh, cc, 0.0), axis=-1, keepdims=True)

        own = pl.multiple_of(i * blk, blk)
        s = _dot_nt(qb, kb_scr[hh, pl.ds(own, blk), :]) + c_t - cr_ref[hh, pl.ds(i, 1), :]
        s = jnp.where(causal, s, NEG_INF)
        m0 = jnp.max(s, axis=-1, keepdims=True)
        p = jnp.exp(s - m0)
        l0 = jnp.sum(p, axis=-1, keepdims=True)
        acc0 = _dot(p.astype(BF16), vb_scr[hh, pl.ds(own, blk), :])

        def past(j, carry, qb=qb, c_t=c_t, hh=hh):
            m, l, acc = carry
            start = pl.multiple_of(j * blk, blk)
            s = _dot_nt(qb, kb_scr[hh, pl.ds(start, blk), :]) + c_t - cr_ref[hh, pl.ds(j, 1), :]
            m_new = jnp.maximum(m, jnp.max(s, axis=-1, keepdims=True))
            alpha = jnp.exp(m - m_new)
            p = jnp.exp(s - m_new)
            l = alpha * l + jnp.sum(p, axis=-1, keepdims=True)
            acc = alpha * acc + _dot(p.astype(BF16), vb_scr[hh, pl.ds(start, blk), :])
            return m_new, l, acc

        m, l, acc = lax.fori_loop(0, i, past, (m0, l0, acc0))
        outs.append(acc / l)
    o_ref[...] = jnp.concatenate(outs, axis=-1)


def fox_prompt(q, k, v, c_col, c_row, batch, seq):
    width = FOX_HEADS * HEAD_DIM
    assert seq % FOX_TILE == 0
    nq = seq // FOX_TILE
    heads = LANES // HEAD_DIM
    n_pairs = width // LANES
    c_row4 = c_row.reshape(n_pairs, heads, batch * nq, FOX_TILE)
    return pl.pallas_call(
        _fox_body,
        grid=(batch, n_pairs, nq),
        in_specs=[pl.BlockSpec((FOX_TILE, LANES), lambda b, hp, i: (b * nq + i, hp)),
                  pl.BlockSpec((seq, LANES), lambda b, hp, i: (b, hp)),
                  pl.BlockSpec((seq, LANES), lambda b, hp, i: (b, hp)),
                  pl.BlockSpec((FOX_TILE, GATE_PAD), lambda b, hp, i: (b * nq + i, 0)),
                  pl.BlockSpec((None, heads, nq, FOX_TILE), lambda b, hp, i: (hp, 0, b, 0))],
        out_specs=pl.BlockSpec((FOX_TILE, LANES), lambda b, hp, i: (b * nq + i, hp)),
        out_shape=jax.ShapeDtypeStruct((batch * seq, width), F32),
        scratch_shapes=[pltpu.VMEM((heads, seq, HEAD_DIM), BF16), pltpu.VMEM((heads, seq, HEAD_DIM), BF16)],
        compiler_params=pltpu.CompilerParams(dimension_semantics=("arbitrary", "arbitrary", "arbitrary")),
        name="fox_prompt",
    )(q, k, v, c_col, c_row4)


def _block_diag_queries(q, n_heads, n_tok):
    rows, width = n_heads * n_tok, n_heads * HEAD_DIM
    tiled = jnp.concatenate([q] * n_heads, axis=0)
    same = (_iota((rows, width), 0) // n_tok) == (_iota((rows, width), 1) // HEAD_DIM)
    return jnp.where(same, tiled, 0.0)


def _head_diag(acc, n_heads, n_tok):
    width = n_heads * HEAD_DIM
    lane_head = _iota((n_tok, width), 1) // HEAD_DIM
    out = jnp.zeros((n_tok, width), F32)
    for h in range(n_heads):
        out = out + jnp.where(lane_head == h, acc[h * n_tok:(h + 1) * n_tok, :], 0.0)
    return out


def _moba_dec_body(pt_ref, q_ref, kn_ref, vn_ref, sl_ref, *refs, pages_per_step, page, n_tok, past_len):
    pp = pages_per_step
    k_refs, v_refs = refs[:pp], refs[pp:2 * pp]
    o_ref = refs[2 * pp]
    qbd_scr, km_scr, m_scr, l_scr, acc_scr = refs[2 * pp + 1:]
    s_id = pl.program_id(1)
    nh = MOBA_HEADS
    rows = nh * n_tok
    ppb = MOBA_BLOCK // page
    n_past = past_len // MOBA_BLOCK
    slope = sl_ref[...]
    tok = (_iota((rows, 1), 0) % n_tok).astype(F32)

    @pl.when(s_id == 0)
    def _():
        qbd_scr[...] = _block_diag_queries(q_ref[...], nh, n_tok)

    qbd = qbd_scr[...]
    qb = (qbd * (HEAD_DIM ** -0.5)).astype(BF16)
    for g in range(pp // ppb):
        blk_id = s_id * (pp // ppb) + g
        ss, ksum = [], None
        for u in range(ppb):
            kp = k_refs[g * ppb + u][...]
            part = jnp.sum(kp, axis=0, keepdims=True)
            ksum = part if ksum is None else ksum + part
            kpos = (blk_id * MOBA_BLOCK + u * page).astype(F32) + _iota((rows, page), 1).astype(F32)
            dist = (past_len + tok) - kpos
            ss.append(_dot_nt(qb, kp.astype(BF16)) - slope * dist)
        m = ss[0].max(axis=-1, keepdims=True)
        for s in ss[1:]:
            m = jnp.maximum(m, s.max(axis=-1, keepdims=True))
        l, acc = None, None
        for u, s in enumerate(ss):
            p = jnp.exp(s - m)
            lp = jnp.sum(p, axis=-1, keepdims=True)
            ap = _dot(p.astype(BF16), v_refs[g * ppb + u][...].astype(BF16))
            l = lp if l is None else l + lp
            acc = ap if acc is None else acc + ap
        km_scr[pl.ds(blk_id, 1), :] = ksum * (1.0 / MOBA_BLOCK)
        m_scr[blk_id] = m
        l_scr[blk_id] = l
        acc_scr[blk_id] = acc

    @pl.when(s_id == pl.num_programs(1) - 1)
    def _():
        gate = _dot_nt(qbd, km_scr[...], precision=HIGHEST)
        sel = _topk_blocks(gate, n_past, n_past)
        kn = kn_ref[...]
        t_k = _iota((rows, n_tok), 1).astype(F32)
        dist = tok - t_k
        s = _dot_nt(qb, kn.astype(BF16)) - slope * dist
        s = jnp.where(dist >= 0.0, s, NEG_INF)
        m_own = jnp.max(s, axis=-1, keepdims=True)
        p = jnp.exp(s - m_own)
        l_own = jnp.sum(p, axis=-1, keepdims=True)
        acc_own = _dot(p.astype(BF16), vn_ref[...].astype(BF16))
        m_tot = m_own
        for n in range(n_past):
            m_tot = jnp.maximum(m_tot, jnp.where(sel[:, n:n + 1] > 0.0, m_scr[n], NEG_INF))
        w_own = jnp.exp(m_own - m_tot)
        l_tot = w_own * l_own
        acc = w_own * acc_own
        for n in range(n_past):
            w = jnp.where(sel[:, n:n + 1] > 0.0, jnp.exp(m_scr[n] - m_tot), 0.0)
            l_tot = l_tot + w * l_scr[n]
            acc = acc + w * acc_scr[n]
        o_ref[...] = _head_diag(acc / l_tot, nh, n_tok)


def moba_decode(q, k_new, v_new, cache_k, cache_v, page_table, n_tok, pages_per_step):
    n_seq, n_pages = page_table.shape
    n_phys, page = cache_k.shape[0], cache_k.shape[1]
    nh = MOBA_HEADS
    width = nh * HEAD_DIM
    past_len = n_pages * page
    rows = nh * n_tok
    assert MOBA_BLOCK % page == 0 and past_len % MOBA_BLOCK == 0 and n_tok <= MOBA_BLOCK
    assert pages_per_step % (MOBA_BLOCK // page) == 0 and n_pages % pages_per_step == 0
    n_past = past_len // MOBA_BLOCK
    ck = cache_k.reshape(n_phys, page, width)
    cv = cache_v.reshape(n_phys, page, width)
    slope_rows = jnp.asarray(np.repeat(
        np.array([2.0 ** (-8.0 * (i + 1) / nh) for i in range(nh)], dtype=np.float32), n_tok).reshape(rows, 1))
    tok_spec = pl.BlockSpec((n_tok, width), lambda b, s, pt: (b, 0))

    def page_spec(u):
        return pl.BlockSpec((None, page, width), lambda b, s, pt: (pt[b, s * pages_per_step + u], 0, 0))

    grid_spec = pltpu.PrefetchScalarGridSpec(
        num_scalar_prefetch=1,
        grid=(n_seq, n_pages // pages_per_step),
        in_specs=[tok_spec, tok_spec, tok_spec, pl.BlockSpec((rows, 1), lambda b, s, pt: (0, 0))]
        + [page_spec(u) for u in range(pages_per_step)] * 2,
        out_specs=tok_spec,
        scratch_shapes=[pltpu.VMEM((rows, width), F32), pltpu.VMEM((n_past, width), F32),
                        pltpu.VMEM((n_past, rows, 1), F32), pltpu.VMEM((n_past, rows, 1), F32),
                        pltpu.VMEM((n_past, rows, width), F32)],
    )
    return pl.pallas_call(
        functools.partial(_moba_dec_body, pages_per_step=pages_per_step, page=page, n_tok=n_tok, past_len=past_len),
        grid_spec=grid_spec,
        out_shape=jax.ShapeDtypeStruct((n_seq * n_tok, width), F32),
        compiler_params=pltpu.CompilerParams(dimension_semantics=("arbitrary", "arbitrary")),
        name="moba_decode",
    )(page_table, q, k_new, v_new, slope_rows, *([ck] * pages_per_step), *([cv] * pages_per_step))


def _fox_dec_body(pt_ref, q_ref, kn_ref, vn_ref, f_ref, fb_ref, *refs, pages_per_step, page, n_tok):
    pp = pages_per_step
    k_refs, v_refs, lf_refs = refs[:pp], refs[pp:2 * pp], refs[2 * pp:3 * pp]
    o_ref, lfn_ref = refs[3 * pp], refs[3 * pp + 1]
    qb_scr, carry_scr, m_scr, l_scr, acc_scr = refs[3 * pp + 2:]
    s_id = pl.program_id(1)
    nh = FOX_HEADS
    rows = nh * n_tok
    expand = ((_iota((rows, nh), 0) // n_tok) == _iota((rows, nh), 1)).astype(F32)
    tril = (_iota((page, page), 0) >= _iota((page, page), 1)).astype(F32)

    @pl.when(s_id == 0)
    def _():
        qb_scr[...] = (_block_diag_queries(q_ref[...], nh, n_tok) * (HEAD_DIM ** -0.5)).astype(BF16)
        carry_scr[...] = jnp.zeros_like(carry_scr)
        m_scr[...] = jnp.full_like(m_scr, NEG_INF)
        l_scr[...] = jnp.zeros_like(l_scr)
        acc_scr[...] = jnp.zeros_like(acc_scr)

    qb = qb_scr[...]
    for u in range(pp):
        c_page = jnp.dot(tril, lf_refs[u][...], precision=HIGHEST, preferred_element_type=F32) + carry_scr[...]
        carry_scr[...] = c_page[page - 1:page, :]
        bias = _dot_nt(expand, c_page, precision=HIGHEST)
        s = _dot_nt(qb, k_refs[u][...].astype(BF16)) - bias
        m_prev = m_scr[...]
        m_new = jnp.maximum(m_prev, jnp.max(s, axis=-1, keepdims=True))
        alpha = jnp.exp(m_prev - m_new)
        p = jnp.exp(s - m_new)
        l_scr[...] = alpha * l_scr[...] + jnp.sum(p, axis=-1, keepdims=True)
        acc_scr[...] = alpha * acc_scr[...] + _dot(p.astype(BF16), v_refs[u][...].astype(BF16))
        m_scr[...] = m_new

    @pl.when(s_id == pl.num_programs(1) - 1)
    def _():
        lf_new = _log_sigmoid(f_ref[...] + fb_ref[...])
        lfn_ref[...] = lf_new
        tril_t = (_iota((n_tok, n_tok), 0) >= _iota((n_tok, n_tok), 1)).astype(F32)
        c_new = jnp.dot(tril_t, lf_new, precision=HIGHEST, preferred_element_type=F32)[:, :nh]
        c_new_rows = jnp.concatenate([c_new] * nh, axis=0)
        c_q = jnp.sum(expand * c_new_rows, axis=-1, keepdims=True)
        c_tot = jnp.sum(expand * carry_scr[...], axis=-1, keepdims=True)
        m_past = m_scr[...] + (c_tot + c_q)
        tok = _iota((rows, n_tok), 0) % n_tok
        t_k = _iota((rows, n_tok), 1)
        c_k = _dot_nt(expand, c_new, precision=HIGHEST)
        s = _dot_nt(qb, kn_ref[...].astype(BF16)) + c_q - c_k
        s = jnp.where(tok >= t_k, s, NEG_INF)
        m_own = jnp.max(s, axis=-1, keepdims=True)
        p = jnp.exp(s - m_own)
        l_own = jnp.sum(p, axis=-1, keepdims=True)
        acc_own = _dot(p.astype(BF16), vn_ref[...].astype(BF16))
        m_tot = jnp.maximum(m_past, m_own)
        w_past = jnp.exp(m_past - m_tot)
        w_own = jnp.exp(m_own - m_tot)
        l_tot = w_past * l_scr[...] + w_own * l_own
        acc = w_past * acc_scr[...] + w_own * acc_own
        o_ref[...] = _head_diag(acc / l_tot, nh, n_tok)


def fox_decode(q, k_new, v_new, f_new, f_bias, cache_k, cache_v, cache_lf, page_table, n_tok, pages_per_step):
    n_seq, n_pages = page_table.shape
    n_phys, page = cache_k.shape[0], cache_k.shape[1]
    nh = FOX_HEADS
    width = nh * HEAD_DIM
    rows = nh * n_tok
    assert n_pages % pages_per_step == 0
    ck = cache_k.reshape(n_phys, page, width)
    cv = cache_v.reshape(n_phys, page, width)
    tok_spec = pl.BlockSpec((n_tok, width), lambda b, s, pt: (b, 0))
    gate_spec = pl.BlockSpec((n_tok, GATE_PAD), lambda b, s, pt: (b, 0))

    def page_spec(u, w):
        return pl.BlockSpec((None, page, w), lambda b, s, pt: (pt[b, s * pages_per_step + u], 0, 0))

    grid_spec = pltpu.PrefetchScalarGridSpec(
        num_scalar_prefetch=1,
        grid=(n_seq, n_pages // pages_per_step),
        in_specs=[tok_spec, tok_spec, tok_spec, gate_spec, pl.BlockSpec((1, GATE_PAD), lambda b, s, pt: (0, 0))]
        + [page_spec(u, width) for u in range(pages_per_step)] * 2
        + [page_spec(u, nh) for u in range(pages_per_step)],
        out_specs=[tok_spec, gate_spec],
        scratch_shapes=[pltpu.VMEM((rows, width), BF16), pltpu.VMEM((1, nh), F32),
                        pltpu.VMEM((rows, 1), F32), pltpu.VMEM((rows, 1), F32), pltpu.VMEM((rows, width), F32)],
    )
    return pl.pallas_call(
        functools.partial(_fox_dec_body, pages_per_step=pages_per_step, page=page, n_tok=n_tok),
        grid_spec=grid_spec,
        out_shape=[jax.ShapeDtypeStruct((n_seq * n_tok, width), F32),
                   jax.ShapeDtypeStruct((n_seq * n_tok, GATE_PAD), F32)],
        compiler_params=pltpu.CompilerParams(dimension_semantics=("arbitrary", "arbitrary")),
        name="fox_decode",
    )(page_table, q, k_new, v_new, f_new, f_bias,
      *([ck] * pages_per_step), *([cv] * pages_per_step), *([cache_lf] * pages_per_step))


def _pad_lanes(w, width):
    return jnp.pad(w, ((0, 0), (0, width - w.shape[1])))


def _even_pieces(w_in):
    nq = MLSTM_HEADS * MLSTM_DQK
    nv = MLSTM_HEADS * MLSTM_DV
    nm = MOBA_HEADS * HEAD_DIM
    sizes = (nq, nq, nv, nv, 2 * MLSTM_HEADS, nm, nm, nm)
    pieces, start = [], 0
    for s in sizes:
        pieces.append(w_in[:, start:start + s])
        start += s
    assert start == w_in.shape[1]
    pieces[4] = _pad_lanes(pieces[4], GATE_PAD)
    return [p.astype(BF16) for p in pieces]


def _odd_pieces(w_in):
    nw = FOX_HEADS * HEAD_DIM
    pieces = [w_in[:, 0:nw], w_in[:, nw:2 * nw], w_in[:, 2 * nw:3 * nw], _pad_lanes(w_in[:, 3 * nw:], GATE_PAD)]
    assert w_in.shape[1] == 3 * nw + FOX_HEADS
    return [p.astype(BF16) for p in pieces]


PROMPT_TM = 512
FFN_TM = 1024
FFN_TH = 256
MLSTM_SUB = 2
FOX_GATE_TM = 512
MOBA_PAGES_PER_STEP = 4
FOX_PAGES_PER_STEP = 4


def kernel(x_prompt, x_sample, cache_moba_k, cache_moba_v, state_mlstm_C, state_mlstm_n, state_mlstm_m, cache_fox_k, cache_fox_v, cache_fox_logf, page_table, norm_mix_g, norm_ffn_g, norm_final_g, even_w_in, even_b_ig, even_b_fg, even_head_norm_g, even_w_out, odd_w_in, odd_b_f, odd_w_out, ffn_w_gate, ffn_w_up, ffn_w_down):
    bp, tp, d = x_prompt.shape
    bs, ts, _ = x_sample.shape
    depth = norm_mix_g.shape[0]
    mp, ms = bp * tp, bs * ts
    hp = x_prompt.reshape(mp, d)
    hs = x_sample.reshape(ms, d)
    tm_p = min(PROMPT_TM, mp)
    tm_s = ms
    ffn_tm_p = min(FFN_TM, mp)
    nv = MLSTM_HEADS * MLSTM_DV
    pe, se, po, so = [], [], [], []
    for layer in range(depth):
        li = layer // 2
        final = layer == depth - 1
        if layer % 2 == 0:
            ws = _even_pieces(even_w_in[li])
            gate_bias = _pad_lanes(jnp.concatenate([even_b_ig[li], even_b_fg[li]]).reshape(1, -1), GATE_PAD)
            w_out = even_w_out[li].astype(BF16)
            w_out_a, w_out_b = w_out[:nv], w_out[nv:]
            zeros = lambda *s: jnp.zeros(s, F32)
            q1, k1, v1, o1, gt, q2, k2, v2 = norm_proj(hp, norm_mix_g[layer], ws, tm_p)
            y1, c_p, n_p, m_p = mlstm(q1, k1, v1, o1, gt, gate_bias, even_head_norm_g[li],
                                      zeros(bp, MLSTM_HEADS, MLSTM_DQK, MLSTM_DV), zeros(bp, MLSTM_HEADS, MLSTM_DQK),
                                      zeros(bp, MLSTM_HEADS), bp, tp, MLSTM_SUB)
            y2 = moba_prompt(q2, k2, v2, bp, tp)
            hp = proj_res(hp, [y1, y2], [w_out_a, w_out_b], tm_p)
            pe.append((k2.reshape(bp, tp, MOBA_HEADS, HEAD_DIM), v2.reshape(bp, tp, MOBA_HEADS, HEAD_DIM), c_p, n_p, m_p))
            q1, k1, v1, o1, gt, q2, k2, v2 = norm_proj(hs, norm_mix_g[layer], ws, tm_s)
            y1, c_s, n_s, m_s = mlstm(q1, k1, v1, o1, gt, gate_bias, even_head_norm_g[li],
                                      state_mlstm_C[li], state_mlstm_n[li], state_mlstm_m[li], bs, ts, 1)
            y2 = moba_decode(q2, k2, v2, cache_moba_k[li], cache_moba_v[li], page_table, ts, MOBA_PAGES_PER_STEP)
            hs = proj_res(hs, [y1, y2], [w_out_a, w_out_b], tm_s)
            se.append((k2.reshape(bs, ts, MOBA_HEADS, HEAD_DIM), v2.reshape(bs, ts, MOBA_HEADS, HEAD_DIM), c_s, n_s, m_s))
        else:
            ws = _odd_pieces(odd_w_in[li])
            f_bias = _pad_lanes(odd_b_f[li].reshape(1, -1), GATE_PAD)
            w_out = odd_w_out[li].astype(BF16)
            q, k, v, f = norm_proj(hp, norm_mix_g[layer], ws, tm_p)
            lf, c_col, c_row = fox_gates(f, f_bias, tp, min(FOX_GATE_TM, tp))
            y = fox_prompt(q, k, v, c_col, c_row, bp, tp)
            hp = proj_res(hp, [y], [w_out], tm_p)
            po.append((k.reshape(bp, tp, FOX_HEADS, HEAD_DIM), v.reshape(bp, tp, FOX_HEADS, HEAD_DIM),
                       lf[:, :FOX_HEADS].reshape(bp, tp, FOX_HEADS)))
            q, k, v, f = norm_proj(hs, norm_mix_g[layer], ws, tm_s)
            y, lf = fox_decode(q, k, v, f, f_bias, cache_fox_k[li], cache_fox_v[li], cache_fox_logf[li],
                               page_table, ts, FOX_PAGES_PER_STEP)
            hs = proj_res(hs, [y], [w_out], tm_s)
            so.append((k.reshape(bs, ts, FOX_HEADS, HEAD_DIM), v.reshape(bs, ts, FOX_HEADS, HEAD_DIM),
                       lf[:, :FOX_HEADS].reshape(bs, ts, FOX_HEADS)))
        wg, wu, wd = ffn_w_gate[layer].astype(BF16), ffn_w_up[layer].astype(BF16), ffn_w_down[layer].astype(BF16)
        hp = ffn(hp, norm_ffn_g[layer], norm_final_g, wg, wu, wd, ffn_tm_p, FFN_TH, final)
        hs = ffn(hs, norm_ffn_g[layer], norm_final_g, wg, wu, wd, tm_s, FFN_TH, final)
    stack = lambda group, idx: jnp.stack([s[idx] for s in group])
    return (hp.reshape(bp, tp, d), hs.reshape(bs, ts, d),
            stack(pe, 0), stack(pe, 1), stack(pe, 2), stack(pe, 3), stack(pe, 4),
            stack(po, 0), stack(po, 1), stack(po, 2),
            stack(se, 0), stack(se, 1), stack(se, 2), stack(se, 3), stack(se, 4),
            stack(so, 0), stack(so, 1), stack(so, 2))
```

```python
import functools

import jax
import jax.numpy as jnp
import numpy as np
from jax import lax
from jax.experimental import pallas as pl
from jax.experimental.pallas import tpu as pltpu

F32 = jnp.float32
BF16 = jnp.bfloat16
HIGHEST = lax.Precision.HIGHEST

LANES = 128
SUBLANES = 8

HEAD_DIM = 64
MLSTM_HEADS = 4
MLSTM_DQK = 64
MLSTM_DV = 128
MLSTM_CHUNK = 64
MOBA_HEADS = 8
MOBA_BLOCK = 256
MOBA_TOPK = 3
FOX_HEADS = 16
RMS_EPS = 1e-6
GATE_PAD = LANES
HEADS_PER_TILE = LANES // HEAD_DIM

NEG_INF = float("-inf")
MASK_BIAS = -1e30
SPLIT_PIECES = 3


def _log_sigmoid(x):
    return -(jnp.maximum(-x, 0.0) + jnp.log1p(jnp.exp(-jnp.abs(x))))


def _rmsnorm(x, g):
    return x * lax.rsqrt(jnp.mean(x * x, axis=-1, keepdims=True) + RMS_EPS) * g


def _dot(a, b, precision=None):
    return jnp.dot(a, b, precision=precision, preferred_element_type=F32)


def _dot_nt(a, b, precision=None):
    return lax.dot_general(a, b, (((1,), (1,)), ((), ())), precision=precision, preferred_element_type=F32)


def _dot_tn(a, b):
    return lax.dot_general(a, b, (((0,), (0,)), ((), ())), preferred_element_type=F32)


def _iota(shape, dim):
    return lax.broadcasted_iota(jnp.int32, shape, dim)


def _bf16_pieces(x):
    pieces, rest = [], x
    for _ in range(SPLIT_PIECES):
        p = rest.astype(BF16).astype(F32)
        pieces.append(p)
        rest = rest - p
    return pieces


def _norm_proj_body(x_ref, g_ref, *refs, n_w):
    w_refs, o_refs = refs[:n_w], refs[n_w:]
    xb = _rmsnorm(x_ref[...], g_ref[...]).astype(BF16)
    for w_ref, o_ref in zip(w_refs, o_refs):
        o_ref[...] = _dot(xb, w_ref[...])


def norm_proj(x, g, ws, tm):
    m, d = x.shape
    assert m % tm == 0
    in_specs = [pl.BlockSpec((tm, d), lambda i: (i, 0)), pl.BlockSpec((1, d), lambda i: (0, 0))]
    in_specs += [pl.BlockSpec(w.shape, lambda i: (0, 0)) for w in ws]
    out_specs = [pl.BlockSpec((tm, w.shape[1]), lambda i: (i, 0)) for w in ws]
    out_shape = [jax.ShapeDtypeStruct((m, w.shape[1]), F32) for w in ws]
    return pl.pallas_call(
        functools.partial(_norm_proj_body, n_w=len(ws)),
        grid=(m // tm,), in_specs=in_specs, out_specs=out_specs, out_shape=out_shape,
        name="norm_proj",
    )(x, g.reshape(1, d), *ws)


def _proj_res_body(h_ref, *refs, n_a):
    a_refs, w_refs, o_ref = refs[:n_a], refs[n_a:2 * n_a], refs[2 * n_a]
    acc = h_ref[...]
    for a_ref, w_ref in zip(a_refs, w_refs):
        acc = acc + _dot(a_ref[...].astype(BF16), w_ref[...])
    o_ref[...] = acc


def proj_res(h, acts, ws, tm):
    m, d = h.shape
    assert m % tm == 0
    in_specs = [pl.BlockSpec((tm, d), lambda i: (i, 0))]
    in_specs += [pl.BlockSpec((tm, a.shape[1]), lambda i: (i, 0)) for a in acts]
    in_specs += [pl.BlockSpec(w.shape, lambda i: (0, 0)) for w in ws]
    return pl.pallas_call(
        functools.partial(_proj_res_body, n_a=len(acts)),
        grid=(m // tm,), in_specs=in_specs,
        out_specs=pl.BlockSpec((tm, d), lambda i: (i, 0)),
        out_shape=jax.ShapeDtypeStruct((m, d), F32),
        name="proj_res",
    )(h, *acts, *ws)


def _ffn_body(x_ref, g_ref, gf_ref, wg_ref, wu_ref, wd_ref, o_ref, xn_scr, acc_scr, *, final_norm):
    k = pl.program_id(1)

    @pl.when(k == 0)
    def _():
        xn_scr[...] = _rmsnorm(x_ref[...], g_ref[...]).astype(BF16)
        acc_scr[...] = jnp.zeros_like(acc_scr)

    xn = xn_scr[...]
    gate = _dot(xn, wg_ref[...])
    up = _dot(xn, wu_ref[...])
    act = (gate * jax.nn.sigmoid(gate) * up).astype(BF16)
    acc_scr[...] += _dot(act, wd_ref[...])

    @pl.when(k == pl.num_programs(1) - 1)
    def _():
        y = x_ref[...] + acc_scr[...]
        o_ref[...] = _rmsnorm(y, gf_ref[...]) if final_norm else y


def ffn(h, g, g_final, wg, wu, wd, tm, th, final_norm):
    m, d = h.shape
    hidden = wg.shape[1]
    assert m % tm == 0 and hidden % th == 0
    return pl.pallas_call(
        functools.partial(_ffn_body, final_norm=final_norm),
        grid=(m // tm, hidden // th),
        in_specs=[
            pl.BlockSpec((tm, d), lambda i, k: (i, 0)),
            pl.BlockSpec((1, d), lambda i, k: (0, 0)),
            pl.BlockSpec((1, d), lambda i, k: (0, 0)),
            pl.BlockSpec((d, th), lambda i, k: (0, k)),
            pl.BlockSpec((d, th), lambda i, k: (0, k)),
            pl.BlockSpec((th, d), lambda i, k: (k, 0)),
        ],
        out_specs=pl.BlockSpec((tm, d), lambda i, k: (i, 0)),
        out_shape=jax.ShapeDtypeStruct((m, d), F32),
        scratch_shapes=[pltpu.VMEM((tm, d), BF16), pltpu.VMEM((tm, d), F32)],
        compiler_params=pltpu.CompilerParams(dimension_semantics=("arbitrary", "arbitrary")),
        name="ffn",
    )(h, g.reshape(1, d), g_final.reshape(1, d), wg, wu, wd)


def _mlstm_body(q_ref, k_ref, v_ref, o_ref, g_ref, gb_ref, gh_ref, c0_ref, n0_ref, m0_ref,
                y_ref, c_ref, n_ref, m_ref, c_scr, n_scr, m_scr, *, chunk, n_sub):
    j = pl.program_id(1)
    nh, dk, dv = MLSTM_HEADS, MLSTM_DQK, MLSTM_DV

    @pl.when(j == 0)
    def _():
        c_scr[...] = c0_ref[...]
        n_scr[...] = n0_ref[...]
        m_scr[...] = m0_ref[...]

    rows = chunk * n_sub
    gpre = g_ref[...] + gb_ref[...]
    lane = _iota((rows, GATE_PAD), 1)
    act = jnp.where(lane < nh, gpre, _log_sigmoid(gpre))
    r_i = _iota((chunk, chunk), 0)
    c_i = _iota((chunk, chunk), 1)
    causal = r_i >= c_i
    eye = r_i == c_i
    tril = causal.astype(F32)

    def to_row(col):
        return jnp.sum(jnp.where(eye, col, 0.0), axis=0, keepdims=True)

    for s in range(n_sub):
        rs = slice(s * chunk, (s + 1) * chunk)
        a = act[rs]
        bcum = _dot(tril, a, precision=HIGHEST)
        for h in range(nh):
            q = q_ref[rs, h * dk:(h + 1) * dk] * (dk ** -0.5)
            k = k_ref[rs, h * dk:(h + 1) * dk]
            v = v_ref[rs, h * dv:(h + 1) * dv]
            qb, kb, vb = q.astype(BF16), k.astype(BF16), v.astype(BF16)
            ig_c = a[:, h:h + 1]
            b_c = bcum[:, nh + h:nh + h + 1]
            ig_r, b_r = to_row(ig_c), to_row(b_c)
            m_prev = m_scr[h]
            c_prev = c_scr[h]
            n_prev = n_scr[h]
            dmat = jnp.where(causal, b_c - b_r + ig_r, NEG_INF)
            inter = b_c + m_prev
            mt = jnp.maximum(inter, jnp.max(dmat, axis=-1, keepdims=True))
            w_intra = jnp.exp(dmat - mt)
            w_inter = jnp.exp(inter - mt)
            qk = _dot_nt(qb, kb) * w_intra
            num = w_inter * _dot(qb, c_prev.astype(BF16)) + _dot(qk.astype(BF16), vb)
            den = w_inter * jnp.sum(q * n_prev, axis=-1, keepdims=True) + jnp.sum(qk, axis=-1, keepdims=True)
            hh = num / jnp.maximum(jnp.abs(den), jnp.exp(-mt))
            m_new = mt[chunk - 1:chunk, :]
            b_last = b_c[chunk - 1:chunk, :]
            decay = jnp.exp(b_last + m_prev - m_new)
            w_s = jnp.exp(b_last - b_c + ig_c - m_new)
            kw = k * w_s
            c_scr[h] = decay * c_prev + _dot_tn(kw.astype(BF16), vb)
            n_scr[h] = decay * n_prev + jnp.sum(kw, axis=0, keepdims=True)
            m_scr[h] = m_new
            hn = hh * lax.rsqrt(jnp.mean(hh * hh, axis=-1, keepdims=True) + RMS_EPS)
            hn = hn * gh_ref[:, h * dv:(h + 1) * dv]
            y_ref[rs, h * dv:(h + 1) * dv] = jax.nn.sigmoid(o_ref[rs, h * dv:(h + 1) * dv]) * hn

    @pl.when(j == pl.num_programs(1) - 1)
    def _():
        c_ref[...] = c_scr[...]
        n_ref[...] = n_scr[...]
        m_ref[...] = m_scr[...]


def mlstm(q, k, v, o, gates, gate_bias, g_head, c0, n0, m0, batch, seq, n_sub):
    nh, dk, dv = MLSTM_HEADS, MLSTM_DQK, MLSTM_DV
    chunk = MLSTM_CHUNK if seq % MLSTM_CHUNK == 0 else seq
    rows = chunk * n_sub
    assert seq % rows == 0 and rows % SUBLANES == 0
    steps = seq // rows
    row_spec = lambda w: pl.BlockSpec((rows, w), lambda b, j: (b * steps + j, 0))
    st4 = lambda s: pl.BlockSpec((None,) + s, lambda b, j: (b, 0, 0, 0))
    y, c, n, m = pl.pallas_call(
        functools.partial(_mlstm_body, chunk=chunk, n_sub=n_sub),
        grid=(batch, steps),
        in_specs=[row_spec(nh * dk), row_spec(nh * dk), row_spec(nh * dv), row_spec(nh * dv), row_spec(GATE_PAD),
                  pl.BlockSpec((1, GATE_PAD), lambda b, j: (0, 0)),
                  pl.BlockSpec((1, nh * dv), lambda b, j: (0, 0)),
                  st4((nh, dk, dv)), st4((nh, 1, dk)), st4((nh, 1, 1))],
        out_specs=[row_spec(nh * dv), st4((nh, dk, dv)), st4((nh, 1, dk)), st4((nh, 1, 1))],
        out_shape=[jax.ShapeDtypeStruct((batch * seq, nh * dv), F32),
                   jax.ShapeDtypeStruct((batch, nh, dk, dv), F32),
                   jax.ShapeDtypeStruct((batch, nh, 1, dk), F32),
                   jax.ShapeDtypeStruct((batch, nh, 1, 1), F32)],
        scratch_shapes=[pltpu.VMEM((nh, dk, dv), F32), pltpu.VMEM((nh, 1, dk), F32), pltpu.VMEM((nh, 1, 1), F32)],
        compiler_params=pltpu.CompilerParams(dimension_semantics=("arbitrary", "arbitrary")),
        name="mlstm",
    )(q, k, v, o, gates, gate_bias, g_head.reshape(1, nh * dv),
      c0, n0.reshape(batch, nh, 1, dk), m0.reshape(batch, nh, 1, 1))
    return y, c, n.reshape(batch, nh, dk), m.reshape(batch, nh)


def _topk_blocks(gate, n_blocks, n_valid, axis):
    idx = _iota(gate.shape, axis)
    rank = jnp.zeros(gate.shape, F32)
    for c in range(n_blocks):
        other = gate[c:c + 1, :] if axis == 0 else gate[:, c:c + 1]
        beats = (other > gate) | ((other == gate) & (c < idx))
        candidate = jnp.where(c < n_valid, 1.0, 0.0)
        rank = rank + jnp.where(beats, candidate, 0.0)
    return (rank < MOBA_TOPK) & (idx < n_valid)


ATT_TILE = 512


def _data_lanes(lane, hh):
    return (lane >= hh * HEAD_DIM) & (lane < (hh + 1) * HEAD_DIM)


def _extra_lane0(hh):
    return (HEADS_PER_TILE - 1 - hh) * HEAD_DIM


def _flash_pair(qa, ka_scr, vb_scr, i, tile):
    causal = _iota((tile, tile), 0) >= _iota((tile, tile), 1)

    def step(start, carry, diagonal):
        vb = vb_scr[pl.ds(start, tile), :]
        new = []
        for hh in range(HEADS_PER_TILE):
            m, l, acc = carry[hh]
            s = _dot_nt(qa[hh], ka_scr[hh, pl.ds(start, tile), :])
            if diagonal:
                s = jnp.where(causal, s, NEG_INF)
            m_new = jnp.maximum(m, jnp.max(s, axis=-1, keepdims=True))
            alpha = jnp.exp(m - m_new)
            p = jnp.exp(s - m_new)
            l = alpha * l + jnp.sum(p, axis=-1, keepdims=True)
            acc = alpha * acc + _dot(p.astype(BF16), vb)
            new.append((m_new, l, acc))
        return tuple(new)

    init = tuple((jnp.full((tile, 1), NEG_INF, F32), jnp.zeros((tile, 1), F32), jnp.zeros((tile, LANES), F32))
                 for _ in range(HEADS_PER_TILE))
    carry = step(pl.multiple_of(i * tile, tile), init, True)
    carry = lax.fori_loop(0, i, lambda j, c: step(pl.multiple_of(j * tile, tile), c, False), carry)
    lane = _iota((tile, LANES), 1)
    out = jnp.zeros((tile, LANES), F32)
    for hh in range(HEADS_PER_TILE):
        _, l, acc = carry[hh]
        out = jnp.where(_data_lanes(lane, hh), acc / l, out)
    return out


def _alibi_digits(pos):
    return (pos // HEAD_DIM * HEAD_DIM).astype(F32), (pos % HEAD_DIM).astype(F32)


def _moba_body(q_ref, k_ref, v_ref, sl_ref, o_ref, ka_scr, vb_scr, km_scr, *, n_blocks, tile):
    i = pl.program_id(2)
    seq = k_ref.shape[0]
    d = HEAD_DIM
    a_off = n_blocks

    def alibi_cols(idx, on_pos, pos_hi, pos_lo, slope):
        cols = jnp.zeros(idx.shape, F32)
        for n, p in enumerate(_bf16_pieces(slope)):
            base = a_off + 4 * n
            mine, other = (pos_hi, pos_lo), (p, p)
            first, second = (other, mine) if on_pos == "key" else (mine, other)
            for c, val in enumerate(first + second):
                cols = jnp.where(idx == base + c, val, cols)
        return cols

    @pl.when(i == 0)
    def _():
        k = k_ref[...]
        vb_scr[...] = v_ref[...].astype(BF16)
        lane = _iota((seq, LANES), 1)
        pos = _iota((seq, LANES), 0)
        pos_hi, pos_lo = _alibi_digits(pos)
        km = jnp.concatenate(
            [jnp.mean(k[n * MOBA_BLOCK:(n + 1) * MOBA_BLOCK], axis=0, keepdims=True) for n in range(n_blocks)], axis=0)
        for hh in range(HEADS_PER_TILE):
            e0 = _extra_lane0(hh)
            idx = lane - e0
            slope = sl_ref[:, hh * d:hh * d + 1]
            extras = jnp.where(idx == pos // MOBA_BLOCK, 1.0, alibi_cols(idx, "key", pos_hi, pos_lo, slope))
            ka_scr[hh] = jnp.where(_data_lanes(lane, hh), k, extras).astype(BF16)
            km_scr[hh] = jnp.zeros((LANES, LANES), F32)
            km_scr[hh, e0:e0 + n_blocks, :] = km

    q = q_ref[...]
    lane = _iota((tile, LANES), 1)
    t_pos = i * tile + _iota((tile, LANES), 0)
    t_hi, t_lo = _alibi_digits(t_pos)
    blk_rows = -(-n_blocks // SUBLANES) * SUBLANES
    own_t = (i * tile + _iota((1, tile), 1)) // MOBA_BLOCK
    qa = []
    for hh in range(HEADS_PER_TILE):
        e0 = _extra_lane0(hh)
        idx = lane - e0
        data = _data_lanes(lane, hh)
        slope = sl_ref[:, hh * d:hh * d + 1]
        gate_t = _dot_nt(km_scr[hh], jnp.where(data, q, 0.0), precision=HIGHEST)[e0:e0 + blk_rows]
        keep_t = _topk_blocks(gate_t, n_blocks, own_t, axis=0) | (_iota((blk_rows, tile), 0) == own_t)
        above, below = e0, LANES - e0 - blk_rows
        bias_t = ([jnp.zeros((above, tile), F32)] if above else []) + [jnp.where(keep_t, 0.0, MASK_BIAS)]
        bias_t += [jnp.zeros((below, tile), F32)] if below else []
        block_bias = jnp.concatenate(bias_t, axis=0).T
        extras = jnp.where((idx >= 0) & (idx < n_blocks), block_bias, alibi_cols(idx, "query", -t_hi, -t_lo, slope))
        qa.append(jnp.where(data, q * (d ** -0.5), extras).astype(BF16))
    o_ref[...] = _flash_pair(qa, ka_scr, vb_scr, i, tile)


def _alibi_lanes(n_heads):
    slopes = np.array([2.0 ** (-8.0 * (i + 1) / n_heads) for i in range(n_heads)], dtype=np.float32)
    return jnp.asarray(np.repeat(slopes, HEAD_DIM).reshape(1, n_heads * HEAD_DIM))


def moba_prompt(q, k, v, batch, seq):
    width = MOBA_HEADS * HEAD_DIM
    tile = min(ATT_TILE, seq)
    assert seq % tile == 0 and tile % MOBA_BLOCK == 0
    n_blocks = seq // MOBA_BLOCK
    assert n_blocks + 4 * SPLIT_PIECES <= HEAD_DIM
    nq = seq // tile
    return pl.pallas_call(
        functools.partial(_moba_body, n_blocks=n_blocks, tile=tile),
        grid=(batch, width // LANES, nq),
        in_specs=[pl.BlockSpec((tile, LANES), lambda b, hp, i: (b * nq + i, hp)),
                  pl.BlockSpec((seq, LANES), lambda b, hp, i: (b, hp)),
                  pl.BlockSpec((seq, LANES), lambda b, hp, i: (b, hp)),
                  pl.BlockSpec((1, LANES), lambda b, hp, i: (0, hp))],
        out_specs=pl.BlockSpec((tile, LANES), lambda b, hp, i: (b * nq + i, hp)),
        out_shape=jax.ShapeDtypeStruct((batch * seq, width), F32),
        scratch_shapes=[pltpu.VMEM((HEADS_PER_TILE, seq, LANES), BF16), pltpu.VMEM((seq, LANES), BF16),
                        pltpu.VMEM((HEADS_PER_TILE, LANES, LANES), F32)],
        compiler_params=pltpu.CompilerParams(dimension_semantics=("arbitrary", "arbitrary", "arbitrary")),
        name="moba_prompt",
    )(q, k, v, _alibi_lanes(MOBA_HEADS))


def _fox_gate_body(f_ref, b_ref, lf_ref, cc_ref, carry_scr, *, tiles_per_seq):
    i = pl.program_id(0)
    tm = f_ref.shape[0]

    @pl.when(i % tiles_per_seq == 0)
    def _():
        carry_scr[...] = jnp.zeros_like(carry_scr)

    lf = _log_sigmoid(f_ref[...] + b_ref[...])
    lf_ref[...] = lf
    tril = (_iota((tm, tm), 0) >= _iota((tm, tm), 1)).astype(F32)
    c = _dot(tril, lf, precision=HIGHEST) + carry_scr[...]
    carry_scr[...] = c[tm - 1:tm, :]
    cc_ref[...] = c


def fox_gates(f, bias, seq, tm):
    m = f.shape[0]
    assert seq % tm == 0 and m % seq == 0
    return pl.pallas_call(
        functools.partial(_fox_gate_body, tiles_per_seq=seq // tm),
        grid=(m // tm,),
        in_specs=[pl.BlockSpec((tm, GATE_PAD), lambda i: (i, 0)), pl.BlockSpec((1, GATE_PAD), lambda i: (0, 0))],
        out_specs=[pl.BlockSpec((tm, GATE_PAD), lambda i: (i, 0)), pl.BlockSpec((tm, GATE_PAD), lambda i: (i, 0))],
        out_shape=[jax.ShapeDtypeStruct((m, GATE_PAD), F32), jax.ShapeDtypeStruct((m, GATE_PAD), F32)],
        scratch_shapes=[pltpu.VMEM((1, GATE_PAD), F32)],
        compiler_params=pltpu.CompilerParams(dimension_semantics=("arbitrary",)),
        name="fox_gates",
    )(f, bias)


def _fox_body(q_ref, k_ref, v_ref, cq_ref, ck_ref, o_ref, ka_scr, vb_scr, *, tile):
    hp = pl.program_id(1)
    i = pl.program_id(2)
    seq = k_ref.shape[0]
    d = HEAD_DIM

    def head_column(c_all, hh):
        lane_g = _iota(c_all.shape, 1)
        return jnp.sum(jnp.where(lane_g == hp * HEADS_PER_TILE + hh, c_all, 0.0), axis=-1, keepdims=True)

    def sum_cols(idx, c_col, on_pos):
        cols = jnp.zeros(idx.shape, F32)
        for n, p in enumerate(_bf16_pieces(c_col)):
            key_side = on_pos == "key"
            cols = jnp.where(idx == n, 1.0 if key_side else p, cols)
            cols = jnp.where(idx == SPLIT_PIECES + n, -p if key_side else 1.0, cols)
        return cols

    @pl.when(i == 0)
    def _():
        k = k_ref[...]
        vb_scr[...] = v_ref[...].astype(BF16)
        lane = _iota((seq, LANES), 1)
        c_all = ck_ref[...]
        for hh in range(HEADS_PER_TILE):
            extras = sum_cols(lane - _extra_lane0(hh), head_column(c_all, hh), "key")
            ka_scr[hh] = jnp.where(_data_lanes(lane, hh), k, extras).astype(BF16)

    q = q_ref[...]
    lane = _iota((tile, LANES), 1)
    c_all = cq_ref[...]
    qa = []
    for hh in range(HEADS_PER_TILE):
        extras = sum_cols(lane - _extra_lane0(hh), head_column(c_all, hh), "query")
        qa.append(jnp.where(_data_lanes(lane, hh), q * (d ** -0.5), extras).astype(BF16))
    o_ref[...] = _flash_pair(qa, ka_scr, vb_scr, i, tile)


def fox_prompt(q, k, v, c_col, batch, seq):
    width = FOX_HEADS * HEAD_DIM
    tile = min(ATT_TILE, seq)
    assert seq % tile == 0 and 2 * SPLIT_PIECES <= HEAD_DIM
    nq = seq // tile
    return pl.pallas_call(
        functools.partial(_fox_body, tile=tile),
        grid=(batch, width // LANES, nq),
        in_specs=[pl.BlockSpec((tile, LANES), lambda b, hp, i: (b * nq + i, hp)),
                  pl.BlockSpec((seq, LANES), lambda b, hp, i: (b, hp)),
                  pl.BlockSpec((seq, LANES), lambda b, hp, i: (b, hp)),
                  pl.BlockSpec((tile, GATE_PAD), lambda b, hp, i: (b * nq + i, 0)),
                  pl.BlockSpec((seq, GATE_PAD), lambda b, hp, i: (b, 0))],
        out_specs=pl.BlockSpec((tile, LANES), lambda b, hp, i: (b * nq + i, hp)),
        out_shape=jax.ShapeDtypeStruct((batch * seq, width), F32),
        scratch_shapes=[pltpu.VMEM((HEADS_PER_TILE, seq, LANES), BF16), pltpu.VMEM((seq, LANES), BF16)],
        compiler_params=pltpu.CompilerParams(dimension_semantics=("arbitrary", "arbitrary", "arbitrary")),
        name="fox_prompt",
    )(q, k, v, c_col, c_col)


def _block_diag_queries(q, n_heads, n_tok):
    rows, width = n_heads * n_tok, n_heads * HEAD_DIM
    tiled = jnp.concatenate([q] * n_heads, axis=0)
    same = (_iota((rows, width), 0) // n_tok) == (_iota((rows, width), 1) // HEAD_DIM)
    return jnp.where(same, tiled, 0.0)


def _head_diag(acc, n_heads, n_tok):
    width = n_heads * HEAD_DIM
    lane_head = _iota((n_tok, width), 1) // HEAD_DIM
    out = jnp.zeros((n_tok, width), F32)
    for h in range(n_heads):
        out = jnp.where(lane_head == h, acc[h * n_tok:(h + 1) * n_tok, :], out)
    return out


def _new_token_partial(qb, kn, vn, bias, n_tok):
    rows = qb.shape[0]
    s = _dot_nt(qb, kn.astype(BF16)) + bias
    tok = _iota((rows, n_tok), 0) % n_tok
    s = jnp.where(tok >= _iota((rows, n_tok), 1), s, NEG_INF)
    m = jnp.max(s, axis=-1, keepdims=True)
    p = jnp.exp(s - m)
    l = jnp.sum(p, axis=-1, keepdims=True)
    return m, l, _dot(p.astype(BF16), vn.astype(BF16))


def _moba_dec_body(pt_ref, q_ref, kn_ref, vn_ref, sl_ref, *refs, pages_per_step, page, n_tok, past_len):
    pp = pages_per_step
    k_refs, v_refs = refs[:pp], refs[pp:2 * pp]
    o_ref = refs[2 * pp]
    qb_scr, g_scr, m_scr, l_scr, acc_scr = refs[2 * pp + 1:]
    s_id = pl.program_id(1)
    nh, d = MOBA_HEADS, HEAD_DIM
    rows = nh * n_tok
    ppb = MOBA_BLOCK // page
    n_past = past_len // MOBA_BLOCK
    scale = d ** -0.5
    slope = sl_ref[...]
    tok = (_iota((rows, 1), 0) % n_tok).astype(F32)
    lane = _iota((rows, LANES), 1)

    @pl.when(s_id == 0)
    def _():
        qb_scr[...] = (_block_diag_queries(q_ref[...], nh, n_tok) * scale).astype(BF16)
        g_scr[...] = jnp.zeros_like(g_scr)
        m_scr[...] = jnp.zeros_like(m_scr)
        l_scr[...] = jnp.zeros_like(l_scr)

    qb = qb_scr[...]
    for g in range(pp // ppb):
        blk_id = s_id * (pp // ppb) + g
        ss, qk_sum = [], None
        for u in range(ppb):
            s = _dot(qb, k_refs[g * ppb + u][...].astype(BF16))
            r = jnp.sum(s, axis=-1, keepdims=True)
            qk_sum = r if qk_sum is None else qk_sum + r
            kpos = (blk_id * MOBA_BLOCK + u * page).astype(F32) + _iota((rows, page), 1).astype(F32)
            ss.append(s - slope * ((past_len + tok) - kpos))
        m = ss[0].max(axis=-1, keepdims=True)
        for s in ss[1:]:
            m = jnp.maximum(m, s.max(axis=-1, keepdims=True))
        l, acc = None, None
        for u, s in enumerate(ss):
            p = jnp.exp(s - m)
            lp = jnp.sum(p, axis=-1, keepdims=True)
            ap = _dot_nt(p.astype(BF16), v_refs[g * ppb + u][...].astype(BF16))
            l = lp if l is None else l + lp
            acc = ap if acc is None else acc + ap
        gate = qk_sum * (1.0 / (scale * MOBA_BLOCK))
        g_scr[...] = jnp.where(lane == blk_id, gate, g_scr[...])
        m_scr[...] = jnp.where(lane == blk_id, m, m_scr[...])
        l_scr[...] = jnp.where(lane == blk_id, l, l_scr[...])
        acc_scr[blk_id] = acc

    @pl.when(s_id == pl.num_programs(1) - 1)
    def _():
        sel = _topk_blocks(g_scr[...], n_past, n_past, axis=1)
        m_all = m_scr[...]
        t_k = _iota((rows, n_tok), 1).astype(F32)
        m_own, l_own, acc_own = _new_token_partial(qb, kn_ref[...], vn_ref[...], -slope * (tok - t_k), n_tok)
        m_tot = jnp.maximum(m_own, jnp.max(jnp.where(sel, m_all, NEG_INF), axis=-1, keepdims=True))
        w = jnp.where(sel, jnp.exp(m_all - m_tot), 0.0)
        w_own = jnp.exp(m_own - m_tot)
        l_tot = w_own * l_own + jnp.sum(w * l_scr[...], axis=-1, keepdims=True)
        acc = w_own * acc_own
        for n in range(n_past):
            acc = acc + w[:, n:n + 1] * acc_scr[n]
        o_ref[...] = _head_diag(acc / l_tot, nh, n_tok)


def _pages_position_minor(cache):
    n_phys, page, heads, d = cache.shape
    return jnp.transpose(cache, (0, 2, 3, 1)).reshape(n_phys, heads * d, page)


def moba_decode(q, k_new, v_new, cache_k, cache_v, page_table, n_tok, pages_per_step):
    n_seq, n_pages = page_table.shape
    page = cache_k.shape[1]
    nh, d = MOBA_HEADS, HEAD_DIM
    width = nh * d
    past_len = n_pages * page
    rows = nh * n_tok
    assert MOBA_BLOCK % page == 0 and past_len % MOBA_BLOCK == 0 and n_tok <= MOBA_BLOCK
    assert pages_per_step % (MOBA_BLOCK // page) == 0 and n_pages % pages_per_step == 0
    n_past = past_len // MOBA_BLOCK
    ck = _pages_position_minor(cache_k)
    cv = _pages_position_minor(cache_v)
    slope_rows = jnp.asarray(np.repeat(
        np.array([2.0 ** (-8.0 * (i + 1) / nh) for i in range(nh)], dtype=np.float32), n_tok).reshape(rows, 1))
    tok_spec = pl.BlockSpec((n_tok, width), lambda b, s, pt: (b, 0))

    assert n_past <= LANES

    def page_spec(u):
        return pl.BlockSpec((None, width, page), lambda b, s, pt: (pt[b, s * pages_per_step + u], 0, 0))

    grid_spec = pltpu.PrefetchScalarGridSpec(
        num_scalar_prefetch=1,
        grid=(n_seq, n_pages // pages_per_step),
        in_specs=[tok_spec, tok_spec, tok_spec, pl.BlockSpec((rows, 1), lambda b, s, pt: (0, 0))]
        + [page_spec(u) for u in range(pages_per_step)] * 2,
        out_specs=tok_spec,
        scratch_shapes=[pltpu.VMEM((rows, width), BF16), pltpu.VMEM((rows, LANES), F32),
                        pltpu.VMEM((rows, LANES), F32), pltpu.VMEM((rows, LANES), F32),
                        pltpu.VMEM((n_past, rows, width), F32)],
    )
    return pl.pallas_call(
        functools.partial(_moba_dec_body, pages_per_step=pages_per_step, page=page, n_tok=n_tok, past_len=past_len),
        grid_spec=grid_spec,
        out_shape=jax.ShapeDtypeStruct((n_seq * n_tok, width), F32),
        compiler_params=pltpu.CompilerParams(dimension_semantics=("arbitrary", "arbitrary")),
        name="moba_decode",
    )(page_table, q, k_new, v_new, slope_rows, *([ck] * pages_per_step), *([cv] * pages_per_step))


def _fox_dec_body(pt_ref, q_ref, kn_ref, vn_ref, f_ref, fb_ref, *refs, pages_per_step, page, n_tok):
    pp = pages_per_step
    k_refs, v_refs, lf_refs = refs[:pp], refs[pp:2 * pp], refs[2 * pp:3 * pp]
    o_ref, lfn_ref = refs[3 * pp], refs[3 * pp + 1]
    qb_scr, carry_scr, ctot_scr, m_scr, l_scr, acc_scr = refs[3 * pp + 2:]
    s_id = pl.program_id(1)
    nh, d = FOX_HEADS, HEAD_DIM
    rows = nh * n_tok
    expand = ((_iota((rows, nh), 0) // n_tok) == _iota((rows, nh), 1)).astype(F32)
    triu = (_iota((page, page), 0) <= _iota((page, page), 1)).astype(F32)

    @pl.when(s_id == 0)
    def _():
        qb_scr[...] = (_block_diag_queries(q_ref[...], nh, n_tok) * (d ** -0.5)).astype(BF16)
        carry_scr[...] = jnp.zeros_like(carry_scr)
        m_scr[...] = jnp.full_like(m_scr, NEG_INF)
        l_scr[...] = jnp.zeros_like(l_scr)
        acc_scr[...] = jnp.zeros_like(acc_scr)

    qb = qb_scr[...]
    ss = []
    offset = carry_scr[...]
    for u in range(pp):
        c_heads = _dot(lf_refs[u][...], triu, precision=HIGHEST) + offset
        offset = c_heads[:, page - 1:page]
        c_rows = _dot(expand, c_heads, precision=HIGHEST)
        ss.append(_dot(qb, k_refs[u][...].astype(BF16)) - c_rows)
    carry_scr[...] = offset
    ctot_scr[...] = c_rows[:, page - 1:page]
    m_prev = m_scr[...]
    m_new = m_prev
    for s in ss:
        m_new = jnp.maximum(m_new, jnp.max(s, axis=-1, keepdims=True))
    alpha = jnp.exp(m_prev - m_new)
    l_new = alpha * l_scr[...]
    pv = None
    for u, s in enumerate(ss):
        p = jnp.exp(s - m_new)
        l_new = l_new + jnp.sum(p, axis=-1, keepdims=True)
        part = _dot_nt(p.astype(BF16), v_refs[u][...].astype(BF16))
        pv = part if pv is None else pv + part
    acc_scr[...] = alpha * acc_scr[...] + pv
    l_scr[...] = l_new
    m_scr[...] = m_new

    @pl.when(s_id == pl.num_programs(1) - 1)
    def _():
        lf_new = _log_sigmoid(f_ref[...] + fb_ref[...])
        lfn_ref[...] = lf_new
        tril_t = (_iota((n_tok, n_tok), 0) >= _iota((n_tok, n_tok), 1)).astype(F32)
        c_new = _dot(tril_t, lf_new, precision=HIGHEST)[:, :nh]
        c_new_rows = jnp.concatenate([c_new] * nh, axis=0)
        c_q = jnp.sum(expand * c_new_rows, axis=-1, keepdims=True)
        m_past = m_scr[...] + (ctot_scr[...] + c_q)
        c_k = _dot_nt(expand, c_new, precision=HIGHEST)
        m_own, l_own, acc_own = _new_token_partial(qb, kn_ref[...], vn_ref[...], c_q - c_k, n_tok)
        m_tot = jnp.maximum(m_past, m_own)
        w_past = jnp.exp(m_past - m_tot)
        w_own = jnp.exp(m_own - m_tot)
        l_tot = w_past * l_scr[...] + w_own * l_own
        acc = w_past * acc_scr[...] + w_own * acc_own
        o_ref[...] = _head_diag(acc / l_tot, nh, n_tok)


def fox_decode(q, k_new, v_new, f_new, f_bias, cache_k, cache_v, cache_lf, page_table, n_tok, pages_per_step):
    n_seq, n_pages = page_table.shape
    page = cache_k.shape[1]
    nh, d = FOX_HEADS, HEAD_DIM
    width = nh * d
    rows = nh * n_tok
    assert n_pages % pages_per_step == 0
    ck = _pages_position_minor(cache_k)
    cv = _pages_position_minor(cache_v)
    clf = jnp.transpose(cache_lf, (0, 2, 1))
    tok_spec = pl.BlockSpec((n_tok, width), lambda b, s, pt: (b, 0))
    gate_spec = pl.BlockSpec((n_tok, GATE_PAD), lambda b, s, pt: (b, 0))

    def page_spec(u, dims):
        return pl.BlockSpec((None,) + dims, lambda b, s, pt: (pt[b, s * pages_per_step + u],) + (0,) * len(dims))

    grid_spec = pltpu.PrefetchScalarGridSpec(
        num_scalar_prefetch=1,
        grid=(n_seq, n_pages // pages_per_step),
        in_specs=[tok_spec, tok_spec, tok_spec, gate_spec, pl.BlockSpec((1, GATE_PAD), lambda b, s, pt: (0, 0))]
        + [page_spec(u, (width, page)) for u in range(pages_per_step)] * 2
        + [page_spec(u, (nh, page)) for u in range(pages_per_step)],
        out_specs=[tok_spec, gate_spec],
        scratch_shapes=[pltpu.VMEM((rows, width), BF16), pltpu.VMEM((nh, 1), F32), pltpu.VMEM((rows, 1), F32),
                        pltpu.VMEM((rows, 1), F32), pltpu.VMEM((rows, 1), F32), pltpu.VMEM((rows, width), F32)],
    )
    return pl.pallas_call(
        functools.partial(_fox_dec_body, pages_per_step=pages_per_step, page=page, n_tok=n_tok),
        grid_spec=grid_spec,
        out_shape=[jax.ShapeDtypeStruct((n_seq * n_tok, width), F32),
                   jax.ShapeDtypeStruct((n_seq * n_tok, GATE_PAD), F32)],
        compiler_params=pltpu.CompilerParams(dimension_semantics=("arbitrary", "arbitrary")),
        name="fox_decode",
    )(page_table, q, k_new, v_new, f_new, f_bias,
      *([ck] * pages_per_step), *([cv] * pages_per_step), *([clf] * pages_per_step))


def _pad_lanes(w, width):
    return jnp.pad(w, ((0, 0), (0, width - w.shape[1])))


def _even_pieces(w_in):
    nq = MLSTM_HEADS * MLSTM_DQK
    nv = MLSTM_HEADS * MLSTM_DV
    nm = MOBA_HEADS * HEAD_DIM
    sizes = (nq, nq, nv, nv, 2 * MLSTM_HEADS, nm, nm, nm)
    pieces, start = [], 0
    for s in sizes:
        pieces.append(w_in[:, start:start + s])
        start += s
    assert start == w_in.shape[1]
    pieces[4] = _pad_lanes(pieces[4], GATE_PAD)
    return [p.astype(BF16) for p in pieces]


def _odd_pieces(w_in):
    nw = FOX_HEADS * HEAD_DIM
    pieces = [w_in[:, 0:nw], w_in[:, nw:2 * nw], w_in[:, 2 * nw:3 * nw], _pad_lanes(w_in[:, 3 * nw:], GATE_PAD)]
    assert w_in.shape[1] == 3 * nw + FOX_HEADS
    return [p.astype(BF16) for p in pieces]


PROMPT_TM = 512
FFN_TM = 1024
FFN_TH = 256
MLSTM_SUB = 2
FOX_GATE_TM = 512
MOBA_PAGES_PER_STEP = 8
FOX_PAGES_PER_STEP = 4


def kernel(x_prompt, x_sample, cache_moba_k, cache_moba_v, state_mlstm_C, state_mlstm_n, state_mlstm_m, cache_fox_k, cache_fox_v, cache_fox_logf, page_table, norm_mix_g, norm_ffn_g, norm_final_g, even_w_in, even_b_ig, even_b_fg, even_head_norm_g, even_w_out, odd_w_in, odd_b_f, odd_w_out, ffn_w_gate, ffn_w_up, ffn_w_down):
    bp, tp, d = x_prompt.shape
    bs, ts, _ = x_sample.shape
    depth = norm_mix_g.shape[0]
    mp, ms = bp * tp, bs * ts
    hp = x_prompt.reshape(mp, d)
    hs = x_sample.reshape(ms, d)
    tm_p = min(PROMPT_TM, mp)
    tm_s = ms
    ffn_tm_p = min(FFN_TM, mp)
    nv = MLSTM_HEADS * MLSTM_DV
    n_pages = page_table.shape[1]
    pe, se, po, so = [], [], [], []
    for layer in range(depth):
        li = layer // 2
        final = layer == depth - 1
        if layer % 2 == 0:
            ws = _even_pieces(even_w_in[li])
            gate_bias = _pad_lanes(jnp.concatenate([even_b_ig[li], even_b_fg[li]]).reshape(1, -1), GATE_PAD)
            w_out = even_w_out[li].astype(BF16)
            w_out_a, w_out_b = w_out[:nv], w_out[nv:]
            zeros = lambda *s: jnp.zeros(s, F32)
            q1, k1, v1, o1, gt, q2, k2, v2 = norm_proj(hp, norm_mix_g[layer], ws, tm_p)
            y1, c_p, n_p, m_p = mlstm(q1, k1, v1, o1, gt, gate_bias, even_head_norm_g[li],
                                      zeros(bp, MLSTM_HEADS, MLSTM_DQK, MLSTM_DV), zeros(bp, MLSTM_HEADS, MLSTM_DQK),
                                      zeros(bp, MLSTM_HEADS), bp, tp, MLSTM_SUB)
            y2 = moba_prompt(q2, k2, v2, bp, tp)
            hp = proj_res(hp, [y1, y2], [w_out_a, w_out_b], tm_p)
            pe.append((k2.reshape(bp, tp, MOBA_HEADS, HEAD_DIM), v2.reshape(bp, tp, MOBA_HEADS, HEAD_DIM), c_p, n_p, m_p))
            q1, k1, v1, o1, gt, q2, k2, v2 = norm_proj(hs, norm_mix_g[layer], ws, tm_s)
            y1, c_s, n_s, m_s = mlstm(q1, k1, v1, o1, gt, gate_bias, even_head_norm_g[li],
                                      state_mlstm_C[li], state_mlstm_n[li], state_mlstm_m[li], bs, ts, 1)
            y2 = moba_decode(q2, k2, v2, cache_moba_k[li], cache_moba_v[li], page_table, ts,
                             min(MOBA_PAGES_PER_STEP, n_pages))
            hs = proj_res(hs, [y1, y2], [w_out_a, w_out_b], tm_s)
            se.append((k2.reshape(bs, ts, MOBA_HEADS, HEAD_DIM), v2.reshape(bs, ts, MOBA_HEADS, HEAD_DIM), c_s, n_s, m_s))
        else:
            ws = _odd_pieces(odd_w_in[li])
            f_bias = _pad_lanes(odd_b_f[li].reshape(1, -1), GATE_PAD)
            w_out = odd_w_out[li].astype(BF16)
            q, k, v, f = norm_proj(hp, norm_mix_g[layer], ws, tm_p)
            lf, c_col = fox_gates(f, f_bias, tp, min(FOX_GATE_TM, tp))
            y = fox_prompt(q, k, v, c_col, bp, tp)
            hp = proj_res(hp, [y], [w_out], tm_p)
            po.append((k.reshape(bp, tp, FOX_HEADS, HEAD_DIM), v.reshape(bp, tp, FOX_HEADS, HEAD_DIM),
                       lf[:, :FOX_HEADS].reshape(bp, tp, FOX_HEADS)))
            q, k, v, f = norm_proj(hs, norm_mix_g[layer], ws, tm_s)
            y, lf = fox_decode(q, k, v, f, f_bias, cache_fox_k[li], cache_fox_v[li], cache_fox_logf[li],
                               page_table, ts, min(FOX_PAGES_PER_STEP, n_pages))
            hs = proj_res(hs, [y], [w_out], tm_s)
            so.append((k.reshape(bs, ts, FOX_HEADS, HEAD_DIM), v.reshape(bs, ts, FOX_HEADS, HEAD_DIM),
                       lf[:, :FOX_HEADS].reshape(bs, ts, FOX_HEADS)))
        wg, wu, wd = ffn_w_gate[layer].astype(BF16), ffn_w_up[layer].astype(BF16), ffn_w_down[layer].astype(BF16)
        hp = ffn(hp, norm_ffn_g[layer], norm_final_g, wg, wu, wd, ffn_tm_p, FFN_TH, final)
        hs = ffn(hs, norm_ffn_g[layer], norm_final_g, wg, wu, wd, tm_s, FFN_TH, final)
    stack = lambda group, idx: jnp.stack([s[idx] for s in group])
    return (hp.reshape(bp, tp, d), hs.reshape(bs, ts, d),
            stack(pe, 0), stack(pe, 1), stack(pe, 2), stack(pe, 3), stack(pe, 4),
            stack(po, 0), stack(po, 1), stack(po, 2),
            stack(se, 0), stack(se, 1), stack(se, 2), stack(se, 3), stack(se, 4),
            stack(so, 0), stack(so, 1), stack(so, 2))
```

```python
import functools
import math

import jax
import jax.numpy as jnp
import numpy as np
from jax import lax
from jax.experimental import pallas as pl
from jax.experimental.pallas import tpu as pltpu

F32 = jnp.float32
BF16 = jnp.bfloat16
HIGHEST = lax.Precision.HIGHEST

LANES = 128
SUBLANES = 8

HEAD_DIM = 64
MLSTM_HEADS = 4
MLSTM_DQK = 64
MLSTM_DV = 128
MLSTM_CHUNK = 64
MOBA_HEADS = 8
MOBA_BLOCK = 256
MOBA_TOPK = 3
FOX_HEADS = 16
RMS_EPS = 1e-6
GATE_PAD = LANES
HEADS_PER_TILE = LANES // HEAD_DIM

NEG_INF = float("-inf")
MASK_BIAS = -1e30
SPLIT_PIECES = 3


def _log_sigmoid(x):
    return -(jnp.maximum(-x, 0.0) + jnp.log1p(jnp.exp(-jnp.abs(x))))


def _rmsnorm(x, g):
    return x * lax.rsqrt(jnp.mean(x * x, axis=-1, keepdims=True) + RMS_EPS) * g


def _dot(a, b, precision=None):
    return jnp.dot(a, b, precision=precision, preferred_element_type=F32)


def _dot_nt(a, b, precision=None):
    return lax.dot_general(a, b, (((1,), (1,)), ((), ())), precision=precision, preferred_element_type=F32)


def _iota(shape, dim):
    return lax.broadcasted_iota(jnp.int32, shape, dim)


def _add_all(xs):
    return functools.reduce(lambda a, b: a + b, xs)


def _bf16_pieces(x):
    pieces, rest = [], x
    for _ in range(SPLIT_PIECES):
        p = rest.astype(BF16).astype(F32)
        pieces.append(p)
        rest = rest - p
    return pieces


def _norm_proj_body(x_ref, g_ref, *refs, n_w):
    w_refs, o_refs = refs[:n_w], refs[n_w:]
    xb = _rmsnorm(x_ref[...], g_ref[...]).astype(BF16)
    for w_ref, o_ref in zip(w_refs, o_refs):
        o_ref[...] = _dot(xb, w_ref[...])


def norm_proj(x, g, ws, tm):
    m, d = x.shape
    assert m % tm == 0
    in_specs = [pl.BlockSpec((tm, d), lambda i: (i, 0)), pl.BlockSpec((1, d), lambda i: (0, 0))]
    in_specs += [pl.BlockSpec(w.shape, lambda i: (0, 0)) for w in ws]
    out_specs = [pl.BlockSpec((tm, w.shape[1]), lambda i: (i, 0)) for w in ws]
    out_shape = [jax.ShapeDtypeStruct((m, w.shape[1]), F32) for w in ws]
    return pl.pallas_call(
        functools.partial(_norm_proj_body, n_w=len(ws)),
        grid=(m // tm,), in_specs=in_specs, out_specs=out_specs, out_shape=out_shape,
        name="norm_proj",
    )(x, g.reshape(1, d), *ws)


def _proj_res_body(h_ref, *refs, n_a):
    a_refs, w_refs, o_ref = refs[:n_a], refs[n_a:2 * n_a], refs[2 * n_a]
    acc = h_ref[...]
    for a_ref, w_ref in zip(a_refs, w_refs):
        acc = acc + _dot(a_ref[...].astype(BF16), w_ref[...])
    o_ref[...] = acc


def proj_res(h, acts, ws, tm):
    m, d = h.shape
    assert m % tm == 0
    in_specs = [pl.BlockSpec((tm, d), lambda i: (i, 0))]
    in_specs += [pl.BlockSpec((tm, a.shape[1]), lambda i: (i, 0)) for a in acts]
    in_specs += [pl.BlockSpec(w.shape, lambda i: (0, 0)) for w in ws]
    return pl.pallas_call(
        functools.partial(_proj_res_body, n_a=len(acts)),
        grid=(m // tm,), in_specs=in_specs,
        out_specs=pl.BlockSpec((tm, d), lambda i: (i, 0)),
        out_shape=jax.ShapeDtypeStruct((m, d), F32),
        name="proj_res",
    )(h, *acts, *ws)


def _ffn_body(x_ref, g_ref, gf_ref, wg_ref, wu_ref, wd_ref, o_ref, *, final_norm):
    x = x_ref[...]
    xn = _rmsnorm(x, g_ref[...]).astype(BF16)
    gate = _dot(xn, wg_ref[...])
    up = _dot(xn, wu_ref[...])
    act = (gate * jax.nn.sigmoid(gate) * up).astype(BF16)
    y = x + _dot(act, wd_ref[...])
    o_ref[...] = _rmsnorm(y, gf_ref[...]) if final_norm else y


def ffn(h, g, g_final, wg, wu, wd, tm, final_norm):
    m, d = h.shape
    assert m % tm == 0
    resident = lambda w: pl.BlockSpec(w.shape, lambda i: (0, 0), pipeline_mode=pl.Buffered(1))
    return pl.pallas_call(
        functools.partial(_ffn_body, final_norm=final_norm),
        grid=(m // tm,),
        in_specs=[
            pl.BlockSpec((tm, d), lambda i: (i, 0)),
            pl.BlockSpec((1, d), lambda i: (0, 0)),
            pl.BlockSpec((1, d), lambda i: (0, 0)),
            resident(wg), resident(wu), resident(wd),
        ],
        out_specs=pl.BlockSpec((tm, d), lambda i: (i, 0)),
        out_shape=jax.ShapeDtypeStruct((m, d), F32),
        name="ffn",
    )(h, g.reshape(1, d), g_final.reshape(1, d), wg, wu, wd)


def _mlstm_body(q_ref, k_ref, v_ref, o_ref, g_ref, gb_ref, gh_ref, c0_ref, n0_ref, m0_ref,
                y_ref, c_ref, n_ref, m_ref, c_scr, n_scr, m_scr, *, chunk, n_batch):
    j = pl.program_id(1)
    nh, dk, dv = MLSTM_HEADS, MLSTM_DQK, MLSTM_DV

    @pl.when(j == 0)
    def _():
        c_scr[...] = c0_ref[...]
        n_scr[...] = n0_ref[...]
        m_scr[...] = m0_ref[...]

    lane = _iota((chunk, GATE_PAD), 1)
    r_i = _iota((chunk, chunk), 0)
    c_i = _iota((chunk, chunk), 1)
    causal = r_i >= c_i
    eye = r_i == c_i
    tril = causal.astype(F32)
    eye_b = (_iota((dk, dk), 0) == _iota((dk, dk), 1)).astype(BF16)

    def to_row(col):
        return jnp.sum(jnp.where(eye, col, 0.0), axis=0, keepdims=True)

    chains = [(b, h) for b in range(n_batch) for h in range(nh)]
    each = lambda fn: [fn(n, b, h) for n, (b, h) in enumerate(chains)]

    acts, cums = [], []
    for b in range(n_batch):
        gpre = g_ref[b] + gb_ref[...]
        acts.append(jnp.where(lane < nh, gpre, _log_sigmoid(gpre)))
        cums.append(_dot(tril, acts[b], precision=HIGHEST))
    q = each(lambda n, b, h: q_ref[b, :, h * dk:(h + 1) * dk] * (dk ** -0.5))
    k = each(lambda n, b, h: k_ref[b, :, h * dk:(h + 1) * dk])
    vb = each(lambda n, b, h: v_ref[b, :, h * dv:(h + 1) * dv].astype(BF16))
    qb = each(lambda n, b, h: q[n].astype(BF16))
    ig_c = each(lambda n, b, h: acts[b][:, h:h + 1])
    b_c = each(lambda n, b, h: cums[b][:, nh + h:nh + h + 1])
    m_prev = each(lambda n, b, h: m_scr[b, h])
    c_prev = each(lambda n, b, h: c_scr[b, h])
    n_prev = each(lambda n, b, h: n_scr[b, h])
    dmat = each(lambda n, b, h: jnp.where(causal, b_c[n] - to_row(b_c[n]) + to_row(ig_c[n]), NEG_INF))
    inter = each(lambda n, b, h: b_c[n] + m_prev[n])
    mt = each(lambda n, b, h: jnp.maximum(inter[n], jnp.max(dmat[n], axis=-1, keepdims=True)))
    w_inter = each(lambda n, b, h: jnp.exp(inter[n] - mt[n]))
    qk = each(lambda n, b, h: _dot_nt(qb[n], k[n].astype(BF16)) * jnp.exp(dmat[n] - mt[n]))
    num = each(lambda n, b, h: w_inter[n] * _dot(qb[n], c_prev[n].astype(BF16)) + _dot(qk[n].astype(BF16), vb[n]))
    den = each(lambda n, b, h: w_inter[n] * jnp.sum(q[n] * n_prev[n], axis=-1, keepdims=True)
               + jnp.sum(qk[n], axis=-1, keepdims=True))
    hh = each(lambda n, b, h: num[n] / jnp.maximum(jnp.abs(den[n]), jnp.exp(-mt[n])))
    m_new = each(lambda n, b, h: mt[n][chunk - 1:chunk, :])
    b_last = each(lambda n, b, h: b_c[n][chunk - 1:chunk, :])
    decay = each(lambda n, b, h: jnp.exp(b_last[n] + m_prev[n] - m_new[n]))
    kw = each(lambda n, b, h: k[n] * jnp.exp(b_last[n] - b_c[n] + ig_c[n] - m_new[n]))
    kw_t = each(lambda n, b, h: _dot_nt(eye_b, kw[n].astype(BF16)).astype(BF16))
    for n, (b, h) in enumerate(chains):
        c_scr[b, h] = decay[n] * c_prev[n] + _dot(kw_t[n], vb[n])
        n_scr[b, h] = decay[n] * n_prev[n] + jnp.sum(kw[n], axis=0, keepdims=True)
        m_scr[b, h] = m_new[n]
    for n, (b, h) in enumerate(chains):
        hn = hh[n] * lax.rsqrt(jnp.mean(hh[n] * hh[n], axis=-1, keepdims=True) + RMS_EPS)
        hn = hn * gh_ref[:, h * dv:(h + 1) * dv]
        y_ref[b, :, h * dv:(h + 1) * dv] = jax.nn.sigmoid(o_ref[b, :, h * dv:(h + 1) * dv]) * hn

    @pl.when(j == pl.num_programs(1) - 1)
    def _():
        c_ref[...] = c_scr[...]
        n_ref[...] = n_scr[...]
        m_ref[...] = m_scr[...]


def mlstm(q, k, v, o, gates, gate_bias, g_head, c0, n0, m0, batch, seq, n_batch):
    nh, dk, dv = MLSTM_HEADS, MLSTM_DQK, MLSTM_DV
    chunk = MLSTM_CHUNK if seq % MLSTM_CHUNK == 0 else seq
    assert chunk % SUBLANES == 0 and batch % n_batch == 0
    steps = seq // chunk
    row_spec = lambda w: pl.BlockSpec((n_batch, chunk, w), lambda b, j: (b, j, 0))
    st4 = lambda s: pl.BlockSpec((n_batch,) + s, lambda b, j: (b, 0, 0, 0))
    rows3 = lambda x: x.reshape(batch, seq, x.shape[-1])
    y, c, n, m = pl.pallas_call(
        functools.partial(_mlstm_body, chunk=chunk, n_batch=n_batch),
        grid=(batch // n_batch, steps),
        in_specs=[row_spec(nh * dk), row_spec(nh * dk), row_spec(nh * dv), row_spec(nh * dv), row_spec(GATE_PAD),
                  pl.BlockSpec((1, GATE_PAD), lambda b, j: (0, 0)),
                  pl.BlockSpec((1, nh * dv), lambda b, j: (0, 0)),
                  st4((nh, dk, dv)), st4((nh, 1, dk)), st4((nh, 1, 1))],
        out_specs=[row_spec(nh * dv), st4((nh, dk, dv)), st4((nh, 1, dk)), st4((nh, 1, 1))],
        out_shape=[jax.ShapeDtypeStruct((batch, seq, nh * dv), F32),
                   jax.ShapeDtypeStruct((batch, nh, dk, dv), F32),
                   jax.ShapeDtypeStruct((batch, nh, 1, dk), F32),
                   jax.ShapeDtypeStruct((batch, nh, 1, 1), F32)],
        scratch_shapes=[pltpu.VMEM((n_batch, nh, dk, dv), F32), pltpu.VMEM((n_batch, nh, 1, dk), F32),
                        pltpu.VMEM((n_batch, nh, 1, 1), F32)],
        compiler_params=pltpu.CompilerParams(dimension_semantics=("arbitrary", "arbitrary")),
        name="mlstm",
    )(rows3(q), rows3(k), rows3(v), rows3(o), rows3(gates), gate_bias, g_head.reshape(1, nh * dv),
      c0, n0.reshape(batch, nh, 1, dk), m0.reshape(batch, nh, 1, 1))
    return y.reshape(batch * seq, nh * dv), c, n.reshape(batch, nh, dk), m.reshape(batch, nh)


def _topk_blocks(gate, n_blocks, n_valid, axis):
    idx = _iota(gate.shape, axis)
    rank = jnp.zeros(gate.shape, F32)
    for c in range(n_blocks):
        other = gate[c:c + 1, :] if axis == 0 else gate[:, c:c + 1]
        beats = (other > gate) | ((other == gate) & (c < idx))
        candidate = jnp.where(c < n_valid, 1.0, 0.0)
        rank = rank + jnp.where(beats, candidate, 0.0)
    return (rank < MOBA_TOPK) & (idx < n_valid)


ATT_TILE = 512


def _data_lanes(lane, hh):
    return (lane >= hh * HEAD_DIM) & (lane < (hh + 1) * HEAD_DIM)


def _extra_lane0(hh):
    return (HEADS_PER_TILE - 1 - hh) * HEAD_DIM


def _values_with_ones(v, lane, hh):
    return jnp.where(_data_lanes(lane, hh), v, jnp.where(lane == _extra_lane0(hh), 1.0, 0.0)).astype(BF16)


def _flash_pair(qa, ka_scr, va_scr, i, tile):
    causal = _iota((tile, tile), 0) >= _iota((tile, tile), 1)

    def step(start, carry, diagonal):
        heads = range(HEADS_PER_TILE)
        s = [_dot_nt(qa[hh], ka_scr[hh, pl.ds(start, tile), :]) for hh in heads]
        if diagonal:
            s = [jnp.where(causal, s[hh], NEG_INF) for hh in heads]
        m_new = [jnp.maximum(carry[hh][0], jnp.max(s[hh], axis=-1, keepdims=True)) for hh in heads]
        p = [jnp.exp(s[hh] - m_new[hh]).astype(BF16) for hh in heads]
        acc = [jnp.exp(carry[hh][0] - m_new[hh]) * carry[hh][1] + _dot(p[hh], va_scr[hh, pl.ds(start, tile), :])
               for hh in heads]
        return tuple((m_new[hh], acc[hh]) for hh in heads)

    init = tuple((jnp.full((tile, 1), NEG_INF, F32), jnp.zeros((tile, LANES), F32)) for _ in range(HEADS_PER_TILE))
    carry = step(pl.multiple_of(i * tile, tile), init, True)
    carry = lax.fori_loop(0, i, lambda j, c: step(pl.multiple_of(j * tile, tile), c, False), carry)
    lane = _iota((tile, LANES), 1)
    out = jnp.zeros((tile, LANES), F32)
    for hh in range(HEADS_PER_TILE):
        acc = carry[hh][1]
        e0 = _extra_lane0(hh)
        out = jnp.where(_data_lanes(lane, hh), acc / acc[:, e0:e0 + 1], out)
    return out


def _floor_div_pow2(x, n):
    assert n & (n - 1) == 0
    return lax.shift_right_logical(x, n.bit_length() - 1)


def _alibi_digits(pos):
    assert HEAD_DIM & (HEAD_DIM - 1) == 0
    lo = pos & (HEAD_DIM - 1)
    return (pos - lo).astype(F32), lo.astype(F32)


def _moba_body(q_ref, k_ref, v_ref, sl_ref, o_ref, ka_scr, vb_scr, km_scr, *, n_blocks, tile):
    i = pl.program_id(2)
    seq = k_ref.shape[0]
    d = HEAD_DIM
    a_off = n_blocks

    def alibi_cols(idx, on_pos, pos_hi, pos_lo, slope):
        cols = jnp.zeros(idx.shape, F32)
        for n, p in enumerate(_bf16_pieces(slope)):
            base = a_off + 4 * n
            mine, other = (pos_hi, pos_lo), (p, p)
            first, second = (other, mine) if on_pos == "key" else (mine, other)
            for c, val in enumerate(first + second):
                cols = jnp.where(idx == base + c, val, cols)
        return cols

    @pl.when(i == 0)
    def _():
        k = k_ref[...]
        v = v_ref[...]
        lane = _iota((seq, LANES), 1)
        pos = _iota((seq, LANES), 0)
        pos_hi, pos_lo = _alibi_digits(pos)
        km = jnp.concatenate(
            [jnp.mean(k[n * MOBA_BLOCK:(n + 1) * MOBA_BLOCK], axis=0, keepdims=True) for n in range(n_blocks)], axis=0)
        for hh in range(HEADS_PER_TILE):
            e0 = _extra_lane0(hh)
            idx = lane - e0
            slope = sl_ref[:, hh * d:hh * d + 1]
            extras = jnp.where(idx == _floor_div_pow2(pos, MOBA_BLOCK), 1.0,
                               alibi_cols(idx, "key", pos_hi, pos_lo, slope))
            ka_scr[hh] = jnp.where(_data_lanes(lane, hh), k, extras).astype(BF16)
            vb_scr[hh] = _values_with_ones(v, lane, hh)
            km_scr[hh] = jnp.zeros((LANES, LANES), F32)
            km_scr[hh, e0:e0 + n_blocks, :] = km

    q = q_ref[...]
    lane = _iota((tile, LANES), 1)
    t_pos = i * tile + _iota((tile, LANES), 0)
    t_hi, t_lo = _alibi_digits(t_pos)
    blk_rows = -(-n_blocks // SUBLANES) * SUBLANES
    own_t = _floor_div_pow2(i * tile + _iota((1, tile), 1), MOBA_BLOCK)
    qa = []
    for hh in range(HEADS_PER_TILE):
        e0 = _extra_lane0(hh)
        idx = lane - e0
        data = _data_lanes(lane, hh)
        slope = sl_ref[:, hh * d:hh * d + 1]
        gate_t = _dot_nt(km_scr[hh], jnp.where(data, q, 0.0), precision=HIGHEST)[e0:e0 + blk_rows]
        keep_t = _topk_blocks(gate_t, n_blocks, own_t, axis=0) | (_iota((blk_rows, tile), 0) == own_t)
        above, below = e0, LANES - e0 - blk_rows
        bias_t = ([jnp.zeros((above, tile), F32)] if above else []) + [jnp.where(keep_t, 0.0, MASK_BIAS)]
        bias_t += [jnp.zeros((below, tile), F32)] if below else []
        block_bias = jnp.concatenate(bias_t, axis=0).T
        extras = jnp.where((idx >= 0) & (idx < n_blocks), block_bias, alibi_cols(idx, "query", -t_hi, -t_lo, slope))
        qa.append(jnp.where(data, q * (d ** -0.5), extras).astype(BF16))
    o_ref[...] = _flash_pair(qa, ka_scr, vb_scr, i, tile)


def _alibi_lanes(n_heads):
    slopes = np.array([2.0 ** (-8.0 * (i + 1) / n_heads) for i in range(n_heads)], dtype=np.float32)
    return jnp.asarray(np.repeat(slopes, HEAD_DIM).reshape(1, n_heads * HEAD_DIM))


def moba_prompt(q, k, v, batch, seq):
    width = MOBA_HEADS * HEAD_DIM
    tile = min(ATT_TILE, seq)
    assert seq % tile == 0 and tile % MOBA_BLOCK == 0
    n_blocks = seq // MOBA_BLOCK
    assert n_blocks + 4 * SPLIT_PIECES <= HEAD_DIM
    nq = seq // tile
    return pl.pallas_call(
        functools.partial(_moba_body, n_blocks=n_blocks, tile=tile),
        grid=(batch, width // LANES, nq),
        in_specs=[pl.BlockSpec((tile, LANES), lambda b, hp, i: (b * nq + i, hp)),
                  pl.BlockSpec((seq, LANES), lambda b, hp, i: (b, hp)),
                  pl.BlockSpec((seq, LANES), lambda b, hp, i: (b, hp)),
                  pl.BlockSpec((1, LANES), lambda b, hp, i: (0, hp))],
        out_specs=pl.BlockSpec((tile, LANES), lambda b, hp, i: (b * nq + i, hp)),
        out_shape=jax.ShapeDtypeStruct((batch * seq, width), F32),
        scratch_shapes=[pltpu.VMEM((HEADS_PER_TILE, seq, LANES), BF16), pltpu.VMEM((HEADS_PER_TILE, seq, LANES), BF16),
                        pltpu.VMEM((HEADS_PER_TILE, LANES, LANES), F32)],
        compiler_params=pltpu.CompilerParams(dimension_semantics=("arbitrary", "arbitrary", "arbitrary")),
        name="moba_prompt",
    )(q, k, v, _alibi_lanes(MOBA_HEADS))


def _fox_gate_body(f_ref, b_ref, lf_ref, cc_ref, carry_scr, *, tiles_per_seq):
    i = pl.program_id(0)
    tm = f_ref.shape[0]

    @pl.when(i % tiles_per_seq == 0)
    def _():
        carry_scr[...] = jnp.zeros_like(carry_scr)

    lf = _log_sigmoid(f_ref[...] + b_ref[...])
    lf_ref[...] = lf
    tril = (_iota((tm, tm), 0) >= _iota((tm, tm), 1)).astype(F32)
    c = _dot(tril, lf, precision=HIGHEST) + carry_scr[...]
    carry_scr[...] = c[tm - 1:tm, :]
    cc_ref[...] = c


def fox_gates(f, bias, seq, tm):
    m = f.shape[0]
    assert seq % tm == 0 and m % seq == 0
    return pl.pallas_call(
        functools.partial(_fox_gate_body, tiles_per_seq=seq // tm),
        grid=(m // tm,),
        in_specs=[pl.BlockSpec((tm, GATE_PAD), lambda i: (i, 0)), pl.BlockSpec((1, GATE_PAD), lambda i: (0, 0))],
        out_specs=[pl.BlockSpec((tm, GATE_PAD), lambda i: (i, 0)), pl.BlockSpec((tm, GATE_PAD), lambda i: (i, 0))],
        out_shape=[jax.ShapeDtypeStruct((m, GATE_PAD), F32), jax.ShapeDtypeStruct((m, GATE_PAD), F32)],
        scratch_shapes=[pltpu.VMEM((1, GATE_PAD), F32)],
        compiler_params=pltpu.CompilerParams(dimension_semantics=("arbitrary",)),
        name="fox_gates",
    )(f, bias)


def _fox_body(q_ref, k_ref, v_ref, cq_ref, ck_ref, o_ref, ka_scr, vb_scr, *, tile):
    hp = pl.program_id(1)
    i = pl.program_id(2)
    seq = k_ref.shape[0]
    d = HEAD_DIM

    def head_column(c_all, hh):
        lane_g = _iota(c_all.shape, 1)
        return jnp.sum(jnp.where(lane_g == hp * HEADS_PER_TILE + hh, c_all, 0.0), axis=-1, keepdims=True)

    def sum_cols(idx, c_col, on_pos):
        cols = jnp.zeros(idx.shape, F32)
        for n, p in enumerate(_bf16_pieces(c_col)):
            key_side = on_pos == "key"
            cols = jnp.where(idx == n, 1.0 if key_side else p, cols)
            cols = jnp.where(idx == SPLIT_PIECES + n, -p if key_side else 1.0, cols)
        return cols

    @pl.when(i == 0)
    def _():
        k = k_ref[...]
        v = v_ref[...]
        lane = _iota((seq, LANES), 1)
        c_all = ck_ref[...]
        for hh in range(HEADS_PER_TILE):
            extras = sum_cols(lane - _extra_lane0(hh), head_column(c_all, hh), "key")
            ka_scr[hh] = jnp.where(_data_lanes(lane, hh), k, extras).astype(BF16)
            vb_scr[hh] = _values_with_ones(v, lane, hh)

    q = q_ref[...]
    lane = _iota((tile, LANES), 1)
    c_all = cq_ref[...]
    qa = []
    for hh in range(HEADS_PER_TILE):
        extras = sum_cols(lane - _extra_lane0(hh), head_column(c_all, hh), "query")
        qa.append(jnp.where(_data_lanes(lane, hh), q * (d ** -0.5), extras).astype(BF16))
    o_ref[...] = _flash_pair(qa, ka_scr, vb_scr, i, tile)


def fox_prompt(q, k, v, c_col, batch, seq):
    width = FOX_HEADS * HEAD_DIM
    tile = min(ATT_TILE, seq)
    assert seq % tile == 0 and 2 * SPLIT_PIECES <= HEAD_DIM
    nq = seq // tile
    return pl.pallas_call(
        functools.partial(_fox_body, tile=tile),
        grid=(batch, width // LANES, nq),
        in_specs=[pl.BlockSpec((tile, LANES), lambda b, hp, i: (b * nq + i, hp)),
                  pl.BlockSpec((seq, LANES), lambda b, hp, i: (b, hp)),
                  pl.BlockSpec((seq, LANES), lambda b, hp, i: (b, hp)),
                  pl.BlockSpec((tile, GATE_PAD), lambda b, hp, i: (b * nq + i, 0)),
                  pl.BlockSpec((seq, GATE_PAD), lambda b, hp, i: (b, 0))],
        out_specs=pl.BlockSpec((tile, LANES), lambda b, hp, i: (b * nq + i, hp)),
        out_shape=jax.ShapeDtypeStruct((batch * seq, width), F32),
        scratch_shapes=[pltpu.VMEM((HEADS_PER_TILE, seq, LANES), BF16), pltpu.VMEM((HEADS_PER_TILE, seq, LANES), BF16)],
        compiler_params=pltpu.CompilerParams(dimension_semantics=("arbitrary", "arbitrary", "arbitrary")),
        name="fox_prompt",
    )(q, k, v, c_col, c_col)


def _block_diag_queries(q, n_heads, n_tok):
    rows, width = n_heads * n_tok, n_heads * HEAD_DIM
    tiled = jnp.concatenate([q] * n_heads, axis=0)
    same = (_iota((rows, width), 0) // n_tok) == (_iota((rows, width), 1) // HEAD_DIM)
    return jnp.where(same, tiled, 0.0)


def _head_diag(acc, n_heads, n_tok):
    width = n_heads * HEAD_DIM
    lane_head = _iota((n_tok, width), 1) // HEAD_DIM
    out = jnp.zeros((n_tok, width), F32)
    for h in range(n_heads):
        out = jnp.where(lane_head == h, acc[h * n_tok:(h + 1) * n_tok, :], out)
    return out


def _new_token_partial(qb, kn, vn, bias, n_tok):
    rows = qb.shape[0]
    s = _dot_nt(qb, kn.astype(BF16)) + bias
    tok = _iota((rows, n_tok), 0) % n_tok
    s = jnp.where(tok >= _iota((rows, n_tok), 1), s, NEG_INF)
    m = jnp.max(s, axis=-1, keepdims=True)
    p = jnp.exp(s - m)
    l = jnp.sum(p, axis=-1, keepdims=True)
    return m, l, _dot(p.astype(BF16), vn.astype(BF16))


def _moba_dec_body(pt_ref, q_ref, kn_ref, vn_ref, sl_ref, *refs, pages_per_step, page, n_tok, past_len):
    pp = pages_per_step
    k_refs, v_refs = refs[:pp], refs[pp:2 * pp]
    o_ref = refs[2 * pp]
    qb_scr, g_scr, m_scr, l_scr, acc_scr = refs[2 * pp + 1:]
    s_id = pl.program_id(1)
    nh, d = MOBA_HEADS, HEAD_DIM
    rows = nh * n_tok
    ppb = MOBA_BLOCK // page
    n_past = past_len // MOBA_BLOCK
    scale = d ** -0.5
    slope = sl_ref[...]
    tok = (_iota((rows, 1), 0) % n_tok).astype(F32)
    lane = _iota((rows, LANES), 1)

    @pl.when(s_id == 0)
    def _():
        qb_scr[...] = (_block_diag_queries(q_ref[...], nh, n_tok) * scale).astype(BF16)
        g_scr[...] = jnp.zeros_like(g_scr)
        m_scr[...] = jnp.zeros_like(m_scr)
        l_scr[...] = jnp.zeros_like(l_scr)

    qb = qb_scr[...]
    blocks = range(pp // ppb)
    pages = range(pp)
    blk0 = s_id * (pp // ppb)
    qk = [_dot(qb, k_refs[u][...].astype(BF16)) for u in pages]
    qk_sum = [jnp.sum(qk[u], axis=-1, keepdims=True) for u in pages]
    pos0 = (blk0 * MOBA_BLOCK).astype(F32) + _iota((rows, page), 1).astype(F32)
    s = [qk[u] - slope * ((past_len + tok) - (pos0 + float(u * page))) for u in pages]
    s_max = [jnp.max(s[u], axis=-1, keepdims=True) for u in pages]
    m = [functools.reduce(jnp.maximum, s_max[g * ppb:(g + 1) * ppb]) for g in blocks]
    p = [jnp.exp(s[u] - m[u // ppb]) for u in pages]
    p_sum = [jnp.sum(p[u], axis=-1, keepdims=True) for u in pages]
    pv = [_dot_nt(p[u].astype(BF16), v_refs[u][...].astype(BF16)) for u in pages]
    g_all, m_all, l_all = g_scr[...], m_scr[...], l_scr[...]
    for g in blocks:
        mine = slice(g * ppb, (g + 1) * ppb)
        gate = _add_all(qk_sum[mine]) * (1.0 / (scale * MOBA_BLOCK))
        g_all = jnp.where(lane == blk0 + g, gate, g_all)
        m_all = jnp.where(lane == blk0 + g, m[g], m_all)
        l_all = jnp.where(lane == blk0 + g, _add_all(p_sum[mine]), l_all)
        acc_scr[blk0 + g] = _add_all(pv[mine])
    g_scr[...], m_scr[...], l_scr[...] = g_all, m_all, l_all

    @pl.when(s_id == pl.num_programs(1) - 1)
    def _():
        sel = _topk_blocks(g_scr[...], n_past, n_past, axis=1)
        m_all = m_scr[...]
        t_k = _iota((rows, n_tok), 1).astype(F32)
        m_own, l_own, acc_own = _new_token_partial(qb, kn_ref[...], vn_ref[...], -slope * (tok - t_k), n_tok)
        m_tot = jnp.maximum(m_own, jnp.max(jnp.where(sel, m_all, NEG_INF), axis=-1, keepdims=True))
        w = jnp.where(sel, jnp.exp(m_all - m_tot), 0.0)
        w_own = jnp.exp(m_own - m_tot)
        l_tot = w_own * l_own + jnp.sum(w * l_scr[...], axis=-1, keepdims=True)
        acc = w_own * acc_own
        for n in range(n_past):
            acc = acc + w[:, n:n + 1] * acc_scr[n]
        o_ref[...] = _head_diag(acc / l_tot, nh, n_tok)


def _pages_position_minor(cache):
    n_phys, page, heads, d = cache.shape
    return jnp.transpose(cache, (0, 2, 3, 1)).reshape(n_phys, heads * d, page)


def moba_decode(q, k_new, v_new, cache_k, cache_v, page_table, n_tok, pages_per_step):
    n_seq, n_pages = page_table.shape
    page = cache_k.shape[1]
    nh, d = MOBA_HEADS, HEAD_DIM
    width = nh * d
    past_len = n_pages * page
    rows = nh * n_tok
    assert MOBA_BLOCK % page == 0 and past_len % MOBA_BLOCK == 0 and n_tok <= MOBA_BLOCK
    assert pages_per_step % (MOBA_BLOCK // page) == 0 and n_pages % pages_per_step == 0
    n_past = past_len // MOBA_BLOCK
    ck = _pages_position_minor(cache_k)
    cv = _pages_position_minor(cache_v)
    slope_rows = jnp.asarray(np.repeat(
        np.array([2.0 ** (-8.0 * (i + 1) / nh) for i in range(nh)], dtype=np.float32), n_tok).reshape(rows, 1))
    tok_spec = pl.BlockSpec((n_tok, width), lambda b, s, pt: (b, 0))

    assert n_past <= LANES

    def page_spec(u):
        return pl.BlockSpec((None, width, page), lambda b, s, pt: (pt[b, s * pages_per_step + u], 0, 0))

    grid_spec = pltpu.PrefetchScalarGridSpec(
        num_scalar_prefetch=1,
        grid=(n_seq, n_pages // pages_per_step),
        in_specs=[tok_spec, tok_spec, tok_spec, pl.BlockSpec((rows, 1), lambda b, s, pt: (0, 0))]
        + [page_spec(u) for u in range(pages_per_step)] * 2,
        out_specs=tok_spec,
        scratch_shapes=[pltpu.VMEM((rows, width), BF16), pltpu.VMEM((rows, LANES), F32),
                        pltpu.VMEM((rows, LANES), F32), pltpu.VMEM((rows, LANES), F32),
                        pltpu.VMEM((n_past, rows, width), F32)],
    )
    return pl.pallas_call(
        functools.partial(_moba_dec_body, pages_per_step=pages_per_step, page=page, n_tok=n_tok, past_len=past_len),
        grid_spec=grid_spec,
        out_shape=jax.ShapeDtypeStruct((n_seq * n_tok, width), F32),
        compiler_params=pltpu.CompilerParams(dimension_semantics=("arbitrary", "arbitrary")),
        name="moba_decode",
    )(page_table, q, k_new, v_new, slope_rows, *([ck] * pages_per_step), *([cv] * pages_per_step))


def _fox_dec_body(pt_ref, q_ref, kn_ref, vn_ref, f_ref, fb_ref, *refs, pages_per_step, page, n_tok):
    pp = pages_per_step
    k_refs, v_refs, lf_refs = refs[:pp], refs[pp:2 * pp], refs[2 * pp:3 * pp]
    o_ref, lfn_ref = refs[3 * pp], refs[3 * pp + 1]
    qb_scr, carry_scr, ctot_scr, m_scr, l_scr, acc_scr = refs[3 * pp + 2:]
    s_id = pl.program_id(1)
    nh, d = FOX_HEADS, HEAD_DIM
    rows = nh * n_tok
    expand = ((_iota((rows, nh), 0) // n_tok) == _iota((rows, nh), 1)).astype(F32)
    triu = (_iota((page, page), 0) <= _iota((page, page), 1)).astype(BF16)

    @pl.when(s_id == 0)
    def _():
        qb_scr[...] = (_block_diag_queries(q_ref[...], nh, n_tok) * (d ** -0.5)).astype(BF16)
        carry_scr[...] = jnp.zeros_like(carry_scr)
        m_scr[...] = jnp.full_like(m_scr, NEG_INF)
        l_scr[...] = jnp.zeros_like(l_scr)
        acc_scr[...] = jnp.zeros_like(acc_scr)

    qb = qb_scr[...]
    pages = range(pp)
    qk = [_dot(qb, k_refs[u][...].astype(BF16)) for u in pages]
    pieces = [jnp.concatenate([p.astype(BF16) for p in _bf16_pieces(lf_refs[u][...])], axis=0) for u in pages]
    sums = [_dot(pieces[u], triu) for u in pages]
    local = [_add_all([sums[u][n * nh:(n + 1) * nh] for n in range(SPLIT_PIECES)]) for u in pages]
    c_heads, offset = [], carry_scr[...]
    for u in pages:
        c_heads.append(local[u] + offset)
        offset = c_heads[u][:, page - 1:page]
    c_rows = [jnp.broadcast_to(c_heads[u][:, None, :], (nh, n_tok, page)).reshape(rows, page) for u in pages]
    ss = [qk[u] - c_rows[u] for u in pages]
    carry_scr[...] = offset
    ctot_scr[...] = c_rows[-1][:, page - 1:page]
    m_prev = m_scr[...]
    m_new = m_prev
    for s in ss:
        m_new = jnp.maximum(m_new, jnp.max(s, axis=-1, keepdims=True))
    alpha = jnp.exp(m_prev - m_new)
    p = [jnp.exp(ss[u] - m_new) for u in pages]
    p_sum = [jnp.sum(p[u], axis=-1, keepdims=True) for u in pages]
    pv = [_dot_nt(p[u].astype(BF16), v_refs[u][...].astype(BF16)) for u in pages]
    acc_scr[...] = alpha * acc_scr[...] + _add_all(pv)
    l_scr[...] = alpha * l_scr[...] + _add_all(p_sum)
    m_scr[...] = m_new

    @pl.when(s_id == pl.num_programs(1) - 1)
    def _():
        lf_new = _log_sigmoid(f_ref[...] + fb_ref[...])
        lfn_ref[...] = lf_new
        tril_t = (_iota((n_tok, n_tok), 0) >= _iota((n_tok, n_tok), 1)).astype(F32)
        c_new = _dot(tril_t, lf_new, precision=HIGHEST)[:, :nh]
        c_new_rows = jnp.concatenate([c_new] * nh, axis=0)
        c_q = jnp.sum(expand * c_new_rows, axis=-1, keepdims=True)
        m_past = m_scr[...] + (ctot_scr[...] + c_q)
        c_k = _dot_nt(expand, c_new, precision=HIGHEST)
        m_own, l_own, acc_own = _new_token_partial(qb, kn_ref[...], vn_ref[...], c_q - c_k, n_tok)
        m_tot = jnp.maximum(m_past, m_own)
        w_past = jnp.exp(m_past - m_tot)
        w_own = jnp.exp(m_own - m_tot)
        l_tot = w_past * l_scr[...] + w_own * l_own
        acc = w_past * acc_scr[...] + w_own * acc_own
        o_ref[...] = _head_diag(acc / l_tot, nh, n_tok)


def fox_decode(q, k_new, v_new, f_new, f_bias, cache_k, cache_v, cache_lf, page_table, n_tok, pages_per_step):
    n_seq, n_pages = page_table.shape
    page = cache_k.shape[1]
    nh, d = FOX_HEADS, HEAD_DIM
    width = nh * d
    rows = nh * n_tok
    assert n_pages % pages_per_step == 0
    assert n_tok % SUBLANES == 0
    ck = _pages_position_minor(cache_k)
    cv = _pages_position_minor(cache_v)
    clf =jnp.transpose(cache_lf, (0, 2, 1))
    tok_spec = pl.BlockSpec((n_tok, width), lambda b, s, pt: (b, 0))
    gate_spec = pl.BlockSpec((n_tok, GATE_PAD), lambda b, s, pt: (b, 0))

    def page_spec(u, dims):
        return pl.BlockSpec((None,) + dims, lambda b, s, pt: (pt[b, s * pages_per_step + u],) + (0,) * len(dims))

    grid_spec = pltpu.PrefetchScalarGridSpec(
        num_scalar_prefetch=1,
        grid=(n_seq, n_pages // pages_per_step),
        in_specs=[tok_spec, tok_spec, tok_spec, gate_spec, pl.BlockSpec((1, GATE_PAD), lambda b, s, pt: (0, 0))]
        + [page_spec(u, (width, page)) for u in range(pages_per_step)] * 2
        + [page_spec(u, (nh, page)) for u in range(pages_per_step)],
        out_specs=[tok_spec, gate_spec],
        scratch_shapes=[pltpu.VMEM((rows, width), BF16), pltpu.VMEM((nh, 1), F32), pltpu.VMEM((rows, 1), F32),
                        pltpu.VMEM((rows, 1), F32), pltpu.VMEM((rows, 1), F32), pltpu.VMEM((rows, width), F32)],
    )
    return pl.pallas_call(
        functools.partial(_fox_dec_body, pages_per_step=pages_per_step, page=page, n_tok=n_tok),
        grid_spec=grid_spec,
        out_shape=[jax.ShapeDtypeStruct((n_seq * n_tok, width), F32),
                   jax.ShapeDtypeStruct((n_seq * n_tok, GATE_PAD), F32)],
        compiler_params=pltpu.CompilerParams(dimension_semantics=("arbitrary", "arbitrary")),
        name="fox_decode",
    )(page_table, q, k_new, v_new, f_new, f_bias,
      *([ck] * pages_per_step), *([cv] * pages_per_step), *([clf] * pages_per_step))


def _pad_lanes(w, width):
    return jnp.pad(w, ((0, 0), (0, width - w.shape[1])))


def _even_pieces(w_in):
    nq = MLSTM_HEADS * MLSTM_DQK
    nv = MLSTM_HEADS * MLSTM_DV
    nm = MOBA_HEADS * HEAD_DIM
    sizes = (nq, nq, nv, nv, 2 * MLSTM_HEADS, nm, nm, nm)
    pieces, start = [], 0
    for s in sizes:
        pieces.append(w_in[:, start:start + s])
        start += s
    assert start == w_in.shape[1]
    pieces[4] = _pad_lanes(pieces[4], GATE_PAD)
    return [p.astype(BF16) for p in pieces]


def _odd_pieces(w_in):
    nw = FOX_HEADS * HEAD_DIM
    pieces = [w_in[:, 0:nw], w_in[:, nw:2 * nw], w_in[:, 2 * nw:3 * nw], _pad_lanes(w_in[:, 3 * nw:], GATE_PAD)]
    assert w_in.shape[1] == 3 * nw + FOX_HEADS
    return [p.astype(BF16) for p in pieces]


PROMPT_TM = 512
FFN_TM = 512
MLSTM_PROMPT_BATCH = 4
MLSTM_SAMPLE_BATCH = 8
FOX_GATE_TM = 512
MOBA_PAGES_PER_STEP = 16
FOX_PAGES_PER_STEP = 8


def kernel(x_prompt, x_sample, cache_moba_k, cache_moba_v, state_mlstm_C, state_mlstm_n, state_mlstm_m, cache_fox_k, cache_fox_v, cache_fox_logf, page_table, norm_mix_g, norm_ffn_g, norm_final_g, even_w_in, even_b_ig, even_b_fg, even_head_norm_g, even_w_out, odd_w_in, odd_b_f, odd_w_out, ffn_w_gate, ffn_w_up, ffn_w_down):
    bp, tp, d = x_prompt.shape
    bs, ts, _ = x_sample.shape
    depth = norm_mix_g.shape[0]
    mp, ms = bp * tp, bs * ts
    hp = x_prompt.reshape(mp, d)
    hs = x_sample.reshape(ms, d)
    tm_p = min(PROMPT_TM, mp)
    tm_s = ms
    ffn_tm_p = min(FFN_TM, mp)
    nv = MLSTM_HEADS * MLSTM_DV
    n_pages = page_table.shape[1]
    pe, se, po, so = [], [], [], []
    for layer in range(depth):
        li = layer // 2
        final = layer == depth - 1
        if layer % 2 == 0:
            ws = _even_pieces(even_w_in[li])
            gate_bias = _pad_lanes(jnp.concatenate([even_b_ig[li], even_b_fg[li]]).reshape(1, -1), GATE_PAD)
            w_out = even_w_out[li].astype(BF16)
            w_out_a, w_out_b = w_out[:nv], w_out[nv:]
            zeros = lambda *s: jnp.zeros(s, F32)
            q1, k1, v1, o1, gt, q2, k2, v2 = norm_proj(hp, norm_mix_g[layer], ws, tm_p)
            y1, c_p, n_p, m_p = mlstm(q1, k1, v1, o1, gt, gate_bias, even_head_norm_g[li],
                                      zeros(bp, MLSTM_HEADS, MLSTM_DQK, MLSTM_DV), zeros(bp, MLSTM_HEADS, MLSTM_DQK),
                                      zeros(bp, MLSTM_HEADS), bp, tp, math.gcd(bp, MLSTM_PROMPT_BATCH))
            y2 = moba_prompt(q2, k2, v2, bp, tp)
            hp = proj_res(hp, [y1, y2], [w_out_a, w_out_b], tm_p)
            pe.append((k2.reshape(bp, tp, MOBA_HEADS, HEAD_DIM), v2.reshape(bp, tp, MOBA_HEADS, HEAD_DIM), c_p, n_p, m_p))
            q1, k1, v1, o1, gt, q2, k2, v2 = norm_proj(hs, norm_mix_g[layer], ws, tm_s)
            y1, c_s, n_s, m_s = mlstm(q1, k1, v1, o1, gt, gate_bias, even_head_norm_g[li],
                                      state_mlstm_C[li], state_mlstm_n[li], state_mlstm_m[li], bs, ts,
                                      math.gcd(bs, MLSTM_SAMPLE_BATCH))
            y2 = moba_decode(q2, k2, v2, cache_moba_k[li], cache_moba_v[li], page_table, ts,
                             min(MOBA_PAGES_PER_STEP, n_pages))
            hs = proj_res(hs, [y1, y2], [w_out_a, w_out_b], tm_s)
            se.append((k2.reshape(bs, ts, MOBA_HEADS, HEAD_DIM), v2.reshape(bs, ts, MOBA_HEADS, HEAD_DIM), c_s, n_s, m_s))
        else:
            ws = _odd_pieces(odd_w_in[li])
            f_bias = _pad_lanes(odd_b_f[li].reshape(1, -1), GATE_PAD)
            w_out = odd_w_out[li].astype(BF16)
            q, k, v, f = norm_proj(hp, norm_mix_g[layer], ws, tm_p)
            lf, c_col = fox_gates(f, f_bias, tp, min(FOX_GATE_TM, tp))
            y = fox_prompt(q, k, v, c_col, bp, tp)
            hp = proj_res(hp, [y], [w_out], tm_p)
            po.append((k.reshape(bp, tp, FOX_HEADS, HEAD_DIM), v.reshape(bp, tp, FOX_HEADS, HEAD_DIM),
                       lf[:, :FOX_HEADS].reshape(bp, tp, FOX_HEADS)))
            q, k, v, f = norm_proj(hs, norm_mix_g[layer], ws, tm_s)
            y, lf = fox_decode(q, k, v, f, f_bias, cache_fox_k[li], cache_fox_v[li], cache_fox_logf[li],
                               page_table, ts, min(FOX_PAGES_PER_STEP, n_pages))
            hs = proj_res(hs, [y], [w_out], tm_s)
            so.append((k.reshape(bs, ts, FOX_HEADS, HEAD_DIM), v.reshape(bs, ts, FOX_HEADS, HEAD_DIM),
                       lf[:, :FOX_HEADS].reshape(bs, ts, FOX_HEADS)))
        wg, wu, wd = ffn_w_gate[layer].astype(BF16), ffn_w_up[layer].astype(BF16), ffn_w_down[layer].astype(BF16)
        hp = ffn(hp, norm_ffn_g[layer], norm_final_g, wg, wu, wd, ffn_tm_p, final)
        hs = ffn(hs, norm_ffn_g[layer], norm_final_g, wg, wu, wd, tm_s, final)
    stack = lambda group, idx: jnp.stack([s[idx] for s in group])
    return (hp.reshape(bp, tp, d), hs.reshape(bs, ts, d),
            stack(pe, 0), stack(pe, 1), stack(pe, 2), stack(pe, 3), stack(pe, 4),
            stack(po, 0), stack(po, 1), stack(po, 2),
            stack(se, 0), stack(se, 1), stack(se, 2), stack(se, 3), stack(se, 4),
            stack(so, 0), stack(so, 1), stack(so, 2))
```

```python
import functools
import math

import jax
import jax.numpy as jnp
import ml_dtypes
import numpy as np
from jax import lax
from jax.experimental import pallas as pl
from jax.experimental.pallas import tpu as pltpu

F32 = jnp.float32
BF16 = jnp.bfloat16
HIGHEST = lax.Precision.HIGHEST

LANES = 128
SUBLANES = 8

HEAD_DIM = 64
MLSTM_HEADS = 4
MLSTM_DQK = 64
MLSTM_DV = 128
MLSTM_CHUNK = 64
MOBA_HEADS = 8
MOBA_BLOCK = 256
MOBA_TOPK = 3
FOX_HEADS = 16
RMS_EPS = 1e-6
GATE_PAD = LANES
HEADS_PER_TILE = LANES // HEAD_DIM

NEG_INF = float("-inf")
MASK_BIAS = -1e30
SPLIT_PIECES = 3


def _log_sigmoid(x):
    return -(jnp.maximum(-x, 0.0) + jnp.log1p(jnp.exp(-jnp.abs(x))))


def _rmsnorm(x, g):
    return x * lax.rsqrt(jnp.mean(x * x, axis=-1, keepdims=True) + RMS_EPS) * g


def _dot(a, b, precision=None):
    return jnp.dot(a, b, precision=precision, preferred_element_type=F32)


def _dot_nt(a, b, precision=None):
    return lax.dot_general(a, b, (((1,), (1,)), ((), ())), precision=precision, preferred_element_type=F32)


def _iota(shape, dim):
    return lax.broadcasted_iota(jnp.int32, shape, dim)


def _add_all(xs):
    return functools.reduce(lambda a, b: a + b, xs)


def _bf16_pieces(x):
    pieces, rest = [], x
    for _ in range(SPLIT_PIECES):
        p = rest.astype(BF16).astype(F32)
        pieces.append(p)
        rest = rest - p
    return pieces


def _norm_proj_body(x_ref, g_ref, *refs, n_w, n_t):
    w_refs, wt_refs = refs[:n_w], refs[n_w:n_w + n_t]
    o_refs, ot_refs = refs[n_w + n_t:2 * n_w + n_t], refs[2 * n_w + n_t:]
    xb = _rmsnorm(x_ref[...], g_ref[...]).astype(BF16)
    for w_ref, o_ref in zip(w_refs, o_refs):
        o_ref[...] = _dot(xb, w_ref[...])
    for wt_ref, ot_ref in zip(wt_refs, ot_refs):
        ot_ref[...] = _dot_nt(wt_ref[...], xb)


def norm_proj(x, g, ws, tm, ws_t=(), seq=None):
    m, d = x.shape
    assert m % tm == 0
    in_specs = [pl.BlockSpec((tm, d), lambda i: (i, 0)), pl.BlockSpec((1, d), lambda i: (0, 0))]
    in_specs += [pl.BlockSpec(w.shape, lambda i: (0, 0)) for w in tuple(ws) + tuple(ws_t)]
    out_specs = [pl.BlockSpec((tm, w.shape[1]), lambda i: (i, 0)) for w in ws]
    out_shape = [jax.ShapeDtypeStruct((m, w.shape[1]), F32) for w in ws]
    if ws_t:
        assert seq % tm == 0 and m % seq == 0
        per_seq = seq // tm
        out_specs += [pl.BlockSpec((None, w.shape[0], tm), lambda i: (i // per_seq, 0, i % per_seq)) for w in ws_t]
        out_shape += [jax.ShapeDtypeStruct((m // seq, w.shape[0], seq), F32) for w in ws_t]
    return pl.pallas_call(
        functools.partial(_norm_proj_body, n_w=len(ws), n_t=len(ws_t)),
        grid=(m // tm,), in_specs=in_specs, out_specs=out_specs, out_shape=out_shape,
        name="norm_proj",
    )(x, g.reshape(1, d), *ws, *ws_t)


def _proj_res_body(h_ref, *refs, n_a):
    a_refs, w_refs, o_ref = refs[:n_a], refs[n_a:2 * n_a], refs[2 * n_a]
    acc = h_ref[...]
    for a_ref, w_ref in zip(a_refs, w_refs):
        acc = acc + _dot(a_ref[...].astype(BF16), w_ref[...])
    o_ref[...] = acc


def proj_res(h, acts, ws, tm):
    m, d = h.shape
    assert m % tm == 0
    in_specs = [pl.BlockSpec((tm, d), lambda i: (i, 0))]
    in_specs += [pl.BlockSpec((tm, a.shape[1]), lambda i: (i, 0)) for a in acts]
    in_specs += [pl.BlockSpec(w.shape, lambda i: (0, 0)) for w in ws]
    return pl.pallas_call(
        functools.partial(_proj_res_body, n_a=len(acts)),
        grid=(m // tm,), in_specs=in_specs,
        out_specs=pl.BlockSpec((tm, d), lambda i: (i, 0)),
        out_shape=jax.ShapeDtypeStruct((m, d), F32),
        name="proj_res",
    )(h, *acts, *ws)


def _ffn_body(x_ref, g_ref, gf_ref, wg_ref, wu_ref, wd_ref, o_ref, *, final_norm):
    x = x_ref[...]
    xn = _rmsnorm(x, g_ref[...]).astype(BF16)
    gate = _dot(xn, wg_ref[...])
    up = _dot(xn, wu_ref[...])
    act = (gate * jax.nn.sigmoid(gate) * up).astype(BF16)
    y = x + _dot(act, wd_ref[...])
    o_ref[...] = _rmsnorm(y, gf_ref[...]) if final_norm else y


def ffn(h, g, g_final, wg, wu, wd, tm, final_norm):
    m, d = h.shape
    assert m % tm == 0
    resident = lambda w: pl.BlockSpec(w.shape, lambda i: (0, 0), pipeline_mode=pl.Buffered(1))
    return pl.pallas_call(
        functools.partial(_ffn_body, final_norm=final_norm),
        grid=(m // tm,),
        in_specs=[
            pl.BlockSpec((tm, d), lambda i: (i, 0)),
            pl.BlockSpec((1, d), lambda i: (0, 0)),
            pl.BlockSpec((1, d), lambda i: (0, 0)),
            resident(wg), resident(wu), resident(wd),
        ],
        out_specs=pl.BlockSpec((tm, d), lambda i: (i, 0)),
        out_shape=jax.ShapeDtypeStruct((m, d), F32),
        name="ffn",
    )(h, g.reshape(1, d), g_final.reshape(1, d), wg, wu, wd)


def _mlstm_body(q_ref, k_ref, v_ref, o_ref, g_ref, gb_ref, gh_ref, c0_ref, n0_ref, m0_ref,
                y_ref, c_ref, n_ref, m_ref, c_scr, n_scr, m_scr, *, chunk, n_batch):
    j = pl.program_id(1)
    nh, dk, dv = MLSTM_HEADS, MLSTM_DQK, MLSTM_DV

    @pl.when(j == 0)
    def _():
        c_scr[...] = c0_ref[...]
        n_scr[...] = n0_ref[...]
        m_scr[...] = m0_ref[...]

    lane = _iota((chunk, GATE_PAD), 1)
    r_i = _iota((chunk, chunk), 0)
    c_i = _iota((chunk, chunk), 1)
    causal = r_i >= c_i
    eye = r_i == c_i
    tril = causal.astype(F32)
    eye_b = (_iota((dk, dk), 0) == _iota((dk, dk), 1)).astype(BF16)

    def to_row(col):
        return jnp.sum(jnp.where(eye, col, 0.0), axis=0, keepdims=True)

    chains = [(b, h) for b in range(n_batch) for h in range(nh)]
    each = lambda fn: [fn(n, b, h) for n, (b, h) in enumerate(chains)]

    acts, cums = [], []
    for b in range(n_batch):
        gpre = g_ref[b] + gb_ref[...]
        acts.append(jnp.where(lane < nh, gpre, _log_sigmoid(gpre)))
        cums.append(_dot(tril, acts[b], precision=HIGHEST))
    q = each(lambda n, b, h: q_ref[b, :, h * dk:(h + 1) * dk] * (dk ** -0.5))
    k = each(lambda n, b, h: k_ref[b, :, h * dk:(h + 1) * dk])
    vb = each(lambda n, b, h: v_ref[b, :, h * dv:(h + 1) * dv].astype(BF16))
    qb = each(lambda n, b, h: q[n].astype(BF16))
    ig_c = each(lambda n, b, h: acts[b][:, h:h + 1])
    b_c = each(lambda n, b, h: cums[b][:, nh + h:nh + h + 1])
    m_prev = each(lambda n, b, h: m_scr[b, h])
    c_prev = each(lambda n, b, h: c_scr[b, h])
    n_prev = each(lambda n, b, h: n_scr[b, h])
    dmat = each(lambda n, b, h: jnp.where(causal, b_c[n] - to_row(b_c[n]) + to_row(ig_c[n]), NEG_INF))
    inter = each(lambda n, b, h: b_c[n] + m_prev[n])
    mt = each(lambda n, b, h: jnp.maximum(inter[n], jnp.max(dmat[n], axis=-1, keepdims=True)))
    w_inter = each(lambda n, b, h: jnp.exp(inter[n] - mt[n]))
    qk = each(lambda n, b, h: _dot_nt(qb[n], k[n].astype(BF16)) * jnp.exp(dmat[n] - mt[n]))
    num = each(lambda n, b, h: w_inter[n] * _dot(qb[n], c_prev[n].astype(BF16)) + _dot(qk[n].astype(BF16), vb[n]))
    den = each(lambda n, b, h: w_inter[n] * jnp.sum(q[n] * n_prev[n], axis=-1, keepdims=True)
               + jnp.sum(qk[n], axis=-1, keepdims=True))
    hh = each(lambda n, b, h: num[n] / jnp.maximum(jnp.abs(den[n]), jnp.exp(-mt[n])))
    m_new = each(lambda n, b, h: mt[n][chunk - 1:chunk, :])
    b_last = each(lambda n, b, h: b_c[n][chunk - 1:chunk, :])
    decay = each(lambda n, b, h: jnp.exp(b_last[n] + m_prev[n] - m_new[n]))
    kw = each(lambda n, b, h: k[n] * jnp.exp(b_last[n] - b_c[n] + ig_c[n] - m_new[n]))
    kw_t = each(lambda n, b, h: _dot_nt(eye_b, kw[n].astype(BF16)).astype(BF16))
    for n, (b, h) in enumerate(chains):
        c_scr[b, h] = decay[n] * c_prev[n] + _dot(kw_t[n], vb[n])
        n_scr[b, h] = decay[n] * n_prev[n] + jnp.sum(kw[n], axis=0, keepdims=True)
        m_scr[b, h] = m_new[n]
    for n, (b, h) in enumerate(chains):
        hn = hh[n] * lax.rsqrt(jnp.mean(hh[n] * hh[n], axis=-1, keepdims=True) + RMS_EPS)
        hn = hn * gh_ref[:, h * dv:(h + 1) * dv]
        y_ref[b, :, h * dv:(h + 1) * dv] = jax.nn.sigmoid(o_ref[b, :, h * dv:(h + 1) * dv]) * hn

    @pl.when(j == pl.num_programs(1) - 1)
    def _():
        c_ref[...] = c_scr[...]
        n_ref[...] = n_scr[...]
        m_ref[...] = m_scr[...]


def mlstm(q, k, v, o, gates, gate_bias, g_head, c0, n0, m0, batch, seq, n_batch):
    nh, dk, dv = MLSTM_HEADS, MLSTM_DQK, MLSTM_DV
    chunk = MLSTM_CHUNK if seq % MLSTM_CHUNK == 0 else seq
    assert chunk % SUBLANES == 0 and batch % n_batch == 0
    steps = seq // chunk
    row_spec = lambda w: pl.BlockSpec((n_batch, chunk, w), lambda b, j: (b, j, 0))
    st4 = lambda s: pl.BlockSpec((n_batch,) + s, lambda b, j: (b, 0, 0, 0))
    rows3 = lambda x: x.reshape(batch, seq, x.shape[-1])
    y, c, n, m = pl.pallas_call(
        functools.partial(_mlstm_body, chunk=chunk, n_batch=n_batch),
        grid=(batch // n_batch, steps),
        in_specs=[row_spec(nh * dk), row_spec(nh * dk), row_spec(nh * dv), row_spec(nh * dv), row_spec(GATE_PAD),
                  pl.BlockSpec((1, GATE_PAD), lambda b, j: (0, 0)),
                  pl.BlockSpec((1, nh * dv), lambda b, j: (0, 0)),
                  st4((nh, dk, dv)), st4((nh, 1, dk)), st4((nh, 1, 1))],
        out_specs=[row_spec(nh * dv), st4((nh, dk, dv)), st4((nh, 1, dk)), st4((nh, 1, 1))],
        out_shape=[jax.ShapeDtypeStruct((batch, seq, nh * dv), F32),
                   jax.ShapeDtypeStruct((batch, nh, dk, dv), F32),
                   jax.ShapeDtypeStruct((batch, nh, 1, dk), F32),
                   jax.ShapeDtypeStruct((batch, nh, 1, 1), F32)],
        scratch_shapes=[pltpu.VMEM((n_batch, nh, dk, dv), F32), pltpu.VMEM((n_batch, nh, 1, dk), F32),
                        pltpu.VMEM((n_batch, nh, 1, 1), F32)],
        compiler_params=pltpu.CompilerParams(dimension_semantics=("arbitrary", "arbitrary")),
        name="mlstm",
    )(rows3(q), rows3(k), rows3(v), rows3(o), rows3(gates), gate_bias, g_head.reshape(1, nh * dv),
      c0, n0.reshape(batch, nh, 1, dk), m0.reshape(batch, nh, 1, 1))
    return y.reshape(batch * seq, nh * dv), c, n.reshape(batch, nh, dk), m.reshape(batch, nh)


def _topk_blocks(gate, n_blocks, n_valid, axis):
    idx = _iota(gate.shape, axis)
    rank = jnp.zeros(gate.shape, F32)
    for c in range(n_blocks):
        other = gate[c:c + 1, :] if axis == 0 else gate[:, c:c + 1]
        beats = (other > gate) | ((other == gate) & (c < idx))
        candidate = jnp.where(c < n_valid, 1.0, 0.0)
        rank = rank + jnp.where(beats, candidate, 0.0)
    return (rank < MOBA_TOPK) & (idx < n_valid)


ATT_TILE = 512
ATT_KEY_TILE = 512


def _data_lanes(lane, hh):
    return (lane >= hh * HEAD_DIM) & (lane < (hh + 1) * HEAD_DIM)


def _extra_lane0(hh):
    return (HEADS_PER_TILE - 1 - hh) * HEAD_DIM


def _values_with_ones(v, lane, hh):
    return jnp.where(_data_lanes(lane, hh), v, jnp.where(lane == _extra_lane0(hh), 1.0, 0.0)).astype(BF16)


def _flash_pair(qa, ka_scr, va_scr, i, tile):
    heads = range(HEADS_PER_TILE)
    ktile = min(ATT_KEY_TILE, tile)
    sub = tile // ktile
    rel = _iota((tile, ktile), 0) - _iota((tile, ktile), 1)

    def step(start, carry, diag_part):
        s = [_dot_nt(qa[hh], ka_scr[hh, pl.ds(start, ktile), :]) for hh in heads]
        if diag_part is not None:
            s = [jnp.where(rel >= diag_part * ktile, s[hh], NEG_INF) for hh in heads]
        m_new = [jnp.maximum(carry[hh][0], jnp.max(s[hh], axis=-1, keepdims=True)) for hh in heads]
        p = [jnp.exp(s[hh] - m_new[hh]).astype(BF16) for hh in heads]
        acc = [jnp.exp(carry[hh][0] - m_new[hh]) * carry[hh][1] + _dot(p[hh], va_scr[hh, pl.ds(start, ktile), :])
               for hh in heads]
        return tuple((m_new[hh], acc[hh]) for hh in heads)

    carry = tuple((jnp.full((tile, 1), NEG_INF, F32), jnp.zeros((tile, LANES), F32)) for _ in heads)
    for part in range(sub):
        carry = step(pl.multiple_of(i * tile + part * ktile, ktile), carry, part)
    carry = lax.fori_loop(0, i * sub, lambda j, c: step(pl.multiple_of(j * ktile, ktile), c, None), carry)
    lane = _iota((tile, LANES), 1)
    out = jnp.zeros((tile, LANES), F32)
    for hh in heads:
        acc = carry[hh][1]
        e0 = _extra_lane0(hh)
        out = jnp.where(_data_lanes(lane, hh), acc / acc[:, e0:e0 + 1], out)
    return out


def _floor_div_pow2(x, n):
    assert n & (n - 1) == 0
    return lax.shift_right_logical(x, n.bit_length() - 1)


def _alibi_digits(pos):
    assert HEAD_DIM & (HEAD_DIM - 1) == 0
    lo = pos & (HEAD_DIM - 1)
    return (pos - lo).astype(F32), lo.astype(F32)


ALIBI_ROWS = 8


def _alibi_table(n_heads, n_blocks):
    rest = np.array([2.0 ** (-8.0 * (i + 1) / n_heads) for i in range(n_heads)], dtype=np.float32)
    table = np.zeros((n_heads, ALIBI_ROWS, LANES), np.float32)
    for n in range(SPLIT_PIECES):
        piece = rest.astype(ml_dtypes.bfloat16).astype(np.float32)
        rest = rest - piece
        for h in range(n_heads):
            base = _extra_lane0(h % HEADS_PER_TILE) + n_blocks + 4 * n
            table[h, 0, base + 2:base + 4] = piece[h]
            table[h, 1, base] = 1.0
            table[h, 2, base + 1] = 1.0
            table[h, 3, base:base + 2] = piece[h]
            table[h, 4, base + 2] = 1.0
            table[h, 5, base + 3] = 1.0
    return jnp.asarray(table)


def _moba_body(q_ref, k_ref, v_ref, al_ref, o_ref, ka_scr, vb_scr, km_scr, *, n_blocks, tile):
    i = pl.program_id(2)
    seq = k_ref.shape[0]
    d = HEAD_DIM

    def alibi_cols(hh, side, pos_hi, pos_lo):
        rows = al_ref[hh]
        if side == "query":
            return rows[0:1] - pos_hi * rows[1:2] - pos_lo * rows[2:3]
        return rows[3:4] + pos_hi * rows[4:5] + pos_lo * rows[5:6]

    @pl.when(i == 0)
    def _():
        k = k_ref[...]
        v = v_ref[...]
        lane = _iota((seq, LANES), 1)
        pos = _iota((seq, LANES), 0)
        pos_hi, pos_lo = _alibi_digits(pos)
        km = jnp.concatenate(
            [jnp.mean(k[n * MOBA_BLOCK:(n + 1) * MOBA_BLOCK], axis=0, keepdims=True) for n in range(n_blocks)], axis=0)
        for hh in range(HEADS_PER_TILE):
            e0 = _extra_lane0(hh)
            idx = lane - e0
            extras = jnp.where(idx == _floor_div_pow2(pos, MOBA_BLOCK), 1.0, alibi_cols(hh, "key", pos_hi, pos_lo))
            ka_scr[hh] = jnp.where(_data_lanes(lane, hh), k, extras).astype(BF16)
            vb_scr[hh] = _values_with_ones(v, lane, hh)
            km_scr[hh] = jnp.zeros((LANES, LANES), F32)
            km_scr[hh, e0:e0 + n_blocks, :] = km

    q = q_ref[...]
    lane = _iota((tile, LANES), 1)
    t_pos = i * tile + _iota((tile, LANES), 0)
    t_hi, t_lo = _alibi_digits(t_pos)
    blk_rows = -(-n_blocks // SUBLANES) * SUBLANES
    own_t = _floor_div_pow2(i * tile + _iota((1, tile), 1), MOBA_BLOCK)
    qa = []
    for hh in range(HEADS_PER_TILE):
        e0 = _extra_lane0(hh)
        idx = lane - e0
        data = _data_lanes(lane, hh)
        gate_t = _dot_nt(km_scr[hh], jnp.where(data, q, 0.0), precision=HIGHEST)[e0:e0 + blk_rows]
        keep_t = _topk_blocks(gate_t, n_blocks, own_t, axis=0) | (_iota((blk_rows, tile), 0) == own_t)
        above, below = e0, LANES - e0 - blk_rows
        bias_t = ([jnp.zeros((above, tile), F32)] if above else []) + [jnp.where(keep_t, 0.0, MASK_BIAS)]
        bias_t += [jnp.zeros((below, tile), F32)] if below else []
        block_bias = jnp.concatenate(bias_t, axis=0).T
        extras = jnp.where((idx >= 0) & (idx < n_blocks), block_bias, alibi_cols(hh, "query", t_hi, t_lo))
        qa.append(jnp.where(data, q * (d ** -0.5), extras).astype(BF16))
    o_ref[...] = _flash_pair(qa, ka_scr, vb_scr, i, tile)


def moba_prompt(q, k, v, batch, seq):
    width = MOBA_HEADS * HEAD_DIM
    tile = min(ATT_TILE, seq)
    assert seq % tile == 0 and tile % MOBA_BLOCK == 0
    n_blocks = seq // MOBA_BLOCK
    assert n_blocks + 4 * SPLIT_PIECES <= HEAD_DIM
    nq = seq // tile
    return pl.pallas_call(
        functools.partial(_moba_body, n_blocks=n_blocks, tile=tile),
        grid=(batch, width // LANES, nq),
        in_specs=[pl.BlockSpec((tile, LANES), lambda b, hp, i: (b * nq + i, hp)),
                  pl.BlockSpec((seq, LANES), lambda b, hp, i: (b, hp)),
                  pl.BlockSpec((seq, LANES), lambda b, hp, i: (b, hp)),
                  pl.BlockSpec((HEADS_PER_TILE, ALIBI_ROWS, LANES), lambda b, hp, i: (hp, 0, 0))],
        out_specs=pl.BlockSpec((tile, LANES), lambda b, hp, i: (b * nq + i, hp)),
        out_shape=jax.ShapeDtypeStruct((batch * seq, width), F32),
        scratch_shapes=[pltpu.VMEM((HEADS_PER_TILE, seq, LANES), BF16), pltpu.VMEM((HEADS_PER_TILE, seq, LANES), BF16),
                        pltpu.VMEM((HEADS_PER_TILE, LANES, LANES), F32)],
        compiler_params=pltpu.CompilerParams(dimension_semantics=("arbitrary", "arbitrary", "arbitrary")),
        name="moba_prompt",
    )(q, k, v, _alibi_table(MOBA_HEADS, n_blocks))


def _fox_gate_body(f_ref, b_ref, lf_ref, cc_ref, carry_scr, *, tiles_per_seq):
    i = pl.program_id(0)
    tm = f_ref.shape[0]

    @pl.when(i % tiles_per_seq == 0)
    def _():
        carry_scr[...] = jnp.zeros_like(carry_scr)

    lf = _log_sigmoid(f_ref[...] + b_ref[...])
    lf_ref[...] = lf
    tril = (_iota((tm, tm), 0) >= _iota((tm, tm), 1)).astype(F32)
    c = _dot(tril, lf, precision=HIGHEST) + carry_scr[...]
    carry_scr[...] = c[tm - 1:tm, :]
    cc_ref[...] = c


def fox_gates(f, bias, seq, tm):
    m = f.shape[0]
    assert seq % tm == 0 and m % seq == 0
    return pl.pallas_call(
        functools.partial(_fox_gate_body, tiles_per_seq=seq // tm),
        grid=(m // tm,),
        in_specs=[pl.BlockSpec((tm, GATE_PAD), lambda i: (i, 0)), pl.BlockSpec((1, GATE_PAD), lambda i: (0, 0))],
        out_specs=[pl.BlockSpec((tm, GATE_PAD), lambda i: (i, 0)), pl.BlockSpec((tm, GATE_PAD), lambda i: (i, 0))],
        out_shape=[jax.ShapeDtypeStruct((m, GATE_PAD), F32), jax.ShapeDtypeStruct((m, GATE_PAD), F32)],
        scratch_shapes=[pltpu.VMEM((1, GATE_PAD), F32)],
        compiler_params=pltpu.CompilerParams(dimension_semantics=("arbitrary",)),
        name="fox_gates",
    )(f, bias)


def _fox_body(q_ref, k_ref, v_ref, cq_ref, ck_ref, o_ref, ka_scr, vb_scr, *, tile):
    hp = pl.program_id(1)
    i = pl.program_id(2)
    seq = k_ref.shape[0]
    d = HEAD_DIM

    def head_column(c_all, hh):
        lane_g = _iota(c_all.shape, 1)
        return jnp.sum(jnp.where(lane_g == hp * HEADS_PER_TILE + hh, c_all, 0.0), axis=-1, keepdims=True)

    def sum_cols(idx, c_col, on_pos):
        cols = jnp.zeros(idx.shape, F32)
        for n, p in enumerate(_bf16_pieces(c_col)):
            key_side = on_pos == "key"
            cols = jnp.where(idx == n, 1.0 if key_side else p, cols)
            cols = jnp.where(idx == SPLIT_PIECES + n, -p if key_side else 1.0, cols)
        return cols

    @pl.when(i == 0)
    def _():
        k = k_ref[...]
        v = v_ref[...]
        lane = _iota((seq, LANES), 1)
        c_all = ck_ref[...]
        for hh in range(HEADS_PER_TILE):
            extras = sum_cols(lane - _extra_lane0(hh), head_column(c_all, hh), "key")
            ka_scr[hh] = jnp.where(_data_lanes(lane, hh), k, extras).astype(BF16)
            vb_scr[hh] = _values_with_ones(v, lane, hh)

    q = q_ref[...]
    lane = _iota((tile, LANES), 1)
    c_all = cq_ref[...]
    qa = []
    for hh in range(HEADS_PER_TILE):
        extras = sum_cols(lane - _extra_lane0(hh), head_column(c_all, hh), "query")
        qa.append(jnp.where(_data_lanes(lane, hh), q * (d ** -0.5), extras).astype(BF16))
    o_ref[...] = _flash_pair(qa, ka_scr, vb_scr, i, tile)


def fox_prompt(q, k, v, c_col, batch, seq):
    width = FOX_HEADS * HEAD_DIM
    tile = min(ATT_TILE, seq)
    assert seq % tile == 0 and 2 * SPLIT_PIECES <= HEAD_DIM
    nq = seq // tile
    return pl.pallas_call(
        functools.partial(_fox_body, tile=tile),
        grid=(batch, width // LANES, nq),
        in_specs=[pl.BlockSpec((tile, LANES), lambda b, hp, i: (b * nq + i, hp)),
                  pl.BlockSpec((seq, LANES), lambda b, hp, i: (b, hp)),
                  pl.BlockSpec((seq, LANES), lambda b, hp, i: (b, hp)),
                  pl.BlockSpec((tile, GATE_PAD), lambda b, hp, i: (b * nq + i, 0)),
                  pl.BlockSpec((seq, GATE_PAD), lambda b, hp, i: (b, 0))],
        out_specs=pl.BlockSpec((tile, LANES), lambda b, hp, i: (b * nq + i, hp)),
        out_shape=jax.ShapeDtypeStruct((batch * seq, width), F32),
        scratch_shapes=[pltpu.VMEM((HEADS_PER_TILE, seq, LANES), BF16), pltpu.VMEM((HEADS_PER_TILE, seq, LANES), BF16)],
        compiler_params=pltpu.CompilerParams(dimension_semantics=("arbitrary", "arbitrary", "arbitrary")),
        name="fox_prompt",
    )(q, k, v, c_col, c_col)


def _block_diag_queries(q, n_heads, n_tok):
    rows, width = n_heads * n_tok, n_heads * HEAD_DIM
    tiled = jnp.concatenate([q] * n_heads, axis=0)
    same = (_iota((rows, width), 0) // n_tok) == (_iota((rows, width), 1) // HEAD_DIM)
    return jnp.where(same, tiled, 0.0)


def _head_diag(acc, n_heads, n_tok):
    width = n_heads * HEAD_DIM
    lane_head = _iota((n_tok, width), 1) // HEAD_DIM
    out = jnp.zeros((n_tok, width), F32)
    for h in range(n_heads):
        out = jnp.where(lane_head == h, acc[h * n_tok:(h + 1) * n_tok, :], out)
    return out


def _new_token_partial(qb, kn, vn, bias, n_tok):
    rows = qb.shape[0]
    s = _dot_nt(qb, kn.astype(BF16)) + bias
    tok = _iota((rows, n_tok), 0) % n_tok
    s = jnp.where(tok >= _iota((rows, n_tok), 1), s, NEG_INF)
    m = jnp.max(s, axis=-1, keepdims=True)
    p = jnp.exp(s - m)
    l = jnp.sum(p, axis=-1, keepdims=True)
    return m, l, _dot(p.astype(BF16), vn.astype(BF16))


def _moba_dec_body(pt_ref, q_ref, kn_ref, vn_ref, sl_ref, *refs, pages_per_step, page, n_tok, past_len):
    pp = pages_per_step
    k_refs, v_refs = refs[:pp], refs[pp:2 * pp]
    o_ref = refs[2 * pp]
    qb_scr, g_scr, m_scr, l_scr, acc_scr = refs[2 * pp + 1:]
    s_id = pl.program_id(1)
    nh, d = MOBA_HEADS, HEAD_DIM
    rows = nh * n_tok
    ppb = MOBA_BLOCK // page
    n_past = past_len // MOBA_BLOCK
    scale = d ** -0.5
    slope = sl_ref[...]
    tok = (_iota((rows, 1), 0) % n_tok).astype(F32)
    lane = _iota((rows, LANES), 1)

    @pl.when(s_id == 0)
    def _():
        qb_scr[...] = (_block_diag_queries(q_ref[...], nh, n_tok) * scale).astype(BF16)
        g_scr[...] = jnp.zeros_like(g_scr)
        m_scr[...] = jnp.zeros_like(m_scr)
        l_scr[...] = jnp.zeros_like(l_scr)

    qb = qb_scr[...]
    blocks = range(pp // ppb)
    pages = range(pp)
    blk0 = s_id * (pp // ppb)
    qk = [_dot(qb, k_refs[u][...].astype(BF16)) for u in pages]
    qk_sum = [jnp.sum(qk[u], axis=-1, keepdims=True) for u in pages]
    pos0 = (blk0 * MOBA_BLOCK).astype(F32) + _iota((rows, page), 1).astype(F32)
    s = [qk[u] - slope * ((past_len + tok) - (pos0 + float(u * page))) for u in pages]
    s_max = [jnp.max(s[u], axis=-1, keepdims=True) for u in pages]
    m = [functools.reduce(jnp.maximum, s_max[g * ppb:(g + 1) * ppb]) for g in blocks]
    p = [jnp.exp(s[u] - m[u // ppb]) for u in pages]
    p_sum = [jnp.sum(p[u], axis=-1, keepdims=True) for u in pages]
    pv = [_dot_nt(p[u].astype(BF16), v_refs[u][...].astype(BF16)) for u in pages]
    g_all, m_all, l_all = g_scr[...], m_scr[...], l_scr[...]
    for g in blocks:
        mine = slice(g * ppb, (g + 1) * ppb)
        gate = _add_all(qk_sum[mine]) * (1.0 / (scale * MOBA_BLOCK))
        g_all = jnp.where(lane == blk0 + g, gate, g_all)
        m_all = jnp.where(lane == blk0 + g, m[g], m_all)
        l_all = jnp.where(lane == blk0 + g, _add_all(p_sum[mine]), l_all)
        acc_scr[blk0 + g] = _add_all(pv[mine])
    g_scr[...], m_scr[...], l_scr[...] = g_all, m_all, l_all

    @pl.when(s_id == pl.num_programs(1) - 1)
    def _():
        sel = _topk_blocks(g_scr[...], n_past, n_past, axis=1)
        m_all = m_scr[...]
        t_k = _iota((rows, n_tok), 1).astype(F32)
        m_own, l_own, acc_own = _new_token_partial(qb, kn_ref[...], vn_ref[...], -slope * (tok - t_k), n_tok)
        m_tot = jnp.maximum(m_own, jnp.max(jnp.where(sel, m_all, NEG_INF), axis=-1, keepdims=True))
        w = jnp.where(sel, jnp.exp(m_all - m_tot), 0.0)
        w_own = jnp.exp(m_own - m_tot)
        l_tot = w_own * l_own + jnp.sum(w * l_scr[...], axis=-1, keepdims=True)
        acc = w_own * acc_own
        for n in range(n_past):
            acc = acc + w[:, n:n + 1] * acc_scr[n]
        o_ref[...] = _head_diag(acc / l_tot, nh, n_tok)


def _pages_position_minor(cache):
    n_phys, page, heads, d = cache.shape
    return jnp.transpose(cache, (0, 2, 3, 1)).reshape(n_phys, heads * d, page)


def moba_decode(q, k_new, v_new, cache_k, cache_v, page_table, n_tok, pages_per_step):
    n_seq, n_pages = page_table.shape
    page = cache_k.shape[1]
    nh, d = MOBA_HEADS, HEAD_DIM
    width = nh * d
    past_len = n_pages * page
    rows = nh * n_tok
    assert MOBA_BLOCK % page == 0 and past_len % MOBA_BLOCK == 0 and n_tok <= MOBA_BLOCK
    assert pages_per_step % (MOBA_BLOCK // page) == 0 and n_pages % pages_per_step == 0
    n_past = past_len // MOBA_BLOCK
    ck = _pages_position_minor(cache_k)
    cv = _pages_position_minor(cache_v)
    slope_rows = jnp.asarray(np.repeat(
        np.array([2.0 ** (-8.0 * (i + 1) / nh) for i in range(nh)], dtype=np.float32), n_tok).reshape(rows, 1))
    tok_spec = pl.BlockSpec((n_tok, width), lambda b, s, pt: (b, 0))

    assert n_past <= LANES

    def page_spec(u):
        return pl.BlockSpec((None, width, page), lambda b, s, pt: (pt[b, s * pages_per_step + u], 0, 0))

    grid_spec = pltpu.PrefetchScalarGridSpec(
        num_scalar_prefetch=1,
        grid=(n_seq, n_pages // pages_per_step),
        in_specs=[tok_spec, tok_spec, tok_spec, pl.BlockSpec((rows, 1), lambda b, s, pt: (0, 0))]
        + [page_spec(u) for u in range(pages_per_step)] * 2,
        out_specs=tok_spec,
        scratch_shapes=[pltpu.VMEM((rows, width), BF16), pltpu.VMEM((rows, LANES), F32),
                        pltpu.VMEM((rows, LANES), F32), pltpu.VMEM((rows, LANES), F32),
                        pltpu.VMEM((n_past, rows, width), F32)],
    )
    return pl.pallas_call(
        functools.partial(_moba_dec_body, pages_per_step=pages_per_step, page=page, n_tok=n_tok, past_len=past_len),
        grid_spec=grid_spec,
        out_shape=jax.ShapeDtypeStruct((n_seq * n_tok, width), F32),
        compiler_params=pltpu.CompilerParams(dimension_semantics=("arbitrary", "arbitrary")),
        name="moba_decode",
    )(page_table, q, k_new, v_new, slope_rows, *([ck] * pages_per_step), *([cv] * pages_per_step))


def _fox_dec_body(pt_ref, q_ref, kn_ref, vn_ref, f_ref, fb_ref, *refs, pages_per_step, page, n_tok):
    pp = pages_per_step
    k_refs, v_refs, lf_refs = refs[:pp], refs[pp:2 * pp], refs[2 * pp:3 * pp]
    o_ref, lfn_ref = refs[3 * pp], refs[3 * pp + 1]
    qb_scr, carry_scr, ctot_scr, m_scr, l_scr, acc_scr = refs[3 * pp + 2:]
    s_id = pl.program_id(1)
    nh, d = FOX_HEADS, HEAD_DIM
    rows = nh * n_tok
    expand = ((_iota((rows, nh), 0) // n_tok) == _iota((rows, nh), 1)).astype(F32)
    triu = (_iota((page, page), 0) <= _iota((page, page), 1)).astype(BF16)

    @pl.when(s_id == 0)
    def _():
        qb_scr[...] = (_block_diag_queries(q_ref[...], nh, n_tok) * (d ** -0.5)).astype(BF16)
        carry_scr[...] = jnp.zeros_like(carry_scr)
        m_scr[...] = jnp.full_like(m_scr, NEG_INF)
        l_scr[...] = jnp.zeros_like(l_scr)
        acc_scr[...] = jnp.zeros_like(acc_scr)

    qb = qb_scr[...]
    pages = range(pp)
    qk = [_dot(qb, k_refs[u][...].astype(BF16)) for u in pages]
    pieces = [jnp.concatenate([p.astype(BF16) for p in _bf16_pieces(lf_refs[u][...])], axis=0) for u in pages]
    sums = [_dot(pieces[u], triu) for u in pages]
    local = [_add_all([sums[u][n * nh:(n + 1) * nh] for n in range(SPLIT_PIECES)]) for u in pages]
    c_heads, offset = [], carry_scr[...]
    for u in pages:
        c_heads.append(local[u] + offset)
        offset = c_heads[u][:, page - 1:page]
    c_rows = [jnp.broadcast_to(c_heads[u][:, None, :], (nh, n_tok, page)).reshape(rows, page) for u in pages]
    ss = [qk[u] - c_rows[u] for u in pages]
    carry_scr[...] = offset
    ctot_scr[...] = c_rows[-1][:, page - 1:page]
    m_prev = m_scr[...]
    m_new = m_prev
    for s in ss:
        m_new = jnp.maximum(m_new, jnp.max(s, axis=-1, keepdims=True))
    alpha = jnp.exp(m_prev - m_new)
    p = [jnp.exp(ss[u] - m_new) for u in pages]
    p_sum = [jnp.sum(p[u], axis=-1, keepdims=True) for u in pages]
    pv = [_dot_nt(p[u].astype(BF16), v_refs[u][...].astype(BF16)) for u in pages]
    acc_scr[...] = alpha * acc_scr[...] + _add_all(pv)
    l_scr[...] = alpha * l_scr[...] + _add_all(p_sum)
    m_scr[...] = m_new

    @pl.when(s_id == pl.num_programs(1) - 1)
    def _():
        lf_new = _log_sigmoid(f_ref[...] + fb_ref[...])
        lfn_ref[...] = lf_new
        tril_t = (_iota((n_tok, n_tok), 0) >= _iota((n_tok, n_tok), 1)).astype(F32)
        c_new = _dot(tril_t, lf_new, precision=HIGHEST)[:, :nh]
        c_new_rows = jnp.concatenate([c_new] * nh, axis=0)
        c_q = jnp.sum(expand * c_new_rows, axis=-1, keepdims=True)
        m_past = m_scr[...] + (ctot_scr[...] + c_q)
        c_k = _dot_nt(expand, c_new, precision=HIGHEST)
        m_own, l_own, acc_own = _new_token_partial(qb, kn_ref[...], vn_ref[...], c_q - c_k, n_tok)
        m_tot = jnp.maximum(m_past, m_own)
        w_past = jnp.exp(m_past - m_tot)
        w_own = jnp.exp(m_own - m_tot)
        l_tot = w_past * l_scr[...] + w_own * l_own
        acc = w_past * acc_scr[...] + w_own * acc_own
        o_ref[...] = _head_diag(acc / l_tot, nh, n_tok)


def fox_decode(q, k_new, v_new, f_new, f_bias, cache_k, cache_v, cache_lf, page_table, n_tok, pages_per_step):
    n_seq, n_pages = page_table.shape
    page = cache_k.shape[1]
    nh, d = FOX_HEADS, HEAD_DIM
    width = nh * d
    rows = nh * n_tok
    assert n_pages % pages_per_step == 0
    assert n_tok % SUBLANES == 0
    ck = _pages_position_minor(cache_k)
    cv = _pages_position_minor(cache_v)
    clf =jnp.transpose(cache_lf, (0, 2, 1))
    tok_spec = pl.BlockSpec((n_tok, width), lambda b, s, pt: (b, 0))
    gate_spec = pl.BlockSpec((n_tok, GATE_PAD), lambda b, s, pt: (b, 0))

    def page_spec(u, dims):
        return pl.BlockSpec((None,) + dims, lambda b, s, pt: (pt[b, s * pages_per_step + u],) + (0,) * len(dims))

    grid_spec = pltpu.PrefetchScalarGridSpec(
        num_scalar_prefetch=1,
        grid=(n_seq, n_pages // pages_per_step),
        in_specs=[tok_spec, tok_spec, tok_spec, gate_spec, pl.BlockSpec((1, GATE_PAD), lambda b, s, pt: (0, 0))]
        + [page_spec(u, (width, page)) for u in range(pages_per_step)] * 2
        + [page_spec(u, (nh, page)) for u in range(pages_per_step)],
        out_specs=[tok_spec, gate_spec],
        scratch_shapes=[pltpu.VMEM((rows, width), BF16), pltpu.VMEM((nh, 1), F32), pltpu.VMEM((rows, 1), F32),
                        pltpu.VMEM((rows, 1), F32), pltpu.VMEM((rows, 1), F32), pltpu.VMEM((rows, width), F32)],
    )
    return pl.pallas_call(
        functools.partial(_fox_dec_body, pages_per_step=pages_per_step, page=page, n_tok=n_tok),
        grid_spec=grid_spec,
        out_shape=[jax.ShapeDtypeStruct((n_seq * n_tok, width), F32),
                   jax.ShapeDtypeStruct((n_seq * n_tok, GATE_PAD), F32)],
        compiler_params=pltpu.CompilerParams(dimension_semantics=("arbitrary", "arbitrary")),
        name="fox_decode",
    )(page_table, q, k_new, v_new, f_new, f_bias,
      *([ck] * pages_per_step), *([cv] * pages_per_step), *([clf] * pages_per_step))


def _pad_lanes(w, width):
    return jnp.pad(w, ((0, 0), (0, width - w.shape[1])))


def _even_pieces(w_in):
    nq = MLSTM_HEADS * MLSTM_DQK
    nv = MLSTM_HEADS * MLSTM_DV
    nm = MOBA_HEADS * HEAD_DIM
    sizes = (nq, nq, nv, nv, 2 * MLSTM_HEADS, nm, nm, nm)
    pieces, start = [], 0
    for s in sizes:
        pieces.append(w_in[:, start:start + s])
        start += s
    assert start == w_in.shape[1]
    pieces[4] = _pad_lanes(pieces[4], GATE_PAD)
    return [p.astype(BF16) for p in pieces]


def _odd_pieces(w_in):
    nw = FOX_HEADS * HEAD_DIM
    pieces = [w_in[:, 0:nw], w_in[:, nw:2 * nw], w_in[:, 2 * nw:3 * nw], _pad_lanes(w_in[:, 3 * nw:], GATE_PAD)]
    assert w_in.shape[1] == 3 * nw + FOX_HEADS
    return [p.astype(BF16) for p in pieces]


def _token_major_view(x_t, heads):
    batch, _, seq = x_t.shape
    return jnp.transpose(x_t.reshape(batch, heads, HEAD_DIM, seq), (0, 3, 1, 2))


PROMPT_TM = 512
FFN_TM = 512
MLSTM_PROMPT_BATCH = 4
MLSTM_SAMPLE_BATCH = 8
FOX_GATE_TM = 512
MOBA_PAGES_PER_STEP = 16
FOX_PAGES_PER_STEP = 8


def kernel(x_prompt, x_sample, cache_moba_k, cache_moba_v, state_mlstm_C, state_mlstm_n, state_mlstm_m, cache_fox_k, cache_fox_v, cache_fox_logf, page_table, norm_mix_g, norm_ffn_g, norm_final_g, even_w_in, even_b_ig, even_b_fg, even_head_norm_g, even_w_out, odd_w_in, odd_b_f, odd_w_out, ffn_w_gate, ffn_w_up, ffn_w_down):
    bp, tp, d = x_prompt.shape
    bs, ts, _ = x_sample.shape
    depth = norm_mix_g.shape[0]
    mp, ms = bp * tp, bs * ts
    hp = x_prompt.reshape(mp, d)
    hs = x_sample.reshape(ms, d)
    tm_p = min(PROMPT_TM, mp)
    tm_s = ms
    ffn_tm_p = min(FFN_TM, mp)
    nv = MLSTM_HEADS * MLSTM_DV
    n_pages = page_table.shape[1]
    pe, se, po, so = [], [], [], []
    for layer in range(depth):
        li = layer // 2
        final = layer == depth - 1
        if layer % 2 == 0:
            ws = _even_pieces(even_w_in[li])
            gate_bias = _pad_lanes(jnp.concatenate([even_b_ig[li], even_b_fg[li]]).reshape(1, -1), GATE_PAD)
            w_out = even_w_out[li].astype(BF16)
            w_out_a, w_out_b = w_out[:nv], w_out[nv:]
            zeros = lambda *s: jnp.zeros(s, F32)
            q1, k1, v1, o1, gt, q2, k2, v2, k2_t, v2_t = norm_proj(
                hp, norm_mix_g[layer], ws, tm_p, ws_t=(ws[6].T, ws[7].T), seq=tp)
            y1, c_p, n_p, m_p = mlstm(q1, k1, v1, o1, gt, gate_bias, even_head_norm_g[li],
                                      zeros(bp, MLSTM_HEADS, MLSTM_DQK, MLSTM_DV), zeros(bp, MLSTM_HEADS, MLSTM_DQK),
                                      zeros(bp, MLSTM_HEADS), bp, tp, math.gcd(bp, MLSTM_PROMPT_BATCH))
            y2 = moba_prompt(q2, k2, v2, bp, tp)
            hp = proj_res(hp, [y1, y2], [w_out_a, w_out_b], tm_p)
            pe.append((_token_major_view(k2_t, MOBA_HEADS), _token_major_view(v2_t, MOBA_HEADS), c_p, n_p, m_p))
            q1, k1, v1, o1, gt, q2, k2, v2 = norm_proj(hs, norm_mix_g[layer], ws, tm_s)
            y1, c_s, n_s, m_s = mlstm(q1, k1, v1, o1, gt, gate_bias, even_head_norm_g[li],
                                      state_mlstm_C[li], state_mlstm_n[li], state_mlstm_m[li], bs, ts,
                                      math.gcd(bs, MLSTM_SAMPLE_BATCH))
            y2 = moba_decode(q2, k2, v2, cache_moba_k[li], cache_moba_v[li], page_table, ts,
                             min(MOBA_PAGES_PER_STEP, n_pages))
            hs = proj_res(hs, [y1, y2], [w_out_a, w_out_b], tm_s)
            se.append((k2.reshape(bs, ts, MOBA_HEADS, HEAD_DIM), v2.reshape(bs, ts, MOBA_HEADS, HEAD_DIM), c_s, n_s, m_s))
        else:
            ws = _odd_pieces(odd_w_in[li])
            f_bias = _pad_lanes(odd_b_f[li].reshape(1, -1), GATE_PAD)
            w_out = odd_w_out[li].astype(BF16)
            q, k, v, f, k_t, v_t = norm_proj(hp, norm_mix_g[layer], ws, tm_p, ws_t=(ws[1].T, ws[2].T), seq=tp)
            lf, c_col = fox_gates(f, f_bias, tp, min(FOX_GATE_TM, tp))
            y = fox_prompt(q, k, v, c_col, bp, tp)
            hp = proj_res(hp, [y], [w_out], tm_p)
            po.append((_token_major_view(k_t, FOX_HEADS), _token_major_view(v_t, FOX_HEADS),
                       lf[:, :FOX_HEADS].reshape(bp, tp, FOX_HEADS)))
            q, k, v, f = norm_proj(hs, norm_mix_g[layer], ws, tm_s)
            y, lf = fox_decode(q, k, v, f, f_bias, cache_fox_k[li], cache_fox_v[li], cache_fox_logf[li],
                               page_table, ts, min(FOX_PAGES_PER_STEP, n_pages))
            hs = proj_res(hs, [y], [w_out], tm_s)
            so.append((k.reshape(bs, ts, FOX_HEADS, HEAD_DIM), v.reshape(bs, ts, FOX_HEADS, HEAD_DIM),
                       lf[:, :FOX_HEADS].reshape(bs, ts, FOX_HEADS)))
        wg, wu, wd = ffn_w_gate[layer].astype(BF16), ffn_w_up[layer].astype(BF16), ffn_w_down[layer].astype(BF16)
        hp = ffn(hp, norm_ffn_g[layer], norm_final_g, wg, wu, wd, ffn_tm_p, final)
        hs = ffn(hs, norm_ffn_g[layer], norm_final_g, wg, wu, wd, tm_s, final)
    stack = lambda group, idx: jnp.stack([s[idx] for s in group])
    return (hp.reshape(bp, tp, d), hs.reshape(bs, ts, d),
            stack(pe, 0), stack(pe, 1), stack(pe, 2), stack(pe, 3), stack(pe, 4),
            stack(po, 0), stack(po, 1), stack(po, 2),
            stack(se, 0), stack(se, 1), stack(se, 2), stack(se, 3), stack(se, 4),
            stack(so, 0), stack(so, 1), stack(so, 2))
```

```python
import functools
import math

import jax
import jax.numpy as jnp
import ml_dtypes
import numpy as np
from jax import lax
from jax.experimental import pallas as pl
from jax.experimental.pallas import tpu as pltpu

F32 = jnp.float32
BF16 = jnp.bfloat16
HIGHEST = lax.Precision.HIGHEST

LANES = 128
SUBLANES = 8

HEAD_DIM = 64
MLSTM_HEADS = 4
MLSTM_DQK = 64
MLSTM_DV = 128
MLSTM_CHUNK = 64
MOBA_HEADS = 8
MOBA_BLOCK = 256
MOBA_TOPK = 3
FOX_HEADS = 16
RMS_EPS = 1e-6
GATE_PAD = LANES
HEADS_PER_TILE = LANES // HEAD_DIM

NEG_INF = float("-inf")
MASK_BIAS = -1e30
SPLIT_PIECES = 3


def _log_sigmoid(x):
    return -(jnp.maximum(-x, 0.0) + jnp.log1p(jnp.exp(-jnp.abs(x))))


def _rmsnorm(x, g):
    return x * lax.rsqrt(jnp.mean(x * x, axis=-1, keepdims=True) + RMS_EPS) * g


def _dot(a, b, precision=None):
    return jnp.dot(a, b, precision=precision, preferred_element_type=F32)


def _dot_nt(a, b, precision=None):
    return lax.dot_general(a, b, (((1,), (1,)), ((), ())), precision=precision, preferred_element_type=F32)


def _iota(shape, dim):
    return lax.broadcasted_iota(jnp.int32, shape, dim)


def _add_all(xs):
    return functools.reduce(lambda a, b: a + b, xs)


def _bf16_pieces(x):
    pieces, rest = [], x
    for _ in range(SPLIT_PIECES):
        p = rest.astype(BF16).astype(F32)
        pieces.append(p)
        rest = rest - p
    return pieces


def _norm_proj_body(x_ref, g_ref, *refs, n_w, n_t):
    w_refs, wt_refs = refs[:n_w], refs[n_w:n_w + n_t]
    o_refs, ot_refs = refs[n_w + n_t:2 * n_w + n_t], refs[2 * n_w + n_t:]
    xb = _rmsnorm(x_ref[...], g_ref[...]).astype(BF16)
    for w_ref, o_ref in zip(w_refs, o_refs):
        o_ref[...] = _dot(xb, w_ref[...])
    for wt_ref, ot_ref in zip(wt_refs, ot_refs):
        ot_ref[...] = _dot_nt(wt_ref[...], xb)


def norm_proj(x, g, ws, tm, ws_t=(), seq=None):
    m, d = x.shape
    assert m % tm == 0
    in_specs = [pl.BlockSpec((tm, d), lambda i: (i, 0)), pl.BlockSpec((1, d), lambda i: (0, 0))]
    in_specs += [pl.BlockSpec(w.shape, lambda i: (0, 0)) for w in tuple(ws) + tuple(ws_t)]
    out_specs = [pl.BlockSpec((tm, w.shape[1]), lambda i: (i, 0)) for w in ws]
    out_shape = [jax.ShapeDtypeStruct((m, w.shape[1]), F32) for w in ws]
    if ws_t:
        assert seq % tm == 0 and m % seq == 0
        per_seq = seq // tm
        out_specs += [pl.BlockSpec((None, w.shape[0], tm), lambda i: (i // per_seq, 0, i % per_seq)) for w in ws_t]
        out_shape += [jax.ShapeDtypeStruct((m // seq, w.shape[0], seq), F32) for w in ws_t]
    return pl.pallas_call(
        functools.partial(_norm_proj_body, n_w=len(ws), n_t=len(ws_t)),
        grid=(m // tm,), in_specs=in_specs, out_specs=out_specs, out_shape=out_shape,
        name="norm_proj",
    )(x, g.reshape(1, d), *ws, *ws_t)


def _proj_res_body(h_ref, *refs, n_a):
    a_refs, w_refs, o_ref = refs[:n_a], refs[n_a:2 * n_a], refs[2 * n_a]
    acc = h_ref[...]
    for a_ref, w_ref in zip(a_refs, w_refs):
        acc = acc + _dot(a_ref[...].astype(BF16), w_ref[...])
    o_ref[...] = acc


def proj_res(h, acts, ws, tm):
    m, d = h.shape
    assert m % tm == 0
    in_specs = [pl.BlockSpec((tm, d), lambda i: (i, 0))]
    in_specs += [pl.BlockSpec((tm, a.shape[1]), lambda i: (i, 0)) for a in acts]
    in_specs += [pl.BlockSpec(w.shape, lambda i: (0, 0)) for w in ws]
    return pl.pallas_call(
        functools.partial(_proj_res_body, n_a=len(acts)),
        grid=(m // tm,), in_specs=in_specs,
        out_specs=pl.BlockSpec((tm, d), lambda i: (i, 0)),
        out_shape=jax.ShapeDtypeStruct((m, d), F32),
        name="proj_res",
    )(h, *acts, *ws)


def _ffn_body(x_ref, g_ref, gf_ref, wg_ref, wu_ref, wd_ref, o_ref, *, final_norm):
    x = x_ref[...]
    xn = _rmsnorm(x, g_ref[...]).astype(BF16)
    gate = _dot(xn, wg_ref[...])
    up = _dot(xn, wu_ref[...])
    act = (gate * jax.nn.sigmoid(gate) * up).astype(BF16)
    y = x + _dot(act, wd_ref[...])
    o_ref[...] = _rmsnorm(y, gf_ref[...]) if final_norm else y


def ffn(h, g, g_final, wg, wu, wd, tm, final_norm):
    m, d = h.shape
    assert m % tm == 0
    resident = lambda w: pl.BlockSpec(w.shape, lambda i: (0, 0), pipeline_mode=pl.Buffered(1))
    return pl.pallas_call(
        functools.partial(_ffn_body, final_norm=final_norm),
        grid=(m // tm,),
        in_specs=[
            pl.BlockSpec((tm, d), lambda i: (i, 0)),
            pl.BlockSpec((1, d), lambda i: (0, 0)),
            pl.BlockSpec((1, d), lambda i: (0, 0)),
            resident(wg), resident(wu), resident(wd),
        ],
        out_specs=pl.BlockSpec((tm, d), lambda i: (i, 0)),
        out_shape=jax.ShapeDtypeStruct((m, d), F32),
        name="ffn",
    )(h, g.reshape(1, d), g_final.reshape(1, d), wg, wu, wd)


def _mlstm_body(q_ref, k_ref, v_ref, o_ref, g_ref, gb_ref, gh_ref, c0_ref, n0_ref, m0_ref,
                y_ref, c_ref, n_ref, m_ref, c_scr, n_scr, m_scr, *, chunk, n_batch):
    j = pl.program_id(1)
    nh, dk, dv = MLSTM_HEADS, MLSTM_DQK, MLSTM_DV

    @pl.when(j == 0)
    def _():
        c_scr[...] = c0_ref[...]
        n_scr[...] = n0_ref[...]
        m_scr[...] = m0_ref[...]

    lane = _iota((chunk, GATE_PAD), 1)
    r_i = _iota((chunk, chunk), 0)
    c_i = _iota((chunk, chunk), 1)
    causal = r_i >= c_i
    eye = r_i == c_i
    tril = causal.astype(F32)
    eye_b = (_iota((dk, dk), 0) == _iota((dk, dk), 1)).astype(BF16)

    def to_row(col):
        return jnp.sum(jnp.where(eye, col, 0.0), axis=0, keepdims=True)

    chains = [(b, h) for b in range(n_batch) for h in range(nh)]
    each = lambda fn: [fn(n, b, h) for n, (b, h) in enumerate(chains)]

    acts, cums = [], []
    for b in range(n_batch):
        gpre = g_ref[b] + gb_ref[...]
        acts.append(jnp.where(lane < nh, gpre, _log_sigmoid(gpre)))
        cums.append(_dot(tril, acts[b], precision=HIGHEST))
    q = each(lambda n, b, h: q_ref[b, :, h * dk:(h + 1) * dk] * (dk ** -0.5))
    k = each(lambda n, b, h: k_ref[b, :, h * dk:(h + 1) * dk])
    vb = each(lambda n, b, h: v_ref[b, :, h * dv:(h + 1) * dv].astype(BF16))
    qb = each(lambda n, b, h: q[n].astype(BF16))
    ig_c = each(lambda n, b, h: acts[b][:, h:h + 1])
    b_c = each(lambda n, b, h: cums[b][:, nh + h:nh + h + 1])
    m_prev = each(lambda n, b, h: m_scr[b, h])
    c_prev = each(lambda n, b, h: c_scr[b, h])
    n_prev = each(lambda n, b, h: n_scr[b, h])
    dmat = each(lambda n, b, h: jnp.where(causal, b_c[n] - to_row(b_c[n]) + to_row(ig_c[n]), NEG_INF))
    inter = each(lambda n, b, h: b_c[n] + m_prev[n])
    mt = each(lambda n, b, h: jnp.maximum(inter[n], jnp.max(dmat[n], axis=-1, keepdims=True)))
    w_inter = each(lambda n, b, h: jnp.exp(inter[n] - mt[n]))
    qk = each(lambda n, b, h: _dot_nt(qb[n], k[n].astype(BF16)) * jnp.exp(dmat[n] - mt[n]))
    num = each(lambda n, b, h: w_inter[n] * _dot(qb[n], c_prev[n].astype(BF16)) + _dot(qk[n].astype(BF16), vb[n]))
    den = each(lambda n, b, h: w_inter[n] * jnp.sum(q[n] * n_prev[n], axis=-1, keepdims=True)
               + jnp.sum(qk[n], axis=-1, keepdims=True))
    hh = each(lambda n, b, h: num[n] / jnp.maximum(jnp.abs(den[n]), jnp.exp(-mt[n])))
    m_new = each(lambda n, b, h: mt[n][chunk - 1:chunk, :])
    b_last = each(lambda n, b, h: b_c[n][chunk - 1:chunk, :])
    decay = each(lambda n, b, h: jnp.exp(b_last[n] + m_prev[n] - m_new[n]))
    kw = each(lambda n, b, h: k[n] * jnp.exp(b_last[n] - b_c[n] + ig_c[n] - m_new[n]))
    kw_t = each(lambda n, b, h: _dot_nt(eye_b, kw[n].astype(BF16)).astype(BF16))
    for n, (b, h) in enumerate(chains):
        c_scr[b, h] = decay[n] * c_prev[n] + _dot(kw_t[n], vb[n])
        n_scr[b, h] = decay[n] * n_prev[n] + jnp.sum(kw[n], axis=0, keepdims=True)
        m_scr[b, h] = m_new[n]
    for n, (b, h) in enumerate(chains):
        hn = hh[n] * lax.rsqrt(jnp.mean(hh[n] * hh[n], axis=-1, keepdims=True) + RMS_EPS)
        hn = hn * gh_ref[:, h * dv:(h + 1) * dv]
        y_ref[b, :, h * dv:(h + 1) * dv] = jax.nn.sigmoid(o_ref[b, :, h * dv:(h + 1) * dv]) * hn

    @pl.when(j == pl.num_programs(1) - 1)
    def _():
        c_ref[...] = c_scr[...]
        n_ref[...] = n_scr[...]
        m_ref[...] = m_scr[...]


def mlstm(q, k, v, o, gates, gate_bias, g_head, c0, n0, m0, batch, seq, n_batch):
    nh, dk, dv = MLSTM_HEADS, MLSTM_DQK, MLSTM_DV
    chunk = MLSTM_CHUNK if seq % MLSTM_CHUNK == 0 else seq
    assert chunk % SUBLANES == 0 and batch % n_batch == 0
    steps = seq // chunk
    row_spec = lambda w: pl.BlockSpec((n_batch, chunk, w), lambda b, j: (b, j, 0))
    st4 = lambda s: pl.BlockSpec((n_batch,) + s, lambda b, j: (b, 0, 0, 0))
    rows3 = lambda x: x.reshape(batch, seq, x.shape[-1])
    y, c, n, m = pl.pallas_call(
        functools.partial(_mlstm_body, chunk=chunk, n_batch=n_batch),
        grid=(batch // n_batch, steps),
        in_specs=[row_spec(nh * dk), row_spec(nh * dk), row_spec(nh * dv), row_spec(nh * dv), row_spec(GATE_PAD),
                  pl.BlockSpec((1, GATE_PAD), lambda b, j: (0, 0)),
                  pl.BlockSpec((1, nh * dv), lambda b, j: (0, 0)),
                  st4((nh, dk, dv)), st4((nh, 1, dk)), st4((nh, 1, 1))],
        out_specs=[row_spec(nh * dv), st4((nh, dk, dv)), st4((nh, 1, dk)), st4((nh, 1, 1))],
        out_shape=[jax.ShapeDtypeStruct((batch, seq, nh * dv), F32),
                   jax.ShapeDtypeStruct((batch, nh, dk, dv), F32),
                   jax.ShapeDtypeStruct((batch, nh, 1, dk), F32),
                   jax.ShapeDtypeStruct((batch, nh, 1, 1), F32)],
        scratch_shapes=[pltpu.VMEM((n_batch, nh, dk, dv), F32), pltpu.VMEM((n_batch, nh, 1, dk), F32),
                        pltpu.VMEM((n_batch, nh, 1, 1), F32)],
        compiler_params=pltpu.CompilerParams(dimension_semantics=("arbitrary", "arbitrary")),
        name="mlstm",
    )(rows3(q), rows3(k), rows3(v), rows3(o), rows3(gates), gate_bias, g_head.reshape(1, nh * dv),
      c0, n0.reshape(batch, nh, 1, dk), m0.reshape(batch, nh, 1, 1))
    return y.reshape(batch * seq, nh * dv), c, n.reshape(batch, nh, dk), m.reshape(batch, nh)


def _topk_blocks(gate, n_blocks, n_valid, axis):
    idx = _iota(gate.shape, axis)
    rank = jnp.zeros(gate.shape, F32)
    for c in range(n_blocks):
        other = gate[c:c + 1, :] if axis == 0 else gate[:, c:c + 1]
        beats = (other > gate) | ((other == gate) & (c < idx))
        candidate = jnp.where(c < n_valid, 1.0, 0.0)
        rank = rank + jnp.where(beats, candidate, 0.0)
    return (rank < MOBA_TOPK) & (idx < n_valid)


ATT_TILE = 512
ATT_KEY_TILE = 512


def _data_lanes(lane, hh):
    return (lane >= hh * HEAD_DIM) & (lane < (hh + 1) * HEAD_DIM)


def _extra_lane0(hh):
    return (HEADS_PER_TILE - 1 - hh) * HEAD_DIM


def _values_with_ones(v, lane, hh):
    return jnp.where(_data_lanes(lane, hh), v, jnp.where(lane == _extra_lane0(hh), 1.0, 0.0)).astype(BF16)


def _flash_pair(qa, ka_scr, va_scr, i, tile):
    heads = range(HEADS_PER_TILE)
    ktile = min(ATT_KEY_TILE, tile)
    sub = tile // ktile
    rel = _iota((tile, ktile), 0) - _iota((tile, ktile), 1)

    def step(starts, carry, diag_part=None):
        tiles = range(len(starts))
        chains = [(hh, t) for t in tiles for hh in heads]
        s = {c: _dot_nt(qa[c[0]], ka_scr[c[0], pl.ds(starts[c[1]], ktile), :]) for c in chains}
        if diag_part is not None:
            s = {c: jnp.where(rel >= diag_part * ktile, s[c], NEG_INF) for c in chains}
        s_max = {c: jnp.max(s[c], axis=-1, keepdims=True) for c in chains}
        m = {(hh, -1): carry[hh][0] for hh in heads}
        for hh, t in chains:
            m[hh, t] = jnp.maximum(m[hh, t - 1], s_max[hh, t])
        p = {c: jnp.exp(s[c] - m[c]).astype(BF16) for c in chains}
        pv = {c: _dot(p[c], va_scr[c[0], pl.ds(starts[c[1]], ktile), :]) for c in chains}
        acc = [carry[hh][1] for hh in heads]
        for hh, t in chains:
            acc[hh] = jnp.exp(m[hh, t - 1] - m[hh, t]) * acc[hh] + pv[hh, t]
        return tuple((m[hh, len(starts) - 1], acc[hh]) for hh in heads)

    def key_start(j):
        return pl.multiple_of(j * ktile, ktile)

    carry = tuple((jnp.full((tile, 1), NEG_INF, F32), jnp.zeros((tile, LANES), F32)) for _ in heads)
    for part in range(sub):
        carry = step([key_start(i * sub + part)], carry, part)
    n_past = i * sub
    n_group = lax.shift_right_logical(n_past, 2)
    done = n_group * 4
    carry = lax.fori_loop(0, n_group, lambda j, c: step([key_start(4 * j + t) for t in range(4)], c), carry)
    carry = lax.cond(n_past - done >= 2, lambda c: step([key_start(done), key_start(done + 1)], c), lambda c: c, carry)
    carry = lax.cond((n_past & 1) == 1, lambda c: step([key_start(n_past - 1)], c), lambda c: c, carry)
    lane = _iota((tile, LANES), 1)
    out = jnp.zeros((tile, LANES), F32)
    for hh in heads:
        acc = carry[hh][1]
        e0 = _extra_lane0(hh)
        out = jnp.where(_data_lanes(lane, hh), acc / acc[:, e0:e0 + 1], out)
    return out


def _floor_div_pow2(x, n):
    assert n & (n - 1) == 0
    return lax.shift_right_logical(x, n.bit_length() - 1)


def _alibi_digits(pos):
    assert HEAD_DIM & (HEAD_DIM - 1) == 0
    lo = pos & (HEAD_DIM - 1)
    return (pos - lo).astype(F32), lo.astype(F32)


ALIBI_ROWS = 8


def _alibi_table(n_heads, n_blocks):
    rest = np.array([2.0 ** (-8.0 * (i + 1) / n_heads) for i in range(n_heads)], dtype=np.float32)
    table = np.zeros((n_heads, ALIBI_ROWS, LANES), np.float32)
    for n in range(SPLIT_PIECES):
        piece = rest.astype(ml_dtypes.bfloat16).astype(np.float32)
        rest = rest - piece
        for h in range(n_heads):
            base = _extra_lane0(h % HEADS_PER_TILE) + n_blocks + 4 * n
            table[h, 0, base + 2:base + 4] = piece[h]
            table[h, 1, base] = 1.0
            table[h, 2, base + 1] = 1.0
            table[h, 3, base:base + 2] = piece[h]
            table[h, 4, base + 2] = 1.0
            table[h, 5, base + 3] = 1.0
    return jnp.asarray(table)


def _moba_body(q_ref, k_ref, v_ref, al_ref, o_ref, ka_scr, vb_scr, km_scr, *, n_blocks, tile):
    i = pl.program_id(2)
    seq = k_ref.shape[0]
    d = HEAD_DIM

    def alibi_cols(hh, side, pos_hi, pos_lo):
        rows = al_ref[hh]
        if side == "query":
            return rows[0:1] - pos_hi * rows[1:2] - pos_lo * rows[2:3]
        return rows[3:4] + pos_hi * rows[4:5] + pos_lo * rows[5:6]

    @pl.when(i == 0)
    def _():
        k = k_ref[...]
        v = v_ref[...]
        lane = _iota((seq, LANES), 1)
        pos = _iota((seq, LANES), 0)
        pos_hi, pos_lo = _alibi_digits(pos)
        km = jnp.concatenate(
            [jnp.mean(k[n * MOBA_BLOCK:(n + 1) * MOBA_BLOCK], axis=0, keepdims=True) for n in range(n_blocks)], axis=0)
        for hh in range(HEADS_PER_TILE):
            e0 = _extra_lane0(hh)
            idx = lane - e0
            extras = jnp.where(idx == _floor_div_pow2(pos, MOBA_BLOCK), 1.0, alibi_cols(hh, "key", pos_hi, pos_lo))
            ka_scr[hh] = jnp.where(_data_lanes(lane, hh), k, extras).astype(BF16)
            vb_scr[hh] = _values_with_ones(v, lane, hh)
            km_scr[hh] = jnp.zeros((LANES, LANES), F32)
            km_scr[hh, e0:e0 + n_blocks, :] = km

    q = q_ref[...]
    lane = _iota((tile, LANES), 1)
    t_pos = i * tile + _iota((tile, LANES), 0)
    t_hi, t_lo = _alibi_digits(t_pos)
    blk_rows = -(-n_blocks // SUBLANES) * SUBLANES
    own_t = _floor_div_pow2(i * tile + _iota((1, tile), 1), MOBA_BLOCK)
    qa = []
    for hh in range(HEADS_PER_TILE):
        e0 = _extra_lane0(hh)
        idx = lane - e0
        data = _data_lanes(lane, hh)
        gate_t = _dot_nt(km_scr[hh], jnp.where(data, q, 0.0), precision=HIGHEST)[e0:e0 + blk_rows]
        keep_t = _topk_blocks(gate_t, n_blocks, own_t, axis=0) | (_iota((blk_rows, tile), 0) == own_t)
        above, below = e0, LANES - e0 - blk_rows
        bias_t = ([jnp.zeros((above, tile), F32)] if above else []) + [jnp.where(keep_t, 0.0, MASK_BIAS)]
        bias_t += [jnp.zeros((below, tile), F32)] if below else []
        block_bias = jnp.concatenate(bias_t, axis=0).T
        extras = jnp.where((idx >= 0) & (idx < n_blocks), block_bias, alibi_cols(hh, "query", t_hi, t_lo))
        qa.append(jnp.where(data, q * (d ** -0.5), extras).astype(BF16))
    o_ref[...] = _flash_pair(qa, ka_scr, vb_scr, i, tile)


def moba_prompt(q, k, v, batch, seq):
    width = MOBA_HEADS * HEAD_DIM
    tile = min(ATT_TILE, seq)
    assert seq % tile == 0 and tile % MOBA_BLOCK == 0
    n_blocks = seq // MOBA_BLOCK
    assert n_blocks + 4 * SPLIT_PIECES <= HEAD_DIM
    nq = seq // tile
    return pl.pallas_call(
        functools.partial(_moba_body, n_blocks=n_blocks, tile=tile),
        grid=(batch, width // LANES, nq),
        in_specs=[pl.BlockSpec((tile, LANES), lambda b, hp, i: (b * nq + i, hp)),
                  pl.BlockSpec((seq, LANES), lambda b, hp, i: (b, hp)),
                  pl.BlockSpec((seq, LANES), lambda b, hp, i: (b, hp)),
                  pl.BlockSpec((HEADS_PER_TILE, ALIBI_ROWS, LANES), lambda b, hp, i: (hp, 0, 0))],
        out_specs=pl.BlockSpec((tile, LANES), lambda b, hp, i: (b * nq + i, hp)),
        out_shape=jax.ShapeDtypeStruct((batch * seq, width), F32),
        scratch_shapes=[pltpu.VMEM((HEADS_PER_TILE, seq, LANES), BF16), pltpu.VMEM((HEADS_PER_TILE, seq, LANES), BF16),
                        pltpu.VMEM((HEADS_PER_TILE, LANES, LANES), F32)],
        compiler_params=pltpu.CompilerParams(dimension_semantics=("arbitrary", "arbitrary", "arbitrary")),
        name="moba_prompt",
    )(q, k, v, _alibi_table(MOBA_HEADS, n_blocks))


def _fox_gate_body(f_ref, b_ref, lf_ref, cc_ref, carry_scr, *, tiles_per_seq):
    i = pl.program_id(0)
    tm = f_ref.shape[0]

    @pl.when(i % tiles_per_seq == 0)
    def _():
        carry_scr[...] = jnp.zeros_like(carry_scr)

    lf = _log_sigmoid(f_ref[...] + b_ref[...])
    lf_ref[...] = lf
    tril = (_iota((tm, tm), 0) >= _iota((tm, tm), 1)).astype(F32)
    c = _dot(tril, lf, precision=HIGHEST) + carry_scr[...]
    carry_scr[...] = c[tm - 1:tm, :]
    cc_ref[...] = c


def fox_gates(f, bias, seq, tm):
    m = f.shape[0]
    assert seq % tm == 0 and m % seq == 0
    return pl.pallas_call(
        functools.partial(_fox_gate_body, tiles_per_seq=seq // tm),
        grid=(m // tm,),
        in_specs=[pl.BlockSpec((tm, GATE_PAD), lambda i: (i, 0)), pl.BlockSpec((1, GATE_PAD), lambda i: (0, 0))],
        out_specs=[pl.BlockSpec((tm, GATE_PAD), lambda i: (i, 0)), pl.BlockSpec((tm, GATE_PAD), lambda i: (i, 0))],
        out_shape=[jax.ShapeDtypeStruct((m, GATE_PAD), F32), jax.ShapeDtypeStruct((m, GATE_PAD), F32)],
        scratch_shapes=[pltpu.VMEM((1, GATE_PAD), F32)],
        compiler_params=pltpu.CompilerParams(dimension_semantics=("arbitrary",)),
        name="fox_gates",
    )(f, bias)


def _fox_body(q_ref, k_ref, v_ref, cq_ref, ck_ref, o_ref, ka_scr, vb_scr, *, tile):
    hp = pl.program_id(1)
    i = pl.program_id(2)
    seq = k_ref.shape[0]
    d = HEAD_DIM

    def head_column(c_all, hh):
        lane_g = _iota(c_all.shape, 1)
        return jnp.sum(jnp.where(lane_g == hp * HEADS_PER_TILE + hh, c_all, 0.0), axis=-1, keepdims=True)

    def sum_cols(idx, c_col, on_pos):
        cols = jnp.zeros(idx.shape, F32)
        for n, p in enumerate(_bf16_pieces(c_col)):
            key_side = on_pos == "key"
            cols = jnp.where(idx == n, 1.0 if key_side else p, cols)
            cols = jnp.where(idx == SPLIT_PIECES + n, -p if key_side else 1.0, cols)
        return cols

    @pl.when(i == 0)
    def _():
        k = k_ref[...]
        v = v_ref[...]
        lane = _iota((seq, LANES), 1)
        c_all = ck_ref[...]
        for hh in range(HEADS_PER_TILE):
            extras = sum_cols(lane - _extra_lane0(hh), head_column(c_all, hh), "key")
            ka_scr[hh] = jnp.where(_data_lanes(lane, hh), k, extras).astype(BF16)
            vb_scr[hh] = _values_with_ones(v, lane, hh)

    q = q_ref[...]
    lane = _iota((tile, LANES), 1)
    c_all = cq_ref[...]
    qa = []
    for hh in range(HEADS_PER_TILE):
        extras = sum_cols(lane - _extra_lane0(hh), head_column(c_all, hh), "query")
        qa.append(jnp.where(_data_lanes(lane, hh), q * (d ** -0.5), extras).astype(BF16))
    o_ref[...] = _flash_pair(qa, ka_scr, vb_scr, i, tile)


def fox_prompt(q, k, v, c_col, batch, seq):
    width = FOX_HEADS * HEAD_DIM
    tile = min(ATT_TILE, seq)
    assert seq % tile == 0 and 2 * SPLIT_PIECES <= HEAD_DIM
    nq = seq // tile
    return pl.pallas_call(
        functools.partial(_fox_body, tile=tile),
        grid=(batch, width // LANES, nq),
        in_specs=[pl.BlockSpec((tile, LANES), lambda b, hp, i: (b * nq + i, hp)),
                  pl.BlockSpec((seq, LANES), lambda b, hp, i: (b, hp)),
                  pl.BlockSpec((seq, LANES), lambda b, hp, i: (b, hp)),
                  pl.BlockSpec((tile, GATE_PAD), lambda b, hp, i: (b * nq + i, 0)),
                  pl.BlockSpec((seq, GATE_PAD), lambda b, hp, i: (b, 0))],
        out_specs=pl.BlockSpec((tile, LANES), lambda b, hp, i: (b * nq + i, hp)),
        out_shape=jax.ShapeDtypeStruct((batch * seq, width), F32),
        scratch_shapes=[pltpu.VMEM((HEADS_PER_TILE, seq, LANES), BF16), pltpu.VMEM((HEADS_PER_TILE, seq, LANES), BF16)],
        compiler_params=pltpu.CompilerParams(dimension_semantics=("arbitrary", "arbitrary", "arbitrary")),
        name="fox_prompt",
    )(q, k, v, c_col, c_col)


def _block_diag_queries(q, n_heads, n_tok):
    rows, width = n_heads * n_tok, n_heads * HEAD_DIM
    tiled = jnp.concatenate([q] * n_heads, axis=0)
    same = (_iota((rows, width), 0) // n_tok) == (_iota((rows, width), 1) // HEAD_DIM)
    return jnp.where(same, tiled, 0.0)


def _head_diag(acc, n_heads, n_tok):
    width = n_heads * HEAD_DIM
    lane_head = _iota((n_tok, width), 1) // HEAD_DIM
    out = jnp.zeros((n_tok, width), F32)
    for h in range(n_heads):
        out = jnp.where(lane_head == h, acc[h * n_tok:(h + 1) * n_tok, :], out)
    return out


def _new_token_partial(qb, kn, vn, bias, n_tok):
    rows = qb.shape[0]
    s = _dot_nt(qb, kn.astype(BF16)) + bias
    tok = _iota((rows, n_tok), 0) % n_tok
    s = jnp.where(tok >= _iota((rows, n_tok), 1), s, NEG_INF)
    m = jnp.max(s, axis=-1, keepdims=True)
    p = jnp.exp(s - m)
    l = jnp.sum(p, axis=-1, keepdims=True)
    return m, l, _dot(p.astype(BF16), vn.astype(BF16))


def _moba_dec_body(pt_ref, q_ref, kn_ref, vn_ref, sl_ref, *refs, pages_per_step, page, n_tok, past_len):
    pp = pages_per_step
    k_refs, v_refs = refs[:pp], refs[pp:2 * pp]
    o_ref = refs[2 * pp]
    qb_scr, g_scr, m_scr, l_scr, acc_scr = refs[2 * pp + 1:]
    s_id = pl.program_id(1)
    nh, d = MOBA_HEADS, HEAD_DIM
    rows = nh * n_tok
    ppb = MOBA_BLOCK // page
    n_past = past_len // MOBA_BLOCK
    scale = d ** -0.5
    slope = sl_ref[...]
    tok = (_iota((rows, 1), 0) % n_tok).astype(F32)
    lane = _iota((rows, LANES), 1)

    @pl.when(s_id == 0)
    def _():
        qb_scr[...] = (_block_diag_queries(q_ref[...], nh, n_tok) * scale).astype(BF16)
        g_scr[...] = jnp.zeros_like(g_scr)
        m_scr[...] = jnp.zeros_like(m_scr)
        l_scr[...] = jnp.zeros_like(l_scr)

    qb = qb_scr[...]
    blocks = range(pp // ppb)
    pages = range(pp)
    blk0 = s_id * (pp // ppb)
    qk = [_dot(qb, k_refs[u][...].astype(BF16)) for u in pages]
    qk_sum = [jnp.sum(qk[u], axis=-1, keepdims=True) for u in pages]
    pos0 = (blk0 * MOBA_BLOCK).astype(F32) + _iota((rows, page), 1).astype(F32)
    s = [qk[u] - slope * ((past_len + tok) - (pos0 + float(u * page))) for u in pages]
    s_max = [jnp.max(s[u], axis=-1, keepdims=True) for u in pages]
    m = [functools.reduce(jnp.maximum, s_max[g * ppb:(g + 1) * ppb]) for g in blocks]
    p = [jnp.exp(s[u] - m[u // ppb]) for u in pages]
    p_sum = [jnp.sum(p[u], axis=-1, keepdims=True) for u in pages]
    pv = [_dot_nt(p[u].astype(BF16), v_refs[u][...].astype(BF16)) for u in pages]
    g_all, m_all, l_all = g_scr[...], m_scr[...], l_scr[...]
    for g in blocks:
        mine = slice(g * ppb, (g + 1) * ppb)
        gate = _add_all(qk_sum[mine]) * (1.0 / (scale * MOBA_BLOCK))
        g_all = jnp.where(lane == blk0 + g, gate, g_all)
        m_all = jnp.where(lane == blk0 + g, m[g], m_all)
        l_all = jnp.where(lane == blk0 + g, _add_all(p_sum[mine]), l_all)
        acc_scr[blk0 + g] = _add_all(pv[mine])
    g_scr[...], m_scr[...], l_scr[...] = g_all, m_all, l_all

    @pl.when(s_id == pl.num_programs(1) - 1)
    def _():
        sel = _topk_blocks(g_scr[...], n_past, n_past, axis=1)
        m_all = m_scr[...]
        t_k = _iota((rows, n_tok), 1).astype(F32)
        m_own, l_own, acc_own = _new_token_partial(qb, kn_ref[...], vn_ref[...], -slope * (tok - t_k), n_tok)
        m_tot = jnp.maximum(m_own, jnp.max(jnp.where(sel, m_all, NEG_INF), axis=-1, keepdims=True))
        w = jnp.where(sel, jnp.exp(m_all - m_tot), 0.0)
        w_own = jnp.exp(m_own - m_tot)
        l_tot = w_own * l_own + jnp.sum(w * l_scr[...], axis=-1, keepdims=True)
        acc = w_own * acc_own
        for n in range(n_past):
            acc = acc + w[:, n:n + 1] * acc_scr[n]
        o_ref[...] = _head_diag(acc / l_tot, nh, n_tok)


def _pages_position_minor(cache):
    n_phys, page, heads, d = cache.shape
    return jnp.transpose(cache, (0, 2, 3, 1)).reshape(n_phys, heads * d, page)


def moba_decode(q, k_new, v_new, cache_k, cache_v, page_table, n_tok, pages_per_step):
    n_seq, n_pages = page_table.shape
    page = cache_k.shape[1]
    nh, d = MOBA_HEADS, HEAD_DIM
    width = nh * d
    past_len = n_pages * page
    rows = nh * n_tok
    assert MOBA_BLOCK % page == 0 and past_len % MOBA_BLOCK == 0 and n_tok <= MOBA_BLOCK
    assert pages_per_step % (MOBA_BLOCK // page) == 0 and n_pages % pages_per_step == 0
    n_past = past_len // MOBA_BLOCK
    ck = _pages_position_minor(cache_k)
    cv = _pages_position_minor(cache_v)
    slope_rows = jnp.asarray(np.repeat(
        np.array([2.0 ** (-8.0 * (i + 1) / nh) for i in range(nh)], dtype=np.float32), n_tok).reshape(rows, 1))
    tok_spec = pl.BlockSpec((n_tok, width), lambda b, s, pt: (b, 0))

    assert n_past <= LANES

    def page_spec(u):
        return pl.BlockSpec((None, width, page), lambda b, s, pt: (pt[b, s * pages_per_step + u], 0, 0))

    grid_spec = pltpu.PrefetchScalarGridSpec(
        num_scalar_prefetch=1,
        grid=(n_seq, n_pages // pages_per_step),
        in_specs=[tok_spec, tok_spec, tok_spec, pl.BlockSpec((rows, 1), lambda b, s, pt: (0, 0))]
        + [page_spec(u) for u in range(pages_per_step)] * 2,
        out_specs=tok_spec,
        scratch_shapes=[pltpu.VMEM((rows, width), BF16), pltpu.VMEM((rows, LANES), F32),
                        pltpu.VMEM((rows, LANES), F32), pltpu.VMEM((rows, LANES), F32),
                        pltpu.VMEM((n_past, rows, width), F32)],
    )
    return pl.pallas_call(
        functools.partial(_moba_dec_body, pages_per_step=pages_per_step, page=page, n_tok=n_tok, past_len=past_len),
        grid_spec=grid_spec,
        out_shape=jax.ShapeDtypeStruct((n_seq * n_tok, width), F32),
        compiler_params=pltpu.CompilerParams(dimension_semantics=("arbitrary", "arbitrary")),
        name="moba_decode",
    )(page_table, q, k_new, v_new, slope_rows, *([ck] * pages_per_step), *([cv] * pages_per_step))


def _fox_dec_body(pt_ref, q_ref, kn_ref, vn_ref, f_ref, fb_ref, *refs, pages_per_step, page, n_tok):
    pp = pages_per_step
    k_refs, v_refs, lf_refs = refs[:pp], refs[pp:2 * pp], refs[2 * pp:3 * pp]
    o_ref, lfn_ref = refs[3 * pp], refs[3 * pp + 1]
    qb_scr, carry_scr, ctot_scr, m_scr, l_scr, acc_scr = refs[3 * pp + 2:]
    s_id = pl.program_id(1)
    nh, d = FOX_HEADS, HEAD_DIM
    rows = nh * n_tok
    expand = ((_iota((rows, nh), 0) // n_tok) == _iota((rows, nh), 1)).astype(F32)
    triu = (_iota((page, page), 0) <= _iota((page, page), 1)).astype(BF16)

    @pl.when(s_id == 0)
    def _():
        qb_scr[...] = (_block_diag_queries(q_ref[...], nh, n_tok) * (d ** -0.5)).astype(BF16)
        carry_scr[...] = jnp.zeros_like(carry_scr)
        m_scr[...] = jnp.full_like(m_scr, NEG_INF)
        l_scr[...] = jnp.zeros_like(l_scr)
        acc_scr[...] = jnp.zeros_like(acc_scr)

    qb = qb_scr[...]
    pages = range(pp)
    qk = [_dot(qb, k_refs[u][...].astype(BF16)) for u in pages]
    pieces = [jnp.concatenate([p.astype(BF16) for p in _bf16_pieces(lf_refs[u][...])], axis=0) for u in pages]
    sums = [_dot(pieces[u], triu) for u in pages]
    local = [_add_all([sums[u][n * nh:(n + 1) * nh] for n in range(SPLIT_PIECES)]) for u in pages]
    c_heads, offset = [], carry_scr[...]
    for u in pages:
        c_heads.append(local[u] + offset)
        offset = c_heads[u][:, page - 1:page]
    c_rows = [jnp.broadcast_to(c_heads[u][:, None, :], (nh, n_tok, page)).reshape(rows, page) for u in pages]
    ss = [qk[u] - c_rows[u] for u in pages]
    carry_scr[...] = offset
    ctot_scr[...] = c_rows[-1][:, page - 1:page]
    m_prev = m_scr[...]
    m_new = m_prev
    for s in ss:
        m_new = jnp.maximum(m_new, jnp.max(s, axis=-1, keepdims=True))
    alpha = jnp.exp(m_prev - m_new)
    p = [jnp.exp(ss[u] - m_new) for u in pages]
    p_sum = [jnp.sum(p[u], axis=-1, keepdims=True) for u in pages]
    pv = [_dot_nt(p[u].astype(BF16), v_refs[u][...].astype(BF16)) for u in pages]
    acc_scr[...] = alpha * acc_scr[...] + _add_all(pv)
    l_scr[...] = alpha * l_scr[...] + _add_all(p_sum)
    m_scr[...] = m_new

    @pl.when(s_id == pl.num_programs(1) - 1)
    def _():
        lf_new = _log_sigmoid(f_ref[...] + fb_ref[...])
        lfn_ref[...] = lf_new
        tril_t = (_iota((n_tok, n_tok), 0) >= _iota((n_tok, n_tok), 1)).astype(F32)
        c_new = _dot(tril_t, lf_new, precision=HIGHEST)[:, :nh]
        c_new_rows = jnp.concatenate([c_new] * nh, axis=0)
        c_q = jnp.sum(expand * c_new_rows, axis=-1, keepdims=True)
        m_past = m_scr[...] + (ctot_scr[...] + c_q)
        c_k = _dot_nt(expand, c_new, precision=HIGHEST)
        m_own, l_own, acc_own = _new_token_partial(qb, kn_ref[...], vn_ref[...], c_q - c_k, n_tok)
        m_tot = jnp.maximum(m_past, m_own)
        w_past = jnp.exp(m_past - m_tot)
        w_own = jnp.exp(m_own - m_tot)
        l_tot = w_past * l_scr[...] + w_own * l_own
        acc = w_past * acc_scr[...] + w_own * acc_own
        o_ref[...] = _head_diag(acc / l_tot, nh, n_tok)


def fox_decode(q, k_new, v_new, f_new, f_bias, cache_k, cache_v, cache_lf, page_table, n_tok, pages_per_step):
    n_seq, n_pages = page_table.shape
    page = cache_k.shape[1]
    nh, d = FOX_HEADS, HEAD_DIM
    width = nh * d
    rows = nh * n_tok
    assert n_pages % pages_per_step == 0
    assert n_tok % SUBLANES == 0
    ck = _pages_position_minor(cache_k)
    cv = _pages_position_minor(cache_v)
    clf =jnp.transpose(cache_lf, (0, 2, 1))
    tok_spec = pl.BlockSpec((n_tok, width), lambda b, s, pt: (b, 0))
    gate_spec = pl.BlockSpec((n_tok, GATE_PAD), lambda b, s, pt: (b, 0))

    def page_spec(u, dims):
        return pl.BlockSpec((None,) + dims, lambda b, s, pt: (pt[b, s * pages_per_step + u],) + (0,) * len(dims))

    grid_spec = pltpu.PrefetchScalarGridSpec(
        num_scalar_prefetch=1,
        grid=(n_seq, n_pages // pages_per_step),
        in_specs=[tok_spec, tok_spec, tok_spec, gate_spec, pl.BlockSpec((1, GATE_PAD), lambda b, s, pt: (0, 0))]
        + [page_spec(u, (width, page)) for u in range(pages_per_step)] * 2
        + [page_spec(u, (nh, page)) for u in range(pages_per_step)],
        out_specs=[tok_spec, gate_spec],
        scratch_shapes=[pltpu.VMEM((rows, width), BF16), pltpu.VMEM((nh, 1), F32), pltpu.VMEM((rows, 1), F32),
                        pltpu.VMEM((rows, 1), F32), pltpu.VMEM((rows, 1), F32), pltpu.VMEM((rows, width), F32)],
    )
    return pl.pallas_call(
        functools.partial(_fox_dec_body, pages_per_step=pages_per_step, page=page, n_tok=n_tok),
        grid_spec=grid_spec,
        out_shape=[jax.ShapeDtypeStruct((n_seq * n_tok, width), F32),
                   jax.ShapeDtypeStruct((n_seq * n_tok, GATE_PAD), F32)],
        compiler_params=pltpu.CompilerParams(dimension_semantics=("arbitrary", "arbitrary")),
        name="fox_decode",
    )(page_table, q, k_new, v_new, f_new, f_bias,
      *([ck] * pages_per_step), *([cv] * pages_per_step), *([clf] * pages_per_step))


def _pad_lanes(w, width):
    return jnp.pad(w, ((0, 0), (0, width - w.shape[1])))


def _even_pieces(w_in):
    nq = MLSTM_HEADS * MLSTM_DQK
    nv = MLSTM_HEADS * MLSTM_DV
    nm = MOBA_HEADS * HEAD_DIM
    sizes = (nq, nq, nv, nv, 2 * MLSTM_HEADS, nm, nm, nm)
    pieces, start = [], 0
    for s in sizes:
        pieces.append(w_in[:, start:start + s])
        start += s
    assert start == w_in.shape[1]
    pieces[4] = _pad_lanes(pieces[4], GATE_PAD)
    return [p.astype(BF16) for p in pieces]


def _odd_pieces(w_in):
    nw = FOX_HEADS * HEAD_DIM
    pieces = [w_in[:, 0:nw], w_in[:, nw:2 * nw], w_in[:, 2 * nw:3 * nw], _pad_lanes(w_in[:, 3 * nw:], GATE_PAD)]
    assert w_in.shape[1] == 3 * nw + FOX_HEADS
    return [p.astype(BF16) for p in pieces]


def _token_major_view(x_t, heads):
    batch, _, seq = x_t.shape
    return jnp.transpose(x_t.reshape(batch, heads, HEAD_DIM, seq), (0, 3, 1, 2))


PROMPT_TM = 512
FFN_TM = 512
MLSTM_PROMPT_BATCH = 4
MLSTM_SAMPLE_BATCH = 8
FOX_GATE_TM = 512
MOBA_PAGES_PER_STEP = 16
FOX_PAGES_PER_STEP = 8


def kernel(x_prompt, x_sample, cache_moba_k, cache_moba_v, state_mlstm_C, state_mlstm_n, state_mlstm_m, cache_fox_k, cache_fox_v, cache_fox_logf, page_table, norm_mix_g, norm_ffn_g, norm_final_g, even_w_in, even_b_ig, even_b_fg, even_head_norm_g, even_w_out, odd_w_in, odd_b_f, odd_w_out, ffn_w_gate, ffn_w_up, ffn_w_down):
    bp, tp, d = x_prompt.shape
    bs, ts, _ = x_sample.shape
    depth = norm_mix_g.shape[0]
    mp, ms = bp * tp, bs * ts
    hp = x_prompt.reshape(mp, d)
    hs = x_sample.reshape(ms, d)
    tm_p = min(PROMPT_TM, mp)
    tm_s = ms
    ffn_tm_p = min(FFN_TM, mp)
    nv = MLSTM_HEADS * MLSTM_DV
    n_pages = page_table.shape[1]
    pe, se, po, so = [], [], [], []
    for layer in range(depth):
        li = layer // 2
        final = layer == depth - 1
        if layer % 2 == 0:
            ws = _even_pieces(even_w_in[li])
            gate_bias = _pad_lanes(jnp.concatenate([even_b_ig[li], even_b_fg[li]]).reshape(1, -1), GATE_PAD)
            w_out = even_w_out[li].astype(BF16)
            w_out_a, w_out_b = w_out[:nv], w_out[nv:]
            zeros = lambda *s: jnp.zeros(s, F32)
            q1, k1, v1, o1, gt, q2, k2, v2, k2_t, v2_t = norm_proj(
                hp, norm_mix_g[layer], ws, tm_p, ws_t=(ws[6].T, ws[7].T), seq=tp)
            y1, c_p, n_p, m_p = mlstm(q1, k1, v1, o1, gt, gate_bias, even_head_norm_g[li],
                                      zeros(bp, MLSTM_HEADS, MLSTM_DQK, MLSTM_DV), zeros(bp, MLSTM_HEADS, MLSTM_DQK),
                                      zeros(bp, MLSTM_HEADS), bp, tp, math.gcd(bp, MLSTM_PROMPT_BATCH))
            y2 = moba_prompt(q2, k2, v2, bp, tp)
            hp = proj_res(hp, [y1, y2], [w_out_a, w_out_b], tm_p)
            pe.append((_token_major_view(k2_t, MOBA_HEADS), _token_major_view(v2_t, MOBA_HEADS), c_p, n_p, m_p))
            q1, k1, v1, o1, gt, q2, k2, v2 = norm_proj(hs, norm_mix_g[layer], ws, tm_s)
            y1, c_s, n_s, m_s = mlstm(q1, k1, v1, o1, gt, gate_bias, even_head_norm_g[li],
                                      state_mlstm_C[li], state_mlstm_n[li], state_mlstm_m[li], bs, ts,
                                      math.gcd(bs, MLSTM_SAMPLE_BATCH))
            y2 = moba_decode(q2, k2, v2, cache_moba_k[li], cache_moba_v[li], page_table, ts,
                             min(MOBA_PAGES_PER_STEP, n_pages))
            hs = proj_res(hs, [y1, y2], [w_out_a, w_out_b], tm_s)
            se.append((k2.reshape(bs, ts, MOBA_HEADS, HEAD_DIM), v2.reshape(bs, ts, MOBA_HEADS, HEAD_DIM), c_s, n_s, m_s))
        else:
            ws = _odd_pieces(odd_w_in[li])
            f_bias = _pad_lanes(odd_b_f[li].reshape(1, -1), GATE_PAD)
            w_out = odd_w_out[li].astype(BF16)
            q, k, v, f, k_t, v_t = norm_proj(hp, norm_mix_g[layer], ws, tm_p, ws_t=(ws[1].T, ws[2].T), seq=tp)
            lf, c_col = fox_gates(f, f_bias, tp, min(FOX_GATE_TM, tp))
            y = fox_prompt(q, k, v, c_col, bp, tp)
            hp = proj_res(hp, [y], [w_out], tm_p)
            po.append((_token_major_view(k_t, FOX_HEADS), _token_major_view(v_t, FOX_HEADS),
                       lf[:, :FOX_HEADS].reshape(bp, tp, FOX_HEADS)))
            q, k, v, f = norm_proj(hs, norm_mix_g[layer], ws, tm_s)
            y, lf = fox_decode(q, k, v, f, f_bias, cache_fox_k[li], cache_fox_v[li], cache_fox_logf[li],
                               page_table, ts, min(FOX_PAGES_PER_STEP, n_pages))
            hs = proj_res(hs, [y], [w_out], tm_s)
            so.append((k.reshape(bs, ts, FOX_HEADS, HEAD_DIM), v.reshape(bs, ts, FOX_HEADS, HEAD_DIM),
                       lf[:, :FOX_HEADS].reshape(bs, ts, FOX_HEADS)))
        wg, wu, wd = ffn_w_gate[layer].astype(BF16), ffn_w_up[layer].astype(BF16), ffn_w_down[layer].astype(BF16)
        hp = ffn(hp, norm_ffn_g[layer], norm_final_g, wg, wu, wd, ffn_tm_p, final)
        hs = ffn(hs, norm_ffn_g[layer], norm_final_g, wg, wu, wd, tm_s, final)
    stack = lambda group, idx: jnp.stack([s[idx] for s in group])
    return (hp.reshape(bp, tp, d), hs.reshape(bs, ts, d),
            stack(pe, 0), stack(pe, 1), stack(pe, 2), stack(pe, 3), stack(pe, 4),
            stack(po, 0), stack(po, 1), stack(po, 2),
            stack(se, 0), stack(se, 1), stack(se, 2), stack(se, 3), stack(se, 4),
            stack(so, 0), stack(so, 1), stack(so, 2))
```

```python
import functools
import math

import jax
import jax.numpy as jnp
import ml_dtypes
import numpy as np
from jax import lax
from jax.experimental import pallas as pl
from jax.experimental.pallas import tpu as pltpu

F32 = jnp.float32
BF16 = jnp.bfloat16
HIGHEST = lax.Precision.HIGHEST

LANES = 128
SUBLANES = 8

HEAD_DIM = 64
MLSTM_HEADS = 4
MLSTM_DQK = 64
MLSTM_DV = 128
MLSTM_CHUNK = 64
MOBA_HEADS = 8
MOBA_BLOCK = 256
MOBA_TOPK = 3
FOX_HEADS = 16
RMS_EPS = 1e-6
GATE_PAD = LANES
HEADS_PER_TILE = LANES // HEAD_DIM

NEG_INF = float("-inf")
MASK_BIAS = -1e30
SPLIT_PIECES = 3


def _log_sigmoid(x):
    return -(jnp.maximum(-x, 0.0) + jnp.log1p(jnp.exp(-jnp.abs(x))))


def _rmsnorm(x, g):
    return x * lax.rsqrt(jnp.mean(x * x, axis=-1, keepdims=True) + RMS_EPS) * g


def _dot(a, b, precision=None):
    return jnp.dot(a, b, precision=precision, preferred_element_type=F32)


def _dot_nt(a, b, precision=None):
    return lax.dot_general(a, b, (((1,), (1,)), ((), ())), precision=precision, preferred_element_type=F32)


def _iota(shape, dim):
    return lax.broadcasted_iota(jnp.int32, shape, dim)


def _add_all(xs):
    return functools.reduce(lambda a, b: a + b, xs)


def _bf16_pieces(x):
    pieces, rest = [], x
    for _ in range(SPLIT_PIECES):
        p = rest.astype(BF16).astype(F32)
        pieces.append(p)
        rest = rest - p
    return pieces


def _norm_proj_body(x_ref, g_ref, *refs, n_w, n_t):
    w_refs, wt_refs = refs[:n_w], refs[n_w:n_w + n_t]
    o_refs, ot_refs = refs[n_w + n_t:2 * n_w + n_t], refs[2 * n_w + n_t:]
    xb = _rmsnorm(x_ref[...], g_ref[...]).astype(BF16)
    for w_ref, o_ref in zip(w_refs, o_refs):
        o_ref[...] = _dot(xb, w_ref[...])
    for wt_ref, ot_ref in zip(wt_refs, ot_refs):
        ot_ref[...] = _dot_nt(wt_ref[...], xb)


def norm_proj(x, g, ws, tm, ws_t=(), seq=None):
    m, d = x.shape
    assert m % tm == 0
    in_specs = [pl.BlockSpec((tm, d), lambda i: (i, 0)), pl.BlockSpec((1, d), lambda i: (0, 0))]
    in_specs += [pl.BlockSpec(w.shape, lambda i: (0, 0)) for w in tuple(ws) + tuple(ws_t)]
    out_specs = [pl.BlockSpec((tm, w.shape[1]), lambda i: (i, 0)) for w in ws]
    out_shape = [jax.ShapeDtypeStruct((m, w.shape[1]), F32) for w in ws]
    if ws_t:
        assert seq % tm == 0 and m % seq == 0
        per_seq = seq // tm
        out_specs += [pl.BlockSpec((None, w.shape[0], tm), lambda i: (i // per_seq, 0, i % per_seq)) for w in ws_t]
        out_shape += [jax.ShapeDtypeStruct((m // seq, w.shape[0], seq), F32) for w in ws_t]
    return pl.pallas_call(
        functools.partial(_norm_proj_body, n_w=len(ws), n_t=len(ws_t)),
        grid=(m // tm,), in_specs=in_specs, out_specs=out_specs, out_shape=out_shape,
        name="norm_proj",
    )(x, g.reshape(1, d), *ws, *ws_t)


def _mix_ffn_body(h_ref, *refs, n_a, final_norm):
    a_refs, wo_refs = refs[:n_a], refs[n_a:2 * n_a]
    g_ref, gf_ref, wg_ref, wu_ref, wd_ref, o_ref = refs[2 * n_a:]
    x = h_ref[...]
    for a_ref, wo_ref in zip(a_refs, wo_refs):
        x = x + _dot(a_ref[...].astype(BF16), wo_ref[...])
    xn = _rmsnorm(x, g_ref[...]).astype(BF16)
    gate = _dot(xn, wg_ref[...])
    up = _dot(xn, wu_ref[...])
    act = (gate * jax.nn.sigmoid(gate) * up).astype(BF16)
    y = x + _dot(act, wd_ref[...])
    o_ref[...] = _rmsnorm(y, gf_ref[...]) if final_norm else y


def mix_ffn(h, acts, ws_out, g, g_final, wg, wu, wd, tm, final_norm):
    m, d = h.shape
    assert m % tm == 0
    resident = lambda w: pl.BlockSpec(w.shape, lambda i: (0, 0), pipeline_mode=pl.Buffered(1))
    in_specs = [pl.BlockSpec((tm, d), lambda i: (i, 0))]
    in_specs += [pl.BlockSpec((tm, a.shape[1]), lambda i: (i, 0)) for a in acts]
    in_specs += [resident(w) for w in ws_out]
    in_specs += [pl.BlockSpec((1, d), lambda i: (0, 0)), pl.BlockSpec((1, d), lambda i: (0, 0)),
                 resident(wg), resident(wu), resident(wd)]
    return pl.pallas_call(
        functools.partial(_mix_ffn_body, n_a=len(acts), final_norm=final_norm),
        grid=(m // tm,), in_specs=in_specs,
        out_specs=pl.BlockSpec((tm, d), lambda i: (i, 0)),
        out_shape=jax.ShapeDtypeStruct((m, d), F32),
        name="mix_ffn",
    )(h, *acts, *ws_out, g.reshape(1, d), g_final.reshape(1, d), wg, wu, wd)


def _mlstm_body(q_ref, k_ref, v_ref, o_ref, g_ref, gb_ref, gh_ref, c0_ref, n0_ref, m0_ref,
                y_ref, c_ref, n_ref, m_ref, c_scr, n_scr, m_scr, *, chunk, n_batch):
    j = pl.program_id(1)
    nh, dk, dv = MLSTM_HEADS, MLSTM_DQK, MLSTM_DV

    @pl.when(j == 0)
    def _():
        c_scr[...] = c0_ref[...]
        n_scr[...] = n0_ref[...]
        m_scr[...] = m0_ref[...]

    lane = _iota((chunk, GATE_PAD), 1)
    r_i = _iota((chunk, chunk), 0)
    c_i = _iota((chunk, chunk), 1)
    causal = r_i >= c_i
    eye = r_i == c_i
    tril = causal.astype(F32)
    eye_b = (_iota((dk, dk), 0) == _iota((dk, dk), 1)).astype(BF16)

    def to_row(col):
        return jnp.sum(jnp.where(eye, col, 0.0), axis=0, keepdims=True)

    chains = [(b, h) for b in range(n_batch) for h in range(nh)]
    each = lambda fn: [fn(n, b, h) for n, (b, h) in enumerate(chains)]

    acts, cums = [], []
    for b in range(n_batch):
        gpre = g_ref[b] + gb_ref[...]
        acts.append(jnp.where(lane < nh, gpre, _log_sigmoid(gpre)))
        cums.append(_dot(tril, acts[b], precision=HIGHEST))
    q = each(lambda n, b, h: q_ref[b, :, h * dk:(h + 1) * dk] * (dk ** -0.5))
    k = each(lambda n, b, h: k_ref[b, :, h * dk:(h + 1) * dk])
    vb = each(lambda n, b, h: v_ref[b, :, h * dv:(h + 1) * dv].astype(BF16))
    qb = each(lambda n, b, h: q[n].astype(BF16))
    ig_c = each(lambda n, b, h: acts[b][:, h:h + 1])
    b_c = each(lambda n, b, h: cums[b][:, nh + h:nh + h + 1])
    m_prev = each(lambda n, b, h: m_scr[b, h])
    c_prev = each(lambda n, b, h: c_scr[b, h])
    n_prev = each(lambda n, b, h: n_scr[b, h])
    dmat = each(lambda n, b, h: jnp.where(causal, b_c[n] - to_row(b_c[n]) + to_row(ig_c[n]), NEG_INF))
    inter = each(lambda n, b, h: b_c[n] + m_prev[n])
    mt = each(lambda n, b, h: jnp.maximum(inter[n], jnp.max(dmat[n], axis=-1, keepdims=True)))
    w_inter = each(lambda n, b, h: jnp.exp(inter[n] - mt[n]))
    qk = each(lambda n, b, h: _dot_nt(qb[n], k[n].astype(BF16)) * jnp.exp(dmat[n] - mt[n]))
    num = each(lambda n, b, h: w_inter[n] * _dot(qb[n], c_prev[n].astype(BF16)) + _dot(qk[n].astype(BF16), vb[n]))
    den = each(lambda n, b, h: w_inter[n] * jnp.sum(q[n] * n_prev[n], axis=-1, keepdims=True)
               + jnp.sum(qk[n], axis=-1, keepdims=True))
    hh = each(lambda n, b, h: num[n] / jnp.maximum(jnp.abs(den[n]), jnp.exp(-mt[n])))
    m_new = each(lambda n, b, h: mt[n][chunk - 1:chunk, :])
    b_last = each(lambda n, b, h: b_c[n][chunk - 1:chunk, :])
    decay = each(lambda n, b, h: jnp.exp(b_last[n] + m_prev[n] - m_new[n]))
    kw = each(lambda n, b, h: k[n] * jnp.exp(b_last[n] - b_c[n] + ig_c[n] - m_new[n]))
    kw_t = each(lambda n, b, h: _dot_nt(eye_b, kw[n].astype(BF16)).astype(BF16))
    for n, (b, h) in enumerate(chains):
        c_scr[b, h] = decay[n] * c_prev[n] + _dot(kw_t[n], vb[n])
        n_scr[b, h] = decay[n] * n_prev[n] + jnp.sum(kw[n], axis=0, keepdims=True)
        m_scr[b, h] = m_new[n]
    for n, (b, h) in enumerate(chains):
        hn = hh[n] * lax.rsqrt(jnp.mean(hh[n] * hh[n], axis=-1, keepdims=True) + RMS_EPS)
        hn = hn * gh_ref[:, h * dv:(h + 1) * dv]
        y_ref[b, :, h * dv:(h + 1) * dv] = jax.nn.sigmoid(o_ref[b, :, h * dv:(h + 1) * dv]) * hn

    @pl.when(j == pl.num_programs(1) - 1)
    def _():
        c_ref[...] = c_scr[...]
        n_ref[...] = n_scr[...]
        m_ref[...] = m_scr[...]


def mlstm(q, k, v, o, gates, gate_bias, g_head, c0, n0, m0, batch, seq, n_batch):
    nh, dk, dv = MLSTM_HEADS, MLSTM_DQK, MLSTM_DV
    chunk = MLSTM_CHUNK if seq % MLSTM_CHUNK == 0 else seq
    assert chunk % SUBLANES == 0 and batch % n_batch == 0
    steps = seq // chunk
    row_spec = lambda w: pl.BlockSpec((n_batch, chunk, w), lambda b, j: (b, j, 0))
    st4 = lambda s: pl.BlockSpec((n_batch,) + s, lambda b, j: (b, 0, 0, 0))
    rows3 = lambda x: x.reshape(batch, seq, x.shape[-1])
    y, c, n, m = pl.pallas_call(
        functools.partial(_mlstm_body, chunk=chunk, n_batch=n_batch),
        grid=(batch // n_batch, steps),
        in_specs=[row_spec(nh * dk), row_spec(nh * dk), row_spec(nh * dv), row_spec(nh * dv), row_spec(GATE_PAD),
                  pl.BlockSpec((1, GATE_PAD), lambda b, j: (0, 0)),
                  pl.BlockSpec((1, nh * dv), lambda b, j: (0, 0)),
                  st4((nh, dk, dv)), st4((nh, 1, dk)), st4((nh, 1, 1))],
        out_specs=[row_spec(nh * dv), st4((nh, dk, dv)), st4((nh, 1, dk)), st4((nh, 1, 1))],
        out_shape=[jax.ShapeDtypeStruct((batch, seq, nh * dv), F32),
                   jax.ShapeDtypeStruct((batch, nh, dk, dv), F32),
                   jax.ShapeDtypeStruct((batch, nh, 1, dk), F32),
                   jax.ShapeDtypeStruct((batch, nh, 1, 1), F32)],
        scratch_shapes=[pltpu.VMEM((n_batch, nh, dk, dv), F32), pltpu.VMEM((n_batch, nh, 1, dk), F32),
                        pltpu.VMEM((n_batch, nh, 1, 1), F32)],
        compiler_params=pltpu.CompilerParams(dimension_semantics=("arbitrary", "arbitrary")),
        name="mlstm",
    )(rows3(q), rows3(k), rows3(v), rows3(o), rows3(gates), gate_bias, g_head.reshape(1, nh * dv),
      c0, n0.reshape(batch, nh, 1, dk), m0.reshape(batch, nh, 1, 1))
    return y.reshape(batch * seq, nh * dv), c, n.reshape(batch, nh, dk), m.reshape(batch, nh)


def _topk_blocks(gate, n_blocks, n_valid, axis):
    idx = _iota(gate.shape, axis)
    rank = jnp.zeros(gate.shape, F32)
    for c in range(n_blocks):
        other = gate[c:c + 1, :] if axis == 0 else gate[:, c:c + 1]
        beats = (other > gate) | ((other == gate) & (c < idx))
        candidate = jnp.where(c < n_valid, 1.0, 0.0)
        rank = rank + jnp.where(beats, candidate, 0.0)
    return (rank < MOBA_TOPK) & (idx < n_valid)


ATT_TILE = 512


def _data_lanes(lane, hh):
    return (lane >= hh * HEAD_DIM) & (lane < (hh + 1) * HEAD_DIM)


def _extra_lane0(hh):
    return (HEADS_PER_TILE - 1 - hh) * HEAD_DIM


def _values_with_ones(v, lane, hh):
    return jnp.where(_data_lanes(lane, hh), v, jnp.where(lane == _extra_lane0(hh), 1.0, 0.0)).astype(BF16)


def _flash_pair(qa, ka_scr, va_scr, i, tile):
    heads = range(HEADS_PER_TILE)
    causal = _iota((tile, tile), 0) >= _iota((tile, tile), 1)

    def step(key_tiles, carry, first_is_diagonal=False):
        starts = [pl.multiple_of(j * tile, tile) for j in key_tiles]
        chains = [(hh, t) for t in range(len(starts)) for hh in heads]
        s = {c: _dot_nt(qa[c[0]], ka_scr[c[0], pl.ds(starts[c[1]], tile), :]) for c in chains}
        if first_is_diagonal:
            s.update({(hh, 0): jnp.where(causal, s[hh, 0], NEG_INF) for hh in heads})
        s_max = {c: jnp.max(s[c], axis=-1, keepdims=True) for c in chains}
        m = {(hh, -1): carry[hh][0] for hh in heads}
        for hh, t in chains:
            m[hh, t] = jnp.maximum(m[hh, t - 1], s_max[hh, t])
        p = {c: jnp.exp(s[c] - m[c]).astype(BF16) for c in chains}
        pv = {c: _dot(p[c], va_scr[c[0], pl.ds(starts[c[1]], tile), :]) for c in chains}
        acc = [carry[hh][1] for hh in heads]
        for hh, t in chains:
            acc[hh] = jnp.exp(m[hh, t - 1] - m[hh, t]) * acc[hh] + pv[hh, t]
        return tuple((m[hh, len(starts) - 1], acc[hh]) for hh in heads)

    carry = tuple((jnp.full((tile, 1), NEG_INF, F32), jnp.zeros((tile, LANES), F32)) for _ in heads)
    n_group = lax.shift_right_logical(i, 2)
    done = n_group * 4
    carry = lax.cond((i & 1) == 1, lambda c: step([i, i - 1], c, True), lambda c: step([i], c, True), carry)
    carry = lax.fori_loop(0, n_group, lambda j, c: step([4 * j + t for t in range(4)], c), carry)
    carry = lax.cond(i - done >= 2, lambda c: step([done, done + 1], c), lambda c: c, carry)
    lane = _iota((tile, LANES), 1)
    out = jnp.zeros((tile, LANES), F32)
    for hh in heads:
        acc = carry[hh][1]
        e0 = _extra_lane0(hh)
        out = jnp.where(_data_lanes(lane, hh), acc / acc[:, e0:e0 + 1], out)
    return out


def _floor_div_pow2(x, n):
    assert n & (n - 1) == 0
    return lax.shift_right_logical(x, n.bit_length() - 1)


def _alibi_digits(pos):
    assert HEAD_DIM & (HEAD_DIM - 1) == 0
    lo = pos & (HEAD_DIM - 1)
    return (pos - lo).astype(F32), lo.astype(F32)


ALIBI_ROWS = 8


def _alibi_table(n_heads, n_blocks):
    rest = np.array([2.0 ** (-8.0 * (i + 1) / n_heads) for i in range(n_heads)], dtype=np.float32)
    table = np.zeros((n_heads, ALIBI_ROWS, LANES), np.float32)
    for n in range(SPLIT_PIECES):
        piece = rest.astype(ml_dtypes.bfloat16).astype(np.float32)
        rest = rest - piece
        for h in range(n_heads):
            base = _extra_lane0(h % HEADS_PER_TILE) + n_blocks + 4 * n
            table[h, 0, base + 2:base + 4] = piece[h]
            table[h, 1, base] = 1.0
            table[h, 2, base + 1] = 1.0
            table[h, 3, base:base + 2] = piece[h]
            table[h, 4, base + 2] = 1.0
            table[h, 5, base + 3] = 1.0
    return jnp.asarray(table)


def _moba_body(q_ref, k_ref, v_ref, al_ref, o_ref, ka_scr, vb_scr, km_scr, *, n_blocks, tile):
    i = pl.program_id(2)
    seq = k_ref.shape[0]
    d = HEAD_DIM

    def alibi_cols(hh, side, pos_hi, pos_lo):
        rows = al_ref[hh]
        if side == "query":
            return rows[0:1] - pos_hi * rows[1:2] - pos_lo * rows[2:3]
        return rows[3:4] + pos_hi * rows[4:5] + pos_lo * rows[5:6]

    @pl.when(i == 0)
    def _():
        k = k_ref[...]
        v = v_ref[...]
        lane = _iota((seq, LANES), 1)
        pos = _iota((seq, LANES), 0)
        pos_hi, pos_lo = _alibi_digits(pos)
        km_scr[...] = jnp.zeros_like(km_scr)
        for n in range(n_blocks):
            km_scr[n:n + 1, :] = jnp.mean(k[n * MOBA_BLOCK:(n + 1) * MOBA_BLOCK], axis=0, keepdims=True)
        for hh in range(HEADS_PER_TILE):
            idx = lane - _extra_lane0(hh)
            extras = jnp.where(idx == _floor_div_pow2(pos, MOBA_BLOCK), 1.0, alibi_cols(hh, "key", pos_hi, pos_lo))
            ka_scr[hh] = jnp.where(_data_lanes(lane, hh), k, extras).astype(BF16)
            vb_scr[hh] = _values_with_ones(v, lane, hh)

    q = q_ref[...]
    lane = _iota((tile, LANES), 1)
    t_pos = i * tile + _iota((tile, LANES), 0)
    t_hi, t_lo = _alibi_digits(t_pos)
    blk_rows = -(-n_blocks // SUBLANES) * SUBLANES
    own_t = _floor_div_pow2(i * tile + _iota((1, tile), 1), MOBA_BLOCK)
    qa = []
    for hh in range(HEADS_PER_TILE):
        e0 = _extra_lane0(hh)
        idx = lane - e0
        data = _data_lanes(lane, hh)
        gate_t = _dot_nt(km_scr[...], jnp.where(data, q, 0.0), precision=HIGHEST)
        keep_t = _topk_blocks(gate_t, n_blocks, own_t, axis=0) | (_iota((blk_rows, tile), 0) == own_t)
        above, below = e0, LANES - e0 - blk_rows
        bias_t = ([jnp.zeros((above, tile), F32)] if above else []) + [jnp.where(keep_t, 0.0, MASK_BIAS)]
        bias_t += [jnp.zeros((below, tile), F32)] if below else []
        block_bias = jnp.concatenate(bias_t, axis=0).T
        extras = jnp.where((idx >= 0) & (idx < n_blocks), block_bias, alibi_cols(hh, "query", t_hi, t_lo))
        qa.append(jnp.where(data, q * (d ** -0.5), extras).astype(BF16))
    o_ref[...] = _flash_pair(qa, ka_scr, vb_scr, i, tile)


def moba_prompt(q, k, v, batch, seq):
    width = MOBA_HEADS * HEAD_DIM
    tile = min(ATT_TILE, seq)
    assert seq % tile == 0 and tile % MOBA_BLOCK == 0
    n_blocks = seq // MOBA_BLOCK
    assert n_blocks + 4 * SPLIT_PIECES <= HEAD_DIM
    nq = seq // tile
    return pl.pallas_call(
        functools.partial(_moba_body, n_blocks=n_blocks, tile=tile),
        grid=(batch, width // LANES, nq),
        in_specs=[pl.BlockSpec((tile, LANES), lambda b, hp, i: (b * nq + i, hp)),
                  pl.BlockSpec((seq, LANES), lambda b, hp, i: (b, hp)),
                  pl.BlockSpec((seq, LANES), lambda b, hp, i: (b, hp)),
                  pl.BlockSpec((HEADS_PER_TILE, ALIBI_ROWS, LANES), lambda b, hp, i: (hp, 0, 0))],
        out_specs=pl.BlockSpec((tile, LANES), lambda b, hp, i: (b * nq + i, hp)),
        out_shape=jax.ShapeDtypeStruct((batch * seq, width), F32),
        scratch_shapes=[pltpu.VMEM((HEADS_PER_TILE, seq, LANES), BF16), pltpu.VMEM((HEADS_PER_TILE, seq, LANES), BF16),
                        pltpu.VMEM((-(-n_blocks // SUBLANES) * SUBLANES, LANES), F32)],
        compiler_params=pltpu.CompilerParams(dimension_semantics=("arbitrary", "arbitrary", "arbitrary")),
        name="moba_prompt",
    )(q, k, v, _alibi_table(MOBA_HEADS, n_blocks))


def _fox_gate_body(f_ref, b_ref, lf_ref, cc_ref, carry_scr, *, tiles_per_seq):
    i = pl.program_id(0)
    tm = f_ref.shape[0]

    @pl.when(i % tiles_per_seq == 0)
    def _():
        carry_scr[...] = jnp.zeros_like(carry_scr)

    lf = _log_sigmoid(f_ref[...] + b_ref[...])
    lf_ref[...] = lf
    tril = (_iota((tm, tm), 0) >= _iota((tm, tm), 1)).astype(F32)
    c = _dot(tril, lf, precision=HIGHEST) + carry_scr[...]
    carry_scr[...] = c[tm - 1:tm, :]
    cc_ref[...] = c


def fox_gates(f, bias, seq, tm):
    m = f.shape[0]
    assert seq % tm == 0 and m % seq == 0
    return pl.pallas_call(
        functools.partial(_fox_gate_body, tiles_per_seq=seq // tm),
        grid=(m // tm,),
        in_specs=[pl.BlockSpec((tm, GATE_PAD), lambda i: (i, 0)), pl.BlockSpec((1, GATE_PAD), lambda i: (0, 0))],
        out_specs=[pl.BlockSpec((tm, GATE_PAD), lambda i: (i, 0)), pl.BlockSpec((tm, GATE_PAD), lambda i: (i, 0))],
        out_shape=[jax.ShapeDtypeStruct((m, GATE_PAD), F32), jax.ShapeDtypeStruct((m, GATE_PAD), F32)],
        scratch_shapes=[pltpu.VMEM((1, GATE_PAD), F32)],
        compiler_params=pltpu.CompilerParams(dimension_semantics=("arbitrary",)),
        name="fox_gates",
    )(f, bias)


def _fox_body(q_ref, k_ref, v_ref, cq_ref, ck_ref, o_ref, ka_scr, vb_scr, *, tile):
    hp = pl.program_id(1)
    i = pl.program_id(2)
    seq = k_ref.shape[0]
    d = HEAD_DIM

    def head_column(c_all, hh):
        lane_g = _iota(c_all.shape, 1)
        return jnp.sum(jnp.where(lane_g == hp * HEADS_PER_TILE + hh, c_all, 0.0), axis=-1, keepdims=True)

    def sum_cols(idx, c_col, on_pos):
        cols = jnp.zeros(idx.shape, F32)
        for n, p in enumerate(_bf16_pieces(c_col)):
            key_side = on_pos == "key"
            cols = jnp.where(idx == n, 1.0 if key_side else p, cols)
            cols = jnp.where(idx == SPLIT_PIECES + n, -p if key_side else 1.0, cols)
        return cols

    @pl.when(i == 0)
    def _():
        k = k_ref[...]
        v = v_ref[...]
        lane = _iota((seq, LANES), 1)
        c_all = ck_ref[...]
        for hh in range(HEADS_PER_TILE):
            extras = sum_cols(lane - _extra_lane0(hh), head_column(c_all, hh), "key")
            ka_scr[hh] = jnp.where(_data_lanes(lane, hh), k, extras).astype(BF16)
            vb_scr[hh] = _values_with_ones(v, lane, hh)

    q = q_ref[...]
    lane = _iota((tile, LANES), 1)
    c_all = cq_ref[...]
    qa = []
    for hh in range(HEADS_PER_TILE):
        extras = sum_cols(lane - _extra_lane0(hh), head_column(c_all, hh), "query")
        qa.append(jnp.where(_data_lanes(lane, hh), q * (d ** -0.5), extras).astype(BF16))
    o_ref[...] = _flash_pair(qa, ka_scr, vb_scr, i, tile)


def fox_prompt(q, k, v, c_col, batch, seq):
    width = FOX_HEADS * HEAD_DIM
    tile = min(ATT_TILE, seq)
    assert seq % tile == 0 and 2 * SPLIT_PIECES <= HEAD_DIM
    nq = seq // tile
    return pl.pallas_call(
        functools.partial(_fox_body, tile=tile),
        grid=(batch, width // LANES, nq),
        in_specs=[pl.BlockSpec((tile, LANES), lambda b, hp, i: (b * nq + i, hp)),
                  pl.BlockSpec((seq, LANES), lambda b, hp, i: (b, hp)),
                  pl.BlockSpec((seq, LANES), lambda b, hp, i: (b, hp)),
                  pl.BlockSpec((tile, GATE_PAD), lambda b, hp, i: (b * nq + i, 0)),
                  pl.BlockSpec((seq, GATE_PAD), lambda b, hp, i: (b, 0))],
        out_specs=pl.BlockSpec((tile, LANES), lambda b, hp, i: (b * nq + i, hp)),
        out_shape=jax.ShapeDtypeStruct((batch * seq, width), F32),
        scratch_shapes=[pltpu.VMEM((HEADS_PER_TILE, seq, LANES), BF16), pltpu.VMEM((HEADS_PER_TILE, seq, LANES), BF16)],
        compiler_params=pltpu.CompilerParams(dimension_semantics=("arbitrary", "arbitrary", "arbitrary")),
        name="fox_prompt",
    )(q, k, v, c_col, c_col)


def _block_diag_queries(q, n_heads, n_tok):
    rows, width = n_heads * n_tok, n_heads * HEAD_DIM
    tiled = jnp.concatenate([q] * n_heads, axis=0)
    same = (_iota((rows, width), 0) // n_tok) == (_iota((rows, width), 1) // HEAD_DIM)
    return jnp.where(same, tiled, 0.0)


def _head_diag(acc, n_heads, n_tok):
    width = n_heads * HEAD_DIM
    lane_head = _iota((n_tok, width), 1) // HEAD_DIM
    out = jnp.zeros((n_tok, width), F32)
    for h in range(n_heads):
        out = jnp.where(lane_head == h, acc[h * n_tok:(h + 1) * n_tok, :], out)
    return out


def _new_token_partial(qb, kn, vn, bias, n_tok):
    rows = qb.shape[0]
    s = _dot_nt(qb, kn.astype(BF16)) + bias
    tok = _iota((rows, n_tok), 0) % n_tok
    s = jnp.where(tok >= _iota((rows, n_tok), 1), s, NEG_INF)
    m = jnp.max(s, axis=-1, keepdims=True)
    p = jnp.exp(s - m)
    l = jnp.sum(p, axis=-1, keepdims=True)
    return m, l, _dot(p.astype(BF16), vn.astype(BF16))


def _moba_dec_body(pt_ref, q_ref, kn_ref, vn_ref, sl_ref, *refs, pages_per_step, page, n_tok, past_len):
    pp = pages_per_step
    k_refs, v_refs = refs[:pp], refs[pp:2 * pp]
    o_ref = refs[2 * pp]
    qb_scr, g_scr, m_scr, l_scr, acc_scr = refs[2 * pp + 1:]
    s_id = pl.program_id(1)
    nh, d = MOBA_HEADS, HEAD_DIM
    rows = nh * n_tok
    ppb = MOBA_BLOCK // page
    n_past = past_len // MOBA_BLOCK
    scale = d ** -0.5
    slope = sl_ref[...]
    tok = (_iota((rows, 1), 0) % n_tok).astype(F32)
    lane = _iota((rows, LANES), 1)

    @pl.when(s_id == 0)
    def _():
        qb_scr[...] = (_block_diag_queries(q_ref[...], nh, n_tok) * scale).astype(BF16)
        g_scr[...] = jnp.zeros_like(g_scr)
        m_scr[...] = jnp.zeros_like(m_scr)
        l_scr[...] = jnp.zeros_like(l_scr)

    qb = qb_scr[...]
    blocks = range(pp // ppb)
    pages = range(pp)
    blk0 = s_id * (pp // ppb)
    qk = [_dot(qb, k_refs[u][...].astype(BF16)) for u in pages]
    qk_sum = [jnp.sum(qk[u], axis=-1, keepdims=True) for u in pages]
    pos0 = (blk0 * MOBA_BLOCK).astype(F32) + _iota((rows, page), 1).astype(F32)
    s = [qk[u] - slope * ((past_len + tok) - (pos0 + float(u * page))) for u in pages]
    s_max = [jnp.max(s[u], axis=-1, keepdims=True) for u in pages]
    m = [functools.reduce(jnp.maximum, s_max[g * ppb:(g + 1) * ppb]) for g in blocks]
    p = [jnp.exp(s[u] - m[u // ppb]) for u in pages]
    p_sum = [jnp.sum(p[u], axis=-1, keepdims=True) for u in pages]
    pv = [_dot_nt(p[u].astype(BF16), v_refs[u][...].astype(BF16)) for u in pages]
    g_all, m_all, l_all = g_scr[...], m_scr[...], l_scr[...]
    for g in blocks:
        mine = slice(g * ppb, (g + 1) * ppb)
        gate = _add_all(qk_sum[mine]) * (1.0 / (scale * MOBA_BLOCK))
        g_all = jnp.where(lane == blk0 + g, gate, g_all)
        m_all = jnp.where(lane == blk0 + g, m[g], m_all)
        l_all = jnp.where(lane == blk0 + g, _add_all(p_sum[mine]), l_all)
        acc_scr[blk0 + g] = _add_all(pv[mine])
    g_scr[...], m_scr[...], l_scr[...] = g_all, m_all, l_all

    @pl.when(s_id == pl.num_programs(1) - 1)
    def _():
        sel = _topk_blocks(g_scr[...], n_past, n_past, axis=1)
        m_all = m_scr[...]
        t_k = _iota((rows, n_tok), 1).astype(F32)
        m_own, l_own, acc_own = _new_token_partial(qb, kn_ref[...], vn_ref[...], -slope * (tok - t_k), n_tok)
        m_tot = jnp.maximum(m_own, jnp.max(jnp.where(sel, m_all, NEG_INF), axis=-1, keepdims=True))
        w = jnp.where(sel, jnp.exp(m_all - m_tot), 0.0)
        w_own = jnp.exp(m_own - m_tot)
        l_tot = w_own * l_own + jnp.sum(w * l_scr[...], axis=-1, keepdims=True)
        acc = w_own * acc_own
        for n in range(n_past):
            acc = acc + w[:, n:n + 1] * acc_scr[n]
        o_ref[...] = _head_diag(acc / l_tot, nh, n_tok)


def _pages_position_minor(cache):
    n_phys, page, heads, d = cache.shape
    return jnp.transpose(cache, (0, 2, 3, 1)).reshape(n_phys, heads * d, page)


def moba_decode(q, k_new, v_new, cache_k, cache_v, page_table, n_tok, pages_per_step):
    n_seq, n_pages = page_table.shape
    page = cache_k.shape[1]
    nh, d = MOBA_HEADS, HEAD_DIM
    width = nh * d
    past_len = n_pages * page
    rows = nh * n_tok
    assert MOBA_BLOCK % page == 0 and past_len % MOBA_BLOCK == 0 and n_tok <= MOBA_BLOCK
    assert pages_per_step % (MOBA_BLOCK // page) == 0 and n_pages % pages_per_step == 0
    n_past = past_len // MOBA_BLOCK
    ck = _pages_position_minor(cache_k)
    cv = _pages_position_minor(cache_v)
    slope_rows = jnp.asarray(np.repeat(
        np.array([2.0 ** (-8.0 * (i + 1) / nh) for i in range(nh)], dtype=np.float32), n_tok).reshape(rows, 1))
    tok_spec = pl.BlockSpec((n_tok, width), lambda b, s, pt: (b, 0))

    assert n_past <= LANES

    def page_spec(u):
        return pl.BlockSpec((None, width, page), lambda b, s, pt: (pt[b, s * pages_per_step + u], 0, 0))

    grid_spec = pltpu.PrefetchScalarGridSpec(
        num_scalar_prefetch=1,
        grid=(n_seq, n_pages // pages_per_step),
        in_specs=[tok_spec, tok_spec, tok_spec, pl.BlockSpec((rows, 1), lambda b, s, pt: (0, 0))]
        + [page_spec(u) for u in range(pages_per_step)] * 2,
        out_specs=tok_spec,
        scratch_shapes=[pltpu.VMEM((rows, width), BF16), pltpu.VMEM((rows, LANES), F32),
                        pltpu.VMEM((rows, LANES), F32), pltpu.VMEM((rows, LANES), F32),
                        pltpu.VMEM((n_past, rows, width), F32)],
    )
    return pl.pallas_call(
        functools.partial(_moba_dec_body, pages_per_step=pages_per_step, page=page, n_tok=n_tok, past_len=past_len),
        grid_spec=grid_spec,
        out_shape=jax.ShapeDtypeStruct((n_seq * n_tok, width), F32),
        compiler_params=pltpu.CompilerParams(dimension_semantics=("arbitrary", "arbitrary")),
        name="moba_decode",
    )(page_table, q, k_new, v_new, slope_rows, *([ck] * pages_per_step), *([cv] * pages_per_step))


def _fox_dec_body(pt_ref, q_ref, kn_ref, vn_ref, f_ref, fb_ref, *refs, pages_per_step, page, n_tok):
    pp = pages_per_step
    k_refs, v_refs, lf_refs = refs[:pp], refs[pp:2 * pp], refs[2 * pp:3 * pp]
    o_ref, lfn_ref = refs[3 * pp], refs[3 * pp + 1]
    qb_scr, carry_scr, ctot_scr, m_scr, l_scr, acc_scr = refs[3 * pp + 2:]
    s_id = pl.program_id(1)
    nh, d = FOX_HEADS, HEAD_DIM
    rows = nh * n_tok
    expand = ((_iota((rows, nh), 0) // n_tok) == _iota((rows, nh), 1)).astype(F32)
    triu = (_iota((page, page), 0) <= _iota((page, page), 1)).astype(BF16)

    @pl.when(s_id == 0)
    def _():
        qb_scr[...] = (_block_diag_queries(q_ref[...], nh, n_tok) * (d ** -0.5)).astype(BF16)
        carry_scr[...] = jnp.zeros_like(carry_scr)
        m_scr[...] = jnp.full_like(m_scr, NEG_INF)
        l_scr[...] = jnp.zeros_like(l_scr)
        acc_scr[...] = jnp.zeros_like(acc_scr)

    qb = qb_scr[...]
    ss = []
    offset = carry_scr[...]
    for u in range(pp):
        pieces = jnp.concatenate([p.astype(BF16) for p in _bf16_pieces(lf_refs[u][...])], axis=0)
        sums = _dot(pieces, triu)
        c_heads = offset + _add_all([sums[n * nh:(n + 1) * nh] for n in range(SPLIT_PIECES)])
        offset = c_heads[:, page - 1:page]
        c_rows = jnp.broadcast_to(c_heads[:, None, :], (nh, n_tok, page)).reshape(rows, page)
        ss.append(_dot(qb, k_refs[u][...].astype(BF16)) - c_rows)
    carry_scr[...] = offset
    ctot_scr[...] = c_rows[:, page - 1:page]
    m_prev = m_scr[...]
    m_new = m_prev
    for s in ss:
        m_new = jnp.maximum(m_new, jnp.max(s, axis=-1, keepdims=True))
    alpha = jnp.exp(m_prev - m_new)
    l_new = alpha * l_scr[...]
    pv = None
    for u, s in enumerate(ss):
        p = jnp.exp(s - m_new)
        l_new = l_new + jnp.sum(p, axis=-1, keepdims=True)
        part = _dot_nt(p.astype(BF16), v_refs[u][...].astype(BF16))
        pv = part if pv is None else pv + part
    acc_scr[...] = alpha * acc_scr[...] + pv
    l_scr[...] = l_new
    m_scr[...] = m_new

    @pl.when(s_id == pl.num_programs(1) - 1)
    def _():
        lf_new = _log_sigmoid(f_ref[...] + fb_ref[...])
        lfn_ref[...] = lf_new
        tril_t = (_iota((n_tok, n_tok), 0) >= _iota((n_tok, n_tok), 1)).astype(F32)
        c_new = _dot(tril_t, lf_new, precision=HIGHEST)[:, :nh]
        c_new_rows = jnp.concatenate([c_new] * nh, axis=0)
        c_q = jnp.sum(expand * c_new_rows, axis=-1, keepdims=True)
        m_past = m_scr[...] + (ctot_scr[...] + c_q)
        c_k = _dot_nt(expand, c_new, precision=HIGHEST)
        m_own, l_own, acc_own = _new_token_partial(qb, kn_ref[...], vn_ref[...], c_q - c_k, n_tok)
        m_tot = jnp.maximum(m_past, m_own)
        w_past = jnp.exp(m_past - m_tot)
        w_own = jnp.exp(m_own - m_tot)
        l_tot = w_past * l_scr[...] + w_own * l_own
        acc = w_past * acc_scr[...] + w_own * acc_own
        o_ref[...] = _head_diag(acc / l_tot, nh, n_tok)


def fox_decode(q, k_new, v_new, f_new, f_bias, cache_k, cache_v, cache_lf, page_table, n_tok, pages_per_step):
    n_seq, n_pages = page_table.shape
    page = cache_k.shape[1]
    nh, d = FOX_HEADS, HEAD_DIM
    width = nh * d
    rows = nh * n_tok
    assert n_pages % pages_per_step == 0
    assert n_tok % SUBLANES == 0
    ck = _pages_position_minor(cache_k)
    cv = _pages_position_minor(cache_v)
    clf =jnp.transpose(cache_lf, (0, 2, 1))
    tok_spec = pl.BlockSpec((n_tok, width), lambda b, s, pt: (b, 0))
    gate_spec = pl.BlockSpec((n_tok, GATE_PAD), lambda b, s, pt: (b, 0))

    def page_spec(u, dims):
        return pl.BlockSpec((None,) + dims, lambda b, s, pt: (pt[b, s * pages_per_step + u],) + (0,) * len(dims))

    grid_spec = pltpu.PrefetchScalarGridSpec(
        num_scalar_prefetch=1,
        grid=(n_seq, n_pages // pages_per_step),
        in_specs=[tok_spec, tok_spec, tok_spec, gate_spec, pl.BlockSpec((1, GATE_PAD), lambda b, s, pt: (0, 0))]
        + [page_spec(u, (width, page)) for u in range(pages_per_step)] * 2
        + [page_spec(u, (nh, page)) for u in range(pages_per_step)],
        out_specs=[tok_spec, gate_spec],
        scratch_shapes=[pltpu.VMEM((rows, width), BF16), pltpu.VMEM((nh, 1), F32), pltpu.VMEM((rows, 1), F32),
                        pltpu.VMEM((rows, 1), F32), pltpu.VMEM((rows, 1), F32), pltpu.VMEM((rows, width), F32)],
    )
    return pl.pallas_call(
        functools.partial(_fox_dec_body, pages_per_step=pages_per_step, page=page, n_tok=n_tok),
        grid_spec=grid_spec,
        out_shape=[jax.ShapeDtypeStruct((n_seq * n_tok, width), F32),
                   jax.ShapeDtypeStruct((n_seq * n_tok, GATE_PAD), F32)],
        compiler_params=pltpu.CompilerParams(dimension_semantics=("arbitrary", "arbitrary")),
        name="fox_decode",
    )(page_table, q, k_new, v_new, f_new, f_bias,
      *([ck] * pages_per_step), *([cv] * pages_per_step), *([clf] * pages_per_step))


def _pad_lanes(w, width):
    return jnp.pad(w, ((0, 0), (0, width - w.shape[1])))


def _even_pieces(w_in):
    nq = MLSTM_HEADS * MLSTM_DQK
    nv = MLSTM_HEADS * MLSTM_DV
    nm = MOBA_HEADS * HEAD_DIM
    sizes = (nq, nq, nv, nv, 2 * MLSTM_HEADS, nm, nm, nm)
    pieces, start = [], 0
    for s in sizes:
        pieces.append(w_in[:, start:start + s])
        start += s
    assert start == w_in.shape[1]
    pieces[4] = _pad_lanes(pieces[4], GATE_PAD)
    return [p.astype(BF16) for p in pieces]


def _odd_pieces(w_in):
    nw = FOX_HEADS * HEAD_DIM
    pieces = [w_in[:, 0:nw], w_in[:, nw:2 * nw], w_in[:, 2 * nw:3 * nw], _pad_lanes(w_in[:, 3 * nw:], GATE_PAD)]
    assert w_in.shape[1] == 3 * nw + FOX_HEADS
    return [p.astype(BF16) for p in pieces]


def _token_major_view(x_t, heads):
    batch, _, seq = x_t.shape
    return jnp.transpose(x_t.reshape(batch, heads, HEAD_DIM, seq), (0, 3, 1, 2))


PROMPT_TM = 512
FFN_TM = 512
MLSTM_PROMPT_BATCH = 4
MLSTM_SAMPLE_BATCH = 8
FOX_GATE_TM = 512
MOBA_PAGES_PER_STEP = 16
FOX_PAGES_PER_STEP = 8


def kernel(x_prompt, x_sample, cache_moba_k, cache_moba_v, state_mlstm_C, state_mlstm_n, state_mlstm_m, cache_fox_k, cache_fox_v, cache_fox_logf, page_table, norm_mix_g, norm_ffn_g, norm_final_g, even_w_in, even_b_ig, even_b_fg, even_head_norm_g, even_w_out, odd_w_in, odd_b_f, odd_w_out, ffn_w_gate, ffn_w_up, ffn_w_down):
    bp, tp, d = x_prompt.shape
    bs, ts, _ = x_sample.shape
    depth = norm_mix_g.shape[0]
    mp, ms = bp * tp, bs * ts
    hp = x_prompt.reshape(mp, d)
    hs = x_sample.reshape(ms, d)
    tm_p = min(PROMPT_TM, mp)
    tm_s = ms
    ffn_tm_p = min(FFN_TM, mp)
    nv = MLSTM_HEADS * MLSTM_DV
    n_pages = page_table.shape[1]
    pe, se, po, so = [], [], [], []
    for layer in range(depth):
        li = layer // 2
        final = layer == depth - 1
        if layer % 2 == 0:
            ws = _even_pieces(even_w_in[li])
            gate_bias = _pad_lanes(jnp.concatenate([even_b_ig[li], even_b_fg[li]]).reshape(1, -1), GATE_PAD)
            w_out = even_w_out[li].astype(BF16)
            w_out_a, w_out_b = w_out[:nv], w_out[nv:]
            zeros = lambda *s: jnp.zeros(s, F32)
            q1, k1, v1, o1, gt, q2, k2, v2, k2_t, v2_t = norm_proj(
                hp, norm_mix_g[layer], ws, tm_p, ws_t=(ws[6].T, ws[7].T), seq=tp)
            y1, c_p, n_p, m_p = mlstm(q1, k1, v1, o1, gt, gate_bias, even_head_norm_g[li],
                                      zeros(bp, MLSTM_HEADS, MLSTM_DQK, MLSTM_DV), zeros(bp, MLSTM_HEADS, MLSTM_DQK),
                                      zeros(bp, MLSTM_HEADS), bp, tp, math.gcd(bp, MLSTM_PROMPT_BATCH))
            y2 = moba_prompt(q2, k2, v2, bp, tp)
            mixed_p, ws_out = [y1, y2], [w_out_a, w_out_b]
            pe.append((_token_major_view(k2_t, MOBA_HEADS), _token_major_view(v2_t, MOBA_HEADS), c_p, n_p, m_p))
            q1, k1, v1, o1, gt, q2, k2, v2 = norm_proj(hs, norm_mix_g[layer], ws, tm_s)
            y1, c_s, n_s, m_s = mlstm(q1, k1, v1, o1, gt, gate_bias, even_head_norm_g[li],
                                      state_mlstm_C[li], state_mlstm_n[li], state_mlstm_m[li], bs, ts,
                                      math.gcd(bs, MLSTM_SAMPLE_BATCH))
            y2 = moba_decode(q2, k2, v2, cache_moba_k[li], cache_moba_v[li], page_table, ts,
                             min(MOBA_PAGES_PER_STEP, n_pages))
            mixed_s = [y1, y2]
            se.append((k2.reshape(bs, ts, MOBA_HEADS, HEAD_DIM), v2.reshape(bs, ts, MOBA_HEADS, HEAD_DIM), c_s, n_s, m_s))
        else:
            ws = _odd_pieces(odd_w_in[li])
            f_bias = _pad_lanes(odd_b_f[li].reshape(1, -1), GATE_PAD)
            w_out = odd_w_out[li].astype(BF16)
            q, k, v, f, k_t, v_t = norm_proj(hp, norm_mix_g[layer], ws, tm_p, ws_t=(ws[1].T, ws[2].T), seq=tp)
            lf, c_col = fox_gates(f, f_bias, tp, min(FOX_GATE_TM, tp))
            mixed_p, ws_out = [fox_prompt(q, k, v, c_col, bp, tp)], [w_out]
            po.append((_token_major_view(k_t, FOX_HEADS), _token_major_view(v_t, FOX_HEADS),
                       lf[:, :FOX_HEADS].reshape(bp, tp, FOX_HEADS)))
            q, k, v, f = norm_proj(hs, norm_mix_g[layer], ws, tm_s)
            y, lf = fox_decode(q, k, v, f, f_bias, cache_fox_k[li], cache_fox_v[li], cache_fox_logf[li],
                               page_table, ts, min(FOX_PAGES_PER_STEP, n_pages))
            mixed_s = [y]
            so.append((k.reshape(bs, ts, FOX_HEADS, HEAD_DIM), v.reshape(bs, ts, FOX_HEADS, HEAD_DIM),
                       lf[:, :FOX_HEADS].reshape(bs, ts, FOX_HEADS)))
        wg, wu, wd = ffn_w_gate[layer].astype(BF16), ffn_w_up[layer].astype(BF16), ffn_w_down[layer].astype(BF16)
        hp = mix_ffn(hp, mixed_p, ws_out, norm_ffn_g[layer], norm_final_g, wg, wu, wd, ffn_tm_p, final)
        hs = mix_ffn(hs, mixed_s, ws_out, norm_ffn_g[layer], norm_final_g, wg, wu, wd, tm_s, final)
    stack = lambda group, idx: jnp.stack([s[idx] for s in group])
    return (hp.reshape(bp, tp, d), hs.reshape(bs, ts, d),
            stack(pe, 0), stack(pe, 1), stack(pe, 2), stack(pe, 3), stack(pe, 4),
            stack(po, 0), stack(po, 1), stack(po, 2),
            stack(se, 0), stack(se, 1), stack(se, 2), stack(se, 3), stack(se, 4),
            stack(so, 0), stack(so, 1), stack(so, 2))
```

```python
import functools
import math

import jax
import jax.numpy as jnp
import ml_dtypes
import numpy as np
from jax import lax
from jax.experimental import pallas as pl
from jax.experimental.pallas import tpu as pltpu

F32 = jnp.float32
BF16 = jnp.bfloat16
HIGHEST = lax.Precision.HIGHEST

LANES = 128
SUBLANES = 8

HEAD_DIM = 64
MLSTM_HEADS = 4
MLSTM_DQK = 64
MLSTM_DV = 128
MLSTM_CHUNK = 64
MOBA_HEADS = 8
MOBA_BLOCK = 256
MOBA_TOPK = 3
FOX_HEADS = 16
RMS_EPS = 1e-6
GATE_PAD = LANES
HEADS_PER_TILE = LANES // HEAD_DIM

NEG_INF = float("-inf")
MASK_BIAS = -1e30
SPLIT_PIECES = 3


def _log_sigmoid(x):
    return -(jnp.maximum(-x, 0.0) + jnp.log1p(jnp.exp(-jnp.abs(x))))


def _rmsnorm(x, g):
    return x * lax.rsqrt(jnp.mean(x * x, axis=-1, keepdims=True) + RMS_EPS) * g


def _dot(a, b, precision=None):
    return jnp.dot(a, b, precision=precision, preferred_element_type=F32)


def _dot_nt(a, b, precision=None):
    return lax.dot_general(a, b, (((1,), (1,)), ((), ())), precision=precision, preferred_element_type=F32)


def _iota(shape, dim):
    return lax.broadcasted_iota(jnp.int32, shape, dim)


def _add_all(xs):
    return functools.reduce(lambda a, b: a + b, xs)


def _bf16_pieces(x):
    pieces, rest = [], x
    for _ in range(SPLIT_PIECES):
        p = rest.astype(BF16).astype(F32)
        pieces.append(p)
        rest = rest - p
    return pieces


def _norm_proj_body(x_ref, g_ref, *refs, n_w, n_t):
    w_refs, wt_refs = refs[:n_w], refs[n_w:n_w + n_t]
    o_refs, ot_refs = refs[n_w + n_t:2 * n_w + n_t], refs[2 * n_w + n_t:]
    xb = _rmsnorm(x_ref[...], g_ref[...]).astype(BF16)
    for w_ref, o_ref in zip(w_refs, o_refs):
        o_ref[...] = _dot(xb, w_ref[...])
    for wt_ref, ot_ref in zip(wt_refs, ot_refs):
        ot_ref[...] = _dot_nt(wt_ref[...], xb)


def norm_proj(x, g, ws, tm, ws_t=(), seq=None):
    m, d = x.shape
    assert m % tm == 0
    in_specs = [pl.BlockSpec((tm, d), lambda i: (i, 0)), pl.BlockSpec((1, d), lambda i: (0, 0))]
    in_specs += [pl.BlockSpec(w.shape, lambda i: (0, 0)) for w in tuple(ws) + tuple(ws_t)]
    out_specs = [pl.BlockSpec((tm, w.shape[1]), lambda i: (i, 0)) for w in ws]
    out_shape = [jax.ShapeDtypeStruct((m, w.shape[1]), F32) for w in ws]
    if ws_t:
        assert seq % tm == 0 and m % seq == 0
        per_seq = seq // tm
        out_specs += [pl.BlockSpec((None, w.shape[0], tm), lambda i: (i // per_seq, 0, i % per_seq)) for w in ws_t]
        out_shape += [jax.ShapeDtypeStruct((m // seq, w.shape[0], seq), F32) for w in ws_t]
    return pl.pallas_call(
        functools.partial(_norm_proj_body, n_w=len(ws), n_t=len(ws_t)),
        grid=(m // tm,), in_specs=in_specs, out_specs=out_specs, out_shape=out_shape,
        name="norm_proj",
    )(x, g.reshape(1, d), *ws, *ws_t)


def _mix_ffn_body(h_ref, *refs, n_a, final_norm):
    a_refs, wo_refs = refs[:n_a], refs[n_a:2 * n_a]
    g_ref, gf_ref, wg_ref, wu_ref, wd_ref, o_ref = refs[2 * n_a:]
    x = h_ref[...]
    for a_ref, wo_ref in zip(a_refs, wo_refs):
        x = x + _dot(a_ref[...].astype(BF16), wo_ref[...])
    xn = _rmsnorm(x, g_ref[...]).astype(BF16)
    gate = _dot(xn, wg_ref[...])
    up = _dot(xn, wu_ref[...])
    act = (gate * jax.nn.sigmoid(gate) * up).astype(BF16)
    y = x + _dot(act, wd_ref[...])
    o_ref[...] = _rmsnorm(y, gf_ref[...]) if final_norm else y


def mix_ffn(h, acts, ws_out, g, g_final, wg, wu, wd, tm, final_norm):
    m, d = h.shape
    assert m % tm == 0
    resident = lambda w: pl.BlockSpec(w.shape, lambda i: (0, 0), pipeline_mode=pl.Buffered(1))
    in_specs = [pl.BlockSpec((tm, d), lambda i: (i, 0))]
    in_specs += [pl.BlockSpec((tm, a.shape[1]), lambda i: (i, 0)) for a in acts]
    in_specs += [resident(w) for w in ws_out]
    in_specs += [pl.BlockSpec((1, d), lambda i: (0, 0)), pl.BlockSpec((1, d), lambda i: (0, 0)),
                 resident(wg), resident(wu), resident(wd)]
    return pl.pallas_call(
        functools.partial(_mix_ffn_body, n_a=len(acts), final_norm=final_norm),
        grid=(m // tm,), in_specs=in_specs,
        out_specs=pl.BlockSpec((tm, d), lambda i: (i, 0)),
        out_shape=jax.ShapeDtypeStruct((m, d), F32),
        name="mix_ffn",
    )(h, *acts, *ws_out, g.reshape(1, d), g_final.reshape(1, d), wg, wu, wd)


def _mlstm_body(q_ref, k_ref, v_ref, o_ref, g_ref, gb_ref, gh_ref, c0_ref, n0_ref, m0_ref,
                y_ref, c_ref, n_ref, m_ref, c_scr, n_scr, m_scr, *, chunk, n_batch):
    j = pl.program_id(1)
    nh, dk, dv = MLSTM_HEADS, MLSTM_DQK, MLSTM_DV

    @pl.when(j == 0)
    def _():
        c_scr[...] = c0_ref[...]
        n_scr[...] = n0_ref[...]
        m_scr[...] = m0_ref[...]

    lane = _iota((chunk, GATE_PAD), 1)
    r_i = _iota((chunk, chunk), 0)
    c_i = _iota((chunk, chunk), 1)
    causal = r_i >= c_i
    eye = r_i == c_i
    tril = causal.astype(F32)
    eye_b = (_iota((dk, dk), 0) == _iota((dk, dk), 1)).astype(BF16)

    def to_row(col):
        return jnp.sum(jnp.where(eye, col, 0.0), axis=0, keepdims=True)

    chains = [(b, h) for b in range(n_batch) for h in range(nh)]
    each = lambda fn: [fn(n, b, h) for n, (b, h) in enumerate(chains)]

    acts, cums = [], []
    for b in range(n_batch):
        gpre = g_ref[b] + gb_ref[...]
        acts.append(jnp.where(lane < nh, gpre, _log_sigmoid(gpre)))
        cums.append(_dot(tril, acts[b], precision=HIGHEST))
    q = each(lambda n, b, h: q_ref[b, :, h * dk:(h + 1) * dk] * (dk ** -0.5))
    k = each(lambda n, b, h: k_ref[b, :, h * dk:(h + 1) * dk])
    vb = each(lambda n, b, h: v_ref[b, :, h * dv:(h + 1) * dv].astype(BF16))
    qb = each(lambda n, b, h: q[n].astype(BF16))
    ig_c = each(lambda n, b, h: acts[b][:, h:h + 1])
    b_c = each(lambda n, b, h: cums[b][:, nh + h:nh + h + 1])
    m_prev = each(lambda n, b, h: m_scr[b, h])
    c_prev = each(lambda n, b, h: c_scr[b, h])
    n_prev = each(lambda n, b, h: n_scr[b, h])
    dmat = each(lambda n, b, h: jnp.where(causal, b_c[n] - to_row(b_c[n]) + to_row(ig_c[n]), NEG_INF))
    inter = each(lambda n, b, h: b_c[n] + m_prev[n])
    mt = each(lambda n, b, h: jnp.maximum(inter[n], jnp.max(dmat[n], axis=-1, keepdims=True)))
    w_inter = each(lambda n, b, h: jnp.exp(inter[n] - mt[n]))
    qk = each(lambda n, b, h: _dot_nt(qb[n], k[n].astype(BF16)) * jnp.exp(dmat[n] - mt[n]))
    num = each(lambda n, b, h: w_inter[n] * _dot(qb[n], c_prev[n].astype(BF16)) + _dot(qk[n].astype(BF16), vb[n]))
    den = each(lambda n, b, h: w_inter[n] * jnp.sum(q[n] * n_prev[n], axis=-1, keepdims=True)
               + jnp.sum(qk[n], axis=-1, keepdims=True))
    hh = each(lambda n, b, h: num[n] / jnp.maximum(jnp.abs(den[n]), jnp.exp(-mt[n])))
    m_new = each(lambda n, b, h: mt[n][chunk - 1:chunk, :])
    b_last = each(lambda n, b, h: b_c[n][chunk - 1:chunk, :])
    decay = each(lambda n, b, h: jnp.exp(b_last[n] + m_prev[n] - m_new[n]))
    kw = each(lambda n, b, h: k[n] * jnp.exp(b_last[n] - b_c[n] + ig_c[n] - m_new[n]))
    kw_t = each(lambda n, b, h: _dot_nt(eye_b, kw[n].astype(BF16)).astype(BF16))
    for n, (b, h) in enumerate(chains):
        c_scr[b, h] = decay[n] * c_prev[n] + _dot(kw_t[n], vb[n])
        n_scr[b, h] = decay[n] * n_prev[n] + jnp.sum(kw[n], axis=0, keepdims=True)
        m_scr[b, h] = m_new[n]
    for n, (b, h) in enumerate(chains):
        hn = hh[n] * lax.rsqrt(jnp.mean(hh[n] * hh[n], axis=-1, keepdims=True) + RMS_EPS)
        hn = hn * gh_ref[:, h * dv:(h + 1) * dv]
        y_ref[b, :, h * dv:(h + 1) * dv] = jax.nn.sigmoid(o_ref[b, :, h * dv:(h + 1) * dv]) * hn

    @pl.when(j == pl.num_programs(1) - 1)
    def _():
        c_ref[...] = c_scr[...]
        n_ref[...] = n_scr[...]
        m_ref[...] = m_scr[...]


def mlstm(q, k, v, o, gates, gate_bias, g_head, c0, n0, m0, batch, seq, n_batch):
    nh, dk, dv = MLSTM_HEADS, MLSTM_DQK, MLSTM_DV
    chunk = MLSTM_CHUNK if seq % MLSTM_CHUNK == 0 else seq
    assert chunk % SUBLANES == 0 and batch % n_batch == 0
    steps = seq // chunk
    row_spec = lambda w: pl.BlockSpec((n_batch, chunk, w), lambda b, j: (b, j, 0))
    st4 = lambda s: pl.BlockSpec((n_batch,) + s, lambda b, j: (b, 0, 0, 0))
    rows3 = lambda x: x.reshape(batch, seq, x.shape[-1])
    y, c, n, m = pl.pallas_call(
        functools.partial(_mlstm_body, chunk=chunk, n_batch=n_batch),
        grid=(batch // n_batch, steps),
        in_specs=[row_spec(nh * dk), row_spec(nh * dk), row_spec(nh * dv), row_spec(nh * dv), row_spec(GATE_PAD),
                  pl.BlockSpec((1, GATE_PAD), lambda b, j: (0, 0)),
                  pl.BlockSpec((1, nh * dv), lambda b, j: (0, 0)),
                  st4((nh, dk, dv)), st4((nh, 1, dk)), st4((nh, 1, 1))],
        out_specs=[row_spec(nh * dv), st4((nh, dk, dv)), st4((nh, 1, dk)), st4((nh, 1, 1))],
        out_shape=[jax.ShapeDtypeStruct((batch, seq, nh * dv), F32),
                   jax.ShapeDtypeStruct((batch, nh, dk, dv), F32),
                   jax.ShapeDtypeStruct((batch, nh, 1, dk), F32),
                   jax.ShapeDtypeStruct((batch, nh, 1, 1), F32)],
        scratch_shapes=[pltpu.VMEM((n_batch, nh, dk, dv), F32), pltpu.VMEM((n_batch, nh, 1, dk), F32),
                        pltpu.VMEM((n_batch, nh, 1, 1), F32)],
        compiler_params=pltpu.CompilerParams(dimension_semantics=("arbitrary", "arbitrary")),
        name="mlstm",
    )(rows3(q), rows3(k), rows3(v), rows3(o), rows3(gates), gate_bias, g_head.reshape(1, nh * dv),
      c0, n0.reshape(batch, nh, 1, dk), m0.reshape(batch, nh, 1, 1))
    return y.reshape(batch * seq, nh * dv), c, n.reshape(batch, nh, dk), m.reshape(batch, nh)


def _topk_blocks(gate, n_blocks, n_valid, axis):
    idx = _iota(gate.shape, axis)
    rank = jnp.zeros(gate.shape, F32)
    for c in range(n_blocks):
        other = gate[c:c + 1, :] if axis == 0 else gate[:, c:c + 1]
        beats = (other > gate) | ((other == gate) & (c < idx))
        candidate = jnp.where(c < n_valid, 1.0, 0.0)
        rank = rank + jnp.where(beats, candidate, 0.0)
    return (rank < MOBA_TOPK) & (idx < n_valid)


ATT_TILE = 512


def _data_lanes(lane, hh):
    return (lane >= hh * HEAD_DIM) & (lane < (hh + 1) * HEAD_DIM)


def _extra_lane0(hh):
    return (HEADS_PER_TILE - 1 - hh) * HEAD_DIM


def _values_with_ones(v, lane, hh):
    return jnp.where(_data_lanes(lane, hh), v, jnp.where(lane == _extra_lane0(hh), 1.0, 0.0)).astype(BF16)


def _flash_pair(qa, ka_scr, va_scr, i, tile):
    heads = range(HEADS_PER_TILE)
    causal = _iota((tile, tile), 0) >= _iota((tile, tile), 1)

    def step(key_tiles, carry, first_is_diagonal=False):
        starts = [pl.multiple_of(j * tile, tile) for j in key_tiles]
        chains = [(hh, t) for t in range(len(starts)) for hh in heads]
        s = {c: _dot_nt(qa[c[0]], ka_scr[c[0], pl.ds(starts[c[1]], tile), :]) for c in chains}
        if first_is_diagonal:
            s.update({(hh, 0): jnp.where(causal, s[hh, 0], NEG_INF) for hh in heads})
        s_max = {c: jnp.max(s[c], axis=-1, keepdims=True) for c in chains}
        m = {(hh, -1): carry[hh][0] for hh in heads}
        for hh, t in chains:
            m[hh, t] = jnp.maximum(m[hh, t - 1], s_max[hh, t])
        p = {c: jnp.exp(s[c] - m[c]).astype(BF16) for c in chains}
        pv = {c: _dot(p[c], va_scr[c[0], pl.ds(starts[c[1]], tile), :]) for c in chains}
        acc = [carry[hh][1] for hh in heads]
        for hh, t in chains:
            acc[hh] = jnp.exp(m[hh, t - 1] - m[hh, t]) * acc[hh] + pv[hh, t]
        return tuple((m[hh, len(starts) - 1], acc[hh]) for hh in heads)

    carry = tuple((jnp.full((tile, 1), NEG_INF, F32), jnp.zeros((tile, LANES), F32)) for _ in heads)
    n_group = lax.shift_right_logical(i, 2)
    carry = step([i], carry, True)
    carry = lax.fori_loop(0, n_group, lambda j, c: step([4 * j + t for t in range(4)], c), carry)
    carry = lax.fori_loop(n_group * 4, i, lambda j, c: step([j], c), carry)
    lane = _iota((tile, LANES), 1)
    out = jnp.zeros((tile, LANES), F32)
    for hh in heads:
        acc = carry[hh][1]
        e0 = _extra_lane0(hh)
        out = jnp.where(_data_lanes(lane, hh), acc / acc[:, e0:e0 + 1], out)
    return out


def _floor_div_pow2(x, n):
    assert n & (n - 1) == 0
    return lax.shift_right_logical(x, n.bit_length() - 1)


def _alibi_digits(pos):
    assert HEAD_DIM & (HEAD_DIM - 1) == 0
    lo = pos & (HEAD_DIM - 1)
    return (pos - lo).astype(F32), lo.astype(F32)


ALIBI_ROWS = 8


def _alibi_table(n_heads, n_blocks):
    rest = np.array([2.0 ** (-8.0 * (i + 1) / n_heads) for i in range(n_heads)], dtype=np.float32)
    table = np.zeros((n_heads, ALIBI_ROWS, LANES), np.float32)
    for n in range(SPLIT_PIECES):
        piece = rest.astype(ml_dtypes.bfloat16).astype(np.float32)
        rest = rest - piece
        for h in range(n_heads):
            base = _extra_lane0(h % HEADS_PER_TILE) + n_blocks + 4 * n
            table[h, 0, base + 2:base + 4] = piece[h]
            table[h, 1, base] = 1.0
            table[h, 2, base + 1] = 1.0
            table[h, 3, base:base + 2] = piece[h]
            table[h, 4, base + 2] = 1.0
            table[h, 5, base + 3] = 1.0
    return jnp.asarray(table)


def _moba_body(q_ref, k_ref, v_ref, al_ref, o_ref, ka_scr, vb_scr, km_scr, *, n_blocks, tile):
    i = pl.program_id(2)
    seq = k_ref.shape[0]
    d = HEAD_DIM

    def alibi_cols(hh, side, pos_hi, pos_lo):
        rows = al_ref[hh]
        if side == "query":
            return rows[0:1] - pos_hi * rows[1:2] - pos_lo * rows[2:3]
        return rows[3:4] + pos_hi * rows[4:5] + pos_lo * rows[5:6]

    @pl.when(i == 0)
    def _():
        k = k_ref[...]
        v = v_ref[...]
        lane = _iota((seq, LANES), 1)
        pos = _iota((seq, LANES), 0)
        pos_hi, pos_lo = _alibi_digits(pos)
        km_scr[...] = jnp.zeros_like(km_scr)
        for n in range(n_blocks):
            km_scr[n:n + 1, :] = jnp.mean(k[n * MOBA_BLOCK:(n + 1) * MOBA_BLOCK], axis=0, keepdims=True)
        for hh in range(HEADS_PER_TILE):
            idx = lane - _extra_lane0(hh)
            extras = jnp.where(idx == _floor_div_pow2(pos, MOBA_BLOCK), 1.0, alibi_cols(hh, "key", pos_hi, pos_lo))
            ka_scr[hh] = jnp.where(_data_lanes(lane, hh), k, extras).astype(BF16)
            vb_scr[hh] = _values_with_ones(v, lane, hh)

    q = q_ref[...]
    lane = _iota((tile, LANES), 1)
    t_pos = i * tile + _iota((tile, LANES), 0)
    t_hi, t_lo = _alibi_digits(t_pos)
    blk_rows = -(-n_blocks // SUBLANES) * SUBLANES
    own_t = _floor_div_pow2(i * tile + _iota((1, tile), 1), MOBA_BLOCK)
    qa = []
    for hh in range(HEADS_PER_TILE):
        e0 = _extra_lane0(hh)
        idx = lane - e0
        data = _data_lanes(lane, hh)
        gate_t = _dot_nt(km_scr[...], jnp.where(data, q, 0.0), precision=HIGHEST)
        keep_t = _topk_blocks(gate_t, n_blocks, own_t, axis=0) | (_iota((blk_rows, tile), 0) == own_t)
        above, below = e0, LANES - e0 - blk_rows
        bias_t = ([jnp.zeros((above, tile), F32)] if above else []) + [jnp.where(keep_t, 0.0, MASK_BIAS)]
        bias_t += [jnp.zeros((below, tile), F32)] if below else []
        block_bias = jnp.concatenate(bias_t, axis=0).T
        extras = jnp.where((idx >= 0) & (idx < n_blocks), block_bias, alibi_cols(hh, "query", t_hi, t_lo))
        qa.append(jnp.where(data, q * (d ** -0.5), extras).astype(BF16))
    o_ref[...] = _flash_pair(qa, ka_scr, vb_scr, i, tile)


def moba_prompt(q, k, v, batch, seq):
    width = MOBA_HEADS * HEAD_DIM
    tile = min(ATT_TILE, seq)
    assert seq % tile == 0 and tile % MOBA_BLOCK == 0
    n_blocks = seq // MOBA_BLOCK
    assert n_blocks + 4 * SPLIT_PIECES <= HEAD_DIM
    nq = seq // tile
    return pl.pallas_call(
        functools.partial(_moba_body, n_blocks=n_blocks, tile=tile),
        grid=(batch, width // LANES, nq),
        in_specs=[pl.BlockSpec((tile, LANES), lambda b, hp, i: (b * nq + i, hp)),
                  pl.BlockSpec((seq, LANES), lambda b, hp, i: (b, hp)),
                  pl.BlockSpec((seq, LANES), lambda b, hp, i: (b, hp)),
                  pl.BlockSpec((HEADS_PER_TILE, ALIBI_ROWS, LANES), lambda b, hp, i: (hp, 0, 0))],
        out_specs=pl.BlockSpec((tile, LANES), lambda b, hp, i: (b * nq + i, hp)),
        out_shape=jax.ShapeDtypeStruct((batch * seq, width), F32),
        scratch_shapes=[pltpu.VMEM((HEADS_PER_TILE, seq, LANES), BF16), pltpu.VMEM((HEADS_PER_TILE, seq, LANES), BF16),
                        pltpu.VMEM((-(-n_blocks // SUBLANES) * SUBLANES, LANES), F32)],
        compiler_params=pltpu.CompilerParams(dimension_semantics=("arbitrary", "arbitrary", "arbitrary")),
        name="moba_prompt",
    )(q, k, v, _alibi_table(MOBA_HEADS, n_blocks))


def _fox_gate_body(f_ref, b_ref, lf_ref, cc_ref, carry_scr, *, tiles_per_seq):
    i = pl.program_id(0)
    tm = f_ref.shape[0]

    @pl.when(i % tiles_per_seq == 0)
    def _():
        carry_scr[...] = jnp.zeros_like(carry_scr)

    lf = _log_sigmoid(f_ref[...] + b_ref[...])
    lf_ref[...] = lf
    tril = (_iota((tm, tm), 0) >= _iota((tm, tm), 1)).astype(F32)
    c = _dot(tril, lf, precision=HIGHEST) + carry_scr[...]
    carry_scr[...] = c[tm - 1:tm, :]
    cc_ref[...] = c


def fox_gates(f, bias, seq, tm):
    m = f.shape[0]
    assert seq % tm == 0 and m % seq == 0
    return pl.pallas_call(
        functools.partial(_fox_gate_body, tiles_per_seq=seq // tm),
        grid=(m // tm,),
        in_specs=[pl.BlockSpec((tm, GATE_PAD), lambda i: (i, 0)), pl.BlockSpec((1, GATE_PAD), lambda i: (0, 0))],
        out_specs=[pl.BlockSpec((tm, GATE_PAD), lambda i: (i, 0)), pl.BlockSpec((tm, GATE_PAD), lambda i: (i, 0))],
        out_shape=[jax.ShapeDtypeStruct((m, GATE_PAD), F32), jax.ShapeDtypeStruct((m, GATE_PAD), F32)],
        scratch_shapes=[pltpu.VMEM((1, GATE_PAD), F32)],
        compiler_params=pltpu.CompilerParams(dimension_semantics=("arbitrary",)),
        name="fox_gates",
    )(f, bias)


def _fox_body(q_ref, k_ref, v_ref, cq_ref, ck_ref, o_ref, ka_scr, vb_scr, *, tile):
    hp = pl.program_id(1)
    i = pl.program_id(2)
    seq = k_ref.shape[0]
    d = HEAD_DIM

    def head_column(c_all, hh):
        lane_g = _iota(c_all.shape, 1)
        return jnp.sum(jnp.where(lane_g == hp * HEADS_PER_TILE + hh, c_all, 0.0), axis=-1, keepdims=True)

    def sum_cols(idx, c_col, on_pos):
        cols = jnp.zeros(idx.shape, F32)
        for n, p in enumerate(_bf16_pieces(c_col)):
            key_side = on_pos == "key"
            cols = jnp.where(idx == n, 1.0 if key_side else p, cols)
            cols = jnp.where(idx == SPLIT_PIECES + n, -p if key_side else 1.0, cols)
        return cols

    @pl.when(i == 0)
    def _():
        k = k_ref[...]
        v = v_ref[...]
        lane = _iota((seq, LANES), 1)
        c_all = ck_ref[...]
        for hh in range(HEADS_PER_TILE):
            extras = sum_cols(lane - _extra_lane0(hh), head_column(c_all, hh), "key")
            ka_scr[hh] = jnp.where(_data_lanes(lane, hh), k, extras).astype(BF16)
            vb_scr[hh] = _values_with_ones(v, lane, hh)

    q = q_ref[...]
    lane = _iota((tile, LANES), 1)
    c_all = cq_ref[...]
    qa = []
    for hh in range(HEADS_PER_TILE):
        extras = sum_cols(lane - _extra_lane0(hh), head_column(c_all, hh), "query")
        qa.append(jnp.where(_data_lanes(lane, hh), q * (d ** -0.5), extras).astype(BF16))
    o_ref[...] = _flash_pair(qa, ka_scr, vb_scr, i, tile)


def fox_prompt(q, k, v, c_col, batch, seq):
    width = FOX_HEADS * HEAD_DIM
    tile = min(ATT_TILE, seq)
    assert seq % tile == 0 and 2 * SPLIT_PIECES <= HEAD_DIM
    nq = seq // tile
    return pl.pallas_call(
        functools.partial(_fox_body, tile=tile),
        grid=(batch, width // LANES, nq),
        in_specs=[pl.BlockSpec((tile, LANES), lambda b, hp, i: (b * nq + i, hp)),
                  pl.BlockSpec((seq, LANES), lambda b, hp, i: (b, hp)),
                  pl.BlockSpec((seq, LANES), lambda b, hp, i: (b, hp)),
                  pl.BlockSpec((tile, GATE_PAD), lambda b, hp, i: (b * nq + i, 0)),
                  pl.BlockSpec((seq, GATE_PAD), lambda b, hp, i: (b, 0))],
        out_specs=pl.BlockSpec((tile, LANES), lambda b, hp, i: (b * nq + i, hp)),
        out_shape=jax.ShapeDtypeStruct((batch * seq, width), F32),
        scratch_shapes=[pltpu.VMEM((HEADS_PER_TILE, seq, LANES), BF16), pltpu.VMEM((HEADS_PER_TILE, seq, LANES), BF16)],
        compiler_params=pltpu.CompilerParams(dimension_semantics=("arbitrary", "arbitrary", "arbitrary")),
        name="fox_prompt",
    )(q, k, v, c_col, c_col)


def _block_diag_queries(q, n_heads, n_tok):
    rows, width = n_heads * n_tok, n_heads * HEAD_DIM
    tiled = jnp.concatenate([q] * n_heads, axis=0)
    same = (_iota((rows, width), 0) // n_tok) == (_iota((rows, width), 1) // HEAD_DIM)
    return jnp.where(same, tiled, 0.0)


def _head_diag(acc, n_heads, n_tok):
    width = n_heads * HEAD_DIM
    lane_head = _iota((n_tok, width), 1) // HEAD_DIM
    out = jnp.zeros((n_tok, width), F32)
    for h in range(n_heads):
        out = jnp.where(lane_head == h, acc[h * n_tok:(h + 1) * n_tok, :], out)
    return out


def _new_token_partial(qb, kn, vn, bias, n_tok):
    rows = qb.shape[0]
    s = _dot_nt(qb, kn.astype(BF16)) + bias
    tok = _iota((rows, n_tok), 0) % n_tok
    s = jnp.where(tok >= _iota((rows, n_tok), 1), s, NEG_INF)
    m = jnp.max(s, axis=-1, keepdims=True)
    p = jnp.exp(s - m)
    l = jnp.sum(p, axis=-1, keepdims=True)
    return m, l, _dot(p.astype(BF16), vn.astype(BF16))


def _moba_dec_body(pt_ref, q_ref, kn_ref, vn_ref, sl_ref, *refs, pages_per_step, page, n_tok, past_len):
    pp = pages_per_step
    k_refs, v_refs = refs[:pp], refs[pp:2 * pp]
    o_ref = refs[2 * pp]
    qb_scr, g_scr, m_scr, l_scr, acc_scr = refs[2 * pp + 1:]
    s_id = pl.program_id(1)
    nh, d = MOBA_HEADS, HEAD_DIM
    rows = nh * n_tok
    ppb = MOBA_BLOCK // page
    n_past = past_len // MOBA_BLOCK
    scale = d ** -0.5
    slope = sl_ref[...]
    tok = (_iota((rows, 1), 0) % n_tok).astype(F32)
    lane = _iota((rows, LANES), 1)

    @pl.when(s_id == 0)
    def _():
        qb_scr[...] = (_block_diag_queries(q_ref[...], nh, n_tok) * scale).astype(BF16)
        g_scr[...] = jnp.zeros_like(g_scr)
        m_scr[...] = jnp.zeros_like(m_scr)
        l_scr[...] = jnp.zeros_like(l_scr)

    qb = qb_scr[...]
    blocks = range(pp // ppb)
    pages = range(pp)
    blk0 = s_id * (pp // ppb)
    qk = [_dot(qb, k_refs[u][...].astype(BF16)) for u in pages]
    qk_sum = [jnp.sum(qk[u], axis=-1, keepdims=True) for u in pages]
    pos0 = (blk0 * MOBA_BLOCK).astype(F32) + _iota((rows, page), 1).astype(F32)
    s = [qk[u] - slope * ((past_len + tok) - (pos0 + float(u * page))) for u in pages]
    s_max = [jnp.max(s[u], axis=-1, keepdims=True) for u in pages]
    m = [functools.reduce(jnp.maximum, s_max[g * ppb:(g + 1) * ppb]) for g in blocks]
    p = [jnp.exp(s[u] - m[u // ppb]) for u in pages]
    p_sum = [jnp.sum(p[u], axis=-1, keepdims=True) for u in pages]
    pv = [_dot_nt(p[u].astype(BF16), v_refs[u][...].astype(BF16)) for u in pages]
    g_all, m_all, l_all = g_scr[...], m_scr[...], l_scr[...]
    for g in blocks:
        mine = slice(g * ppb, (g + 1) * ppb)
        gate = _add_all(qk_sum[mine]) * (1.0 / (scale * MOBA_BLOCK))
        g_all = jnp.where(lane == blk0 + g, gate, g_all)
        m_all = jnp.where(lane == blk0 + g, m[g], m_all)
        l_all = jnp.where(lane == blk0 + g, _add_all(p_sum[mine]), l_all)
        acc_scr[blk0 + g] = _add_all(pv[mine])
    g_scr[...], m_scr[...], l_scr[...] = g_all, m_all, l_all

    @pl.when(s_id == pl.num_programs(1) - 1)
    def _():
        sel = _topk_blocks(g_scr[...], n_past, n_past, axis=1)
        m_all = m_scr[...]
        t_k = _iota((rows, n_tok), 1).astype(F32)
        m_own, l_own, acc_own = _new_token_partial(qb, kn_ref[...], vn_ref[...], -slope * (tok - t_k), n_tok)
        m_tot = jnp.maximum(m_own, jnp.max(jnp.where(sel, m_all, NEG_INF), axis=-1, keepdims=True))
        w = jnp.where(sel, jnp.exp(m_all - m_tot), 0.0)
        w_own = jnp.exp(m_own - m_tot)
        l_tot = w_own * l_own + jnp.sum(w * l_scr[...], axis=-1, keepdims=True)
        acc = w_own * acc_own
        for n in range(n_past):
            acc = acc + w[:, n:n + 1] * acc_scr[n]
        o_ref[...] = _head_diag(acc / l_tot, nh, n_tok)


def _pages_position_minor(cache):
    n_phys, page, heads, d = cache.shape
    return jnp.transpose(cache, (0, 2, 3, 1)).reshape(n_phys, heads * d, page)


def moba_decode(q, k_new, v_new, cache_k, cache_v, page_table, n_tok, pages_per_step):
    n_seq, n_pages = page_table.shape
    page = cache_k.shape[1]
    nh, d = MOBA_HEADS, HEAD_DIM
    width = nh * d
    past_len = n_pages * page
    rows = nh * n_tok
    assert MOBA_BLOCK % page == 0 and past_len % MOBA_BLOCK == 0 and n_tok <= MOBA_BLOCK
    assert pages_per_step % (MOBA_BLOCK // page) == 0 and n_pages % pages_per_step == 0
    n_past = past_len // MOBA_BLOCK
    ck = _pages_position_minor(cache_k)
    cv = _pages_position_minor(cache_v)
    slope_rows = jnp.asarray(np.repeat(
        np.array([2.0 ** (-8.0 * (i + 1) / nh) for i in range(nh)], dtype=np.float32), n_tok).reshape(rows, 1))
    tok_spec = pl.BlockSpec((n_tok, width), lambda b, s, pt: (b, 0))

    assert n_past <= LANES

    def page_spec(u):
        return pl.BlockSpec((None, width, page), lambda b, s, pt: (pt[b, s * pages_per_step + u], 0, 0))

    grid_spec = pltpu.PrefetchScalarGridSpec(
        num_scalar_prefetch=1,
        grid=(n_seq, n_pages // pages_per_step),
        in_specs=[tok_spec, tok_spec, tok_spec, pl.BlockSpec((rows, 1), lambda b, s, pt: (0, 0))]
        + [page_spec(u) for u in range(pages_per_step)] * 2,
        out_specs=tok_spec,
        scratch_shapes=[pltpu.VMEM((rows, width), BF16), pltpu.VMEM((rows, LANES), F32),
                        pltpu.VMEM((rows, LANES), F32), pltpu.VMEM((rows, LANES), F32),
                        pltpu.VMEM((n_past, rows, width), F32)],
    )
    return pl.pallas_call(
        functools.partial(_moba_dec_body, pages_per_step=pages_per_step, page=page, n_tok=n_tok, past_len=past_len),
        grid_spec=grid_spec,
        out_shape=jax.ShapeDtypeStruct((n_seq * n_tok, width), F32),
        compiler_params=pltpu.CompilerParams(dimension_semantics=("arbitrary", "arbitrary")),
        name="moba_decode",
    )(page_table, q, k_new, v_new, slope_rows, *([ck] * pages_per_step), *([cv] * pages_per_step))


def _fox_dec_body(pt_ref, q_ref, kn_ref, vn_ref, f_ref, fb_ref, *refs, pages_per_step, page, n_tok):
    pp = pages_per_step
    k_refs, v_refs, lf_refs = refs[:pp], refs[pp:2 * pp], refs[2 * pp:3 * pp]
    o_ref, lfn_ref = refs[3 * pp], refs[3 * pp + 1]
    qb_scr, carry_scr, ctot_scr, m_scr, l_scr, acc_scr = refs[3 * pp + 2:]
    s_id = pl.program_id(1)
    nh, d = FOX_HEADS, HEAD_DIM
    rows = nh * n_tok
    expand = ((_iota((rows, nh), 0) // n_tok) == _iota((rows, nh), 1)).astype(F32)
    triu = (_iota((page, page), 0) <= _iota((page, page), 1)).astype(BF16)

    @pl.when(s_id == 0)
    def _():
        qb_scr[...] = (_block_diag_queries(q_ref[...], nh, n_tok) * (d ** -0.5)).astype(BF16)
        carry_scr[...] = jnp.zeros_like(carry_scr)
        m_scr[...] = jnp.full_like(m_scr, NEG_INF)
        l_scr[...] = jnp.zeros_like(l_scr)
        acc_scr[...] = jnp.zeros_like(acc_scr)

    qb = qb_scr[...]
    ss = []
    offset = carry_scr[...]
    for u in range(pp):
        pieces = jnp.concatenate([p.astype(BF16) for p in _bf16_pieces(lf_refs[u][...])], axis=0)
        sums = _dot(pieces, triu)
        c_heads = offset + _add_all([sums[n * nh:(n + 1) * nh] for n in range(SPLIT_PIECES)])
        offset = c_heads[:, page - 1:page]
        c_rows = jnp.broadcast_to(c_heads[:, None, :], (nh, n_tok, page)).reshape(rows, page)
        ss.append(_dot(qb, k_refs[u][...].astype(BF16)) - c_rows)
    carry_scr[...] = offset
    ctot_scr[...] = c_rows[:, page - 1:page]
    m_prev = m_scr[...]
    m_new = m_prev
    for s in ss:
        m_new = jnp.maximum(m_new, jnp.max(s, axis=-1, keepdims=True))
    alpha = jnp.exp(m_prev - m_new)
    l_new = alpha * l_scr[...]
    pv = None
    for u, s in enumerate(ss):
        p = jnp.exp(s - m_new)
        l_new = l_new + jnp.sum(p, axis=-1, keepdims=True)
        part = _dot_nt(p.astype(BF16), v_refs[u][...].astype(BF16))
        pv = part if pv is None else pv + part
    acc_scr[...] = alpha * acc_scr[...] + pv
    l_scr[...] = l_new
    m_scr[...] = m_new

    @pl.when(s_id == pl.num_programs(1) - 1)
    def _():
        lf_new = _log_sigmoid(f_ref[...] + fb_ref[...])
        lfn_ref[...] = lf_new
        tril_t = (_iota((n_tok, n_tok), 0) >= _iota((n_tok, n_tok), 1)).astype(F32)
        c_new = _dot(tril_t, lf_new, precision=HIGHEST)[:, :nh]
        c_new_rows = jnp.concatenate([c_new] * nh, axis=0)
        c_q = jnp.sum(expand * c_new_rows, axis=-1, keepdims=True)
        m_past = m_scr[...] + (ctot_scr[...] + c_q)
        c_k = _dot_nt(expand, c_new, precision=HIGHEST)
        m_own, l_own, acc_own = _new_token_partial(qb, kn_ref[...], vn_ref[...], c_q - c_k, n_tok)
        m_tot = jnp.maximum(m_past, m_own)
        w_past = jnp.exp(m_past - m_tot)
        w_own = jnp.exp(m_own - m_tot)
        l_tot = w_past * l_scr[...] + w_own * l_own
        acc = w_past * acc_scr[...] + w_own * acc_own
        o_ref[...] = _head_diag(acc / l_tot, nh, n_tok)


def fox_decode(q, k_new, v_new, f_new, f_bias, cache_k, cache_v, cache_lf, page_table, n_tok, pages_per_step):
    n_seq, n_pages = page_table.shape
    page = cache_k.shape[1]
    nh, d = FOX_HEADS, HEAD_DIM
    width = nh * d
    rows = nh * n_tok
    assert n_pages % pages_per_step == 0
    assert n_tok % SUBLANES == 0
    ck = _pages_position_minor(cache_k)
    cv = _pages_position_minor(cache_v)
    clf =jnp.transpose(cache_lf, (0, 2, 1))
    tok_spec = pl.BlockSpec((n_tok, width), lambda b, s, pt: (b, 0))
    gate_spec = pl.BlockSpec((n_tok, GATE_PAD), lambda b, s, pt: (b, 0))

    def page_spec(u, dims):
        return pl.BlockSpec((None,) + dims, lambda b, s, pt: (pt[b, s * pages_per_step + u],) + (0,) * len(dims))

    grid_spec = pltpu.PrefetchScalarGridSpec(
        num_scalar_prefetch=1,
        grid=(n_seq, n_pages // pages_per_step),
        in_specs=[tok_spec, tok_spec, tok_spec, gate_spec, pl.BlockSpec((1, GATE_PAD), lambda b, s, pt: (0, 0))]
        + [page_spec(u, (width, page)) for u in range(pages_per_step)] * 2
        + [page_spec(u, (nh, page)) for u in range(pages_per_step)],
        out_specs=[tok_spec, gate_spec],
        scratch_shapes=[pltpu.VMEM((rows, width), BF16), pltpu.VMEM((nh, 1), F32), pltpu.VMEM((rows, 1), F32),
                        pltpu.VMEM((rows, 1), F32), pltpu.VMEM((rows, 1), F32), pltpu.VMEM((rows, width), F32)],
    )
    return pl.pallas_call(
        functools.partial(_fox_dec_body, pages_per_step=pages_per_step, page=page, n_tok=n_tok),
        grid_spec=grid_spec,
        out_shape=[jax.ShapeDtypeStruct((n_seq * n_tok, width), F32),
                   jax.ShapeDtypeStruct((n_seq * n_tok, GATE_PAD), F32)],
        compiler_params=pltpu.CompilerParams(dimension_semantics=("arbitrary", "arbitrary")),
        name="fox_decode",
    )(page_table, q, k_new, v_new, f_new, f_bias,
      *([ck] * pages_per_step), *([cv] * pages_per_step), *([clf] * pages_per_step))


def _pad_lanes(w, width):
    return jnp.pad(w, ((0, 0), (0, width - w.shape[1])))


def _even_pieces(w_in):
    nq = MLSTM_HEADS * MLSTM_DQK
    nv = MLSTM_HEADS * MLSTM_DV
    nm = MOBA_HEADS * HEAD_DIM
    sizes = (nq, nq, nv, nv, 2 * MLSTM_HEADS, nm, nm, nm)
    pieces, start = [], 0
    for s in sizes:
        pieces.append(w_in[:, start:start + s])
        start += s
    assert start == w_in.shape[1]
    pieces[4] = _pad_lanes(pieces[4], GATE_PAD)
    return [p.astype(BF16) for p in pieces]


def _odd_pieces(w_in):
    nw = FOX_HEADS * HEAD_DIM
    pieces = [w_in[:, 0:nw], w_in[:, nw:2 * nw], w_in[:, 2 * nw:3 * nw], _pad_lanes(w_in[:, 3 * nw:], GATE_PAD)]
    assert w_in.shape[1] == 3 * nw + FOX_HEADS
    return [p.astype(BF16) for p in pieces]


def _token_major_view(x_t, heads):
    batch, _, seq = x_t.shape
    return jnp.transpose(x_t.reshape(batch, heads, HEAD_DIM, seq), (0, 3, 1, 2))


PROMPT_TM = 512
FFN_TM = 512
MLSTM_PROMPT_BATCH = 4
MLSTM_SAMPLE_BATCH = 8
FOX_GATE_TM = 512
MOBA_PAGES_PER_STEP = 16
FOX_PAGES_PER_STEP = 8


def kernel(x_prompt, x_sample, cache_moba_k, cache_moba_v, state_mlstm_C, state_mlstm_n, state_mlstm_m, cache_fox_k, cache_fox_v, cache_fox_logf, page_table, norm_mix_g, norm_ffn_g, norm_final_g, even_w_in, even_b_ig, even_b_fg, even_head_norm_g, even_w_out, odd_w_in, odd_b_f, odd_w_out, ffn_w_gate, ffn_w_up, ffn_w_down):
    bp, tp, d = x_prompt.shape
    bs, ts, _ = x_sample.shape
    depth = norm_mix_g.shape[0]
    mp, ms = bp * tp, bs * ts
    hp = x_prompt.reshape(mp, d)
    hs = x_sample.reshape(ms, d)
    tm_p = min(PROMPT_TM, mp)
    tm_s = ms
    ffn_tm_p = min(FFN_TM, mp)
    nv = MLSTM_HEADS * MLSTM_DV
    n_pages = page_table.shape[1]
    pe, se, po, so = [], [], [], []
    for layer in range(depth):
        li = layer // 2
        final = layer == depth - 1
        if layer % 2 == 0:
            ws = _even_pieces(even_w_in[li])
            gate_bias = _pad_lanes(jnp.concatenate([even_b_ig[li], even_b_fg[li]]).reshape(1, -1), GATE_PAD)
            w_out = even_w_out[li].astype(BF16)
            w_out_a, w_out_b = w_out[:nv], w_out[nv:]
            zeros = lambda *s: jnp.zeros(s, F32)
            q1, k1, v1, o1, gt, q2, k2, v2, k2_t, v2_t = norm_proj(
                hp, norm_mix_g[layer], ws, tm_p, ws_t=(ws[6].T, ws[7].T), seq=tp)
            y1, c_p, n_p, m_p = mlstm(q1, k1, v1, o1, gt, gate_bias, even_head_norm_g[li],
                                      zeros(bp, MLSTM_HEADS, MLSTM_DQK, MLSTM_DV), zeros(bp, MLSTM_HEADS, MLSTM_DQK),
                                      zeros(bp, MLSTM_HEADS), bp, tp, math.gcd(bp, MLSTM_PROMPT_BATCH))
            y2 = moba_prompt(q2, k2, v2, bp, tp)
            mixed_p, ws_out = [y1, y2], [w_out_a, w_out_b]
            pe.append((_token_major_view(k2_t, MOBA_HEADS), _token_major_view(v2_t, MOBA_HEADS), c_p, n_p, m_p))
            q1, k1, v1, o1, gt, q2, k2, v2 = norm_proj(hs, norm_mix_g[layer], ws, tm_s)
            y1, c_s, n_s, m_s = mlstm(q1, k1, v1, o1, gt, gate_bias, even_head_norm_g[li],
                                      state_mlstm_C[li], state_mlstm_n[li], state_mlstm_m[li], bs, ts,
                                      math.gcd(bs, MLSTM_SAMPLE_BATCH))
            y2 = moba_decode(q2, k2, v2, cache_moba_k[li], cache_moba_v[li], page_table, ts,
                             min(MOBA_PAGES_PER_STEP, n_pages))
            mixed_s = [y1, y2]
            se.append((k2.reshape(bs, ts, MOBA_HEADS, HEAD_DIM), v2.reshape(bs, ts, MOBA_HEADS, HEAD_DIM), c_s, n_s, m_s))
        else:
            ws = _odd_pieces(odd_w_in[li])
            f_bias = _pad_lanes(odd_b_f[li].reshape(1, -1), GATE_PAD)
            w_out = odd_w_out[li].astype(BF16)
            q, k, v, f, k_t, v_t = norm_proj(hp, norm_mix_g[layer], ws, tm_p, ws_t=(ws[1].T, ws[2].T), seq=tp)
            lf, c_col = fox_gates(f, f_bias, tp, min(FOX_GATE_TM, tp))
            mixed_p, ws_out = [fox_prompt(q, k, v, c_col, bp, tp)], [w_out]
            po.append((_token_major_view(k_t, FOX_HEADS), _token_major_view(v_t, FOX_HEADS),
                       lf[:, :FOX_HEADS].reshape(bp, tp, FOX_HEADS)))
            q, k, v, f = norm_proj(hs, norm_mix_g[layer], ws, tm_s)
            y, lf = fox_decode(q, k, v, f, f_bias, cache_fox_k[li], cache_fox_v[li], cache_fox_logf[li],
                               page_table, ts, min(FOX_PAGES_PER_STEP, n_pages))
            mixed_s = [y]
            so.append((k.reshape(bs, ts, FOX_HEADS, HEAD_DIM), v.reshape(bs, ts, FOX_HEADS, HEAD_DIM),
                       lf[:, :FOX_HEADS].reshape(bs, ts, FOX_HEADS)))
        wg, wu, wd = ffn_w_gate[layer].astype(BF16), ffn_w_up[layer].astype(BF16), ffn_w_down[layer].astype(BF16)
        hp = mix_ffn(hp, mixed_p, ws_out, norm_ffn_g[layer], norm_final_g, wg, wu, wd, ffn_tm_p, final)
        hs = mix_ffn(hs, mixed_s, ws_out, norm_ffn_g[layer], norm_final_g, wg, wu, wd, tm_s, final)
    stack = lambda group, idx: jnp.stack([s[idx] for s in group])
    return (hp.reshape(bp, tp, d), hs.reshape(bs, ts, d),
            stack(pe, 0), stack(pe, 1), stack(pe, 2), stack(pe, 3), stack(pe, 4),
            stack(po, 0), stack(po, 1), stack(po, 2),
            stack(se, 0), stack(se, 1), stack(se, 2), stack(se, 3), stack(se, 4),
            stack(so, 0), stack(so, 1), stack(so, 2))
```

```python
import functools
import math

import jax
import jax.numpy as jnp
import ml_dtypes
import numpy as np
from jax import lax
from jax.experimental import pallas as pl
from jax.experimental.pallas import tpu as pltpu

F32 = jnp.float32
BF16 = jnp.bfloat16
HIGHEST = lax.Precision.HIGHEST

LANES = 128
SUBLANES = 8

HEAD_DIM = 64
MLSTM_HEADS = 4
MLSTM_DQK = 64
MLSTM_DV = 128
MLSTM_CHUNK = 64
MOBA_HEADS = 8
MOBA_BLOCK = 256
MOBA_TOPK = 3
FOX_HEADS = 16
RMS_EPS = 1e-6
GATE_PAD = LANES
HEADS_PER_TILE = LANES // HEAD_DIM

NEG_INF = float("-inf")
MASK_BIAS = -1e30
SPLIT_PIECES = 3


def _log_sigmoid(x):
    return -(jnp.maximum(-x, 0.0) + jnp.log1p(jnp.exp(-jnp.abs(x))))


def _rmsnorm(x, g):
    return x * lax.rsqrt(jnp.mean(x * x, axis=-1, keepdims=True) + RMS_EPS) * g


def _dot(a, b, precision=None):
    return jnp.dot(a, b, precision=precision, preferred_element_type=F32)


def _dot_nt(a, b, precision=None):
    return lax.dot_general(a, b, (((1,), (1,)), ((), ())), precision=precision, preferred_element_type=F32)


def _iota(shape, dim):
    return lax.broadcasted_iota(jnp.int32, shape, dim)


def _add_all(xs):
    return functools.reduce(lambda a, b: a + b, xs)


def _bf16_pieces(x):
    pieces, rest = [], x
    for _ in range(SPLIT_PIECES):
        p = rest.astype(BF16).astype(F32)
        pieces.append(p)
        rest = rest - p
    return pieces


def _norm_proj_body(x_ref, g_ref, *refs, n_w, n_t):
    w_refs, wt_refs = refs[:n_w], refs[n_w:n_w + n_t]
    o_refs, ot_refs = refs[n_w + n_t:2 * n_w + n_t], refs[2 * n_w + n_t:]
    xb = _rmsnorm(x_ref[...], g_ref[...]).astype(BF16)
    for w_ref, o_ref in zip(w_refs, o_refs):
        o_ref[...] = _dot(xb, w_ref[...])
    for wt_ref, ot_ref in zip(wt_refs, ot_refs):
        ot_ref[...] = _dot_nt(wt_ref[...], xb)


def norm_proj(x, g, ws, tm, ws_t=(), seq=None):
    m, d = x.shape
    assert m % tm == 0
    in_specs = [pl.BlockSpec((tm, d), lambda i: (i, 0)), pl.BlockSpec((1, d), lambda i: (0, 0))]
    in_specs += [pl.BlockSpec(w.shape, lambda i: (0, 0)) for w in tuple(ws) + tuple(ws_t)]
    out_specs = [pl.BlockSpec((tm, w.shape[1]), lambda i: (i, 0)) for w in ws]
    out_shape = [jax.ShapeDtypeStruct((m, w.shape[1]), F32) for w in ws]
    if ws_t:
        assert seq % tm == 0 and m % seq == 0
        per_seq = seq // tm
        out_specs += [pl.BlockSpec((None, w.shape[0], tm), lambda i: (i // per_seq, 0, i % per_seq)) for w in ws_t]
        out_shape += [jax.ShapeDtypeStruct((m // seq, w.shape[0], seq), F32) for w in ws_t]
    return pl.pallas_call(
        functools.partial(_norm_proj_body, n_w=len(ws), n_t=len(ws_t)),
        grid=(m // tm,), in_specs=in_specs, out_specs=out_specs, out_shape=out_shape,
        name="norm_proj",
    )(x, g.reshape(1, d), *ws, *ws_t)


def _mix_ffn_body(h_ref, *refs, n_a, final_norm):
    a_refs, wo_refs = refs[:n_a], refs[n_a:2 * n_a]
    g_ref, gf_ref, wg_ref, wu_ref, wd_ref, o_ref = refs[2 * n_a:]
    x = h_ref[...]
    for a_ref, wo_ref in zip(a_refs, wo_refs):
        x = x + _dot(a_ref[...].astype(BF16), wo_ref[...])
    xn = _rmsnorm(x, g_ref[...]).astype(BF16)
    gate = _dot(xn, wg_ref[...])
    up = _dot(xn, wu_ref[...])
    act = (gate * jax.nn.sigmoid(gate) * up).astype(BF16)
    y = x + _dot(act, wd_ref[...])
    o_ref[...] = _rmsnorm(y, gf_ref[...]) if final_norm else y


def mix_ffn(h, acts, ws_out, g, g_final, wg, wu, wd, tm, final_norm):
    m, d = h.shape
    assert m % tm == 0
    resident = lambda w: pl.BlockSpec(w.shape, lambda i: (0, 0), pipeline_mode=pl.Buffered(1))
    in_specs = [pl.BlockSpec((tm, d), lambda i: (i, 0))]
    in_specs += [pl.BlockSpec((tm, a.shape[1]), lambda i: (i, 0)) for a in acts]
    in_specs += [resident(w) for w in ws_out]
    in_specs += [pl.BlockSpec((1, d), lambda i: (0, 0)), pl.BlockSpec((1, d), lambda i: (0, 0)),
                 resident(wg), resident(wu), resident(wd)]
    return pl.pallas_call(
        functools.partial(_mix_ffn_body, n_a=len(acts), final_norm=final_norm),
        grid=(m // tm,), in_specs=in_specs,
        out_specs=pl.BlockSpec((tm, d), lambda i: (i, 0)),
        out_shape=jax.ShapeDtypeStruct((m, d), F32),
        name="mix_ffn",
    )(h, *acts, *ws_out, g.reshape(1, d), g_final.reshape(1, d), wg, wu, wd)


def _mlstm_body(q_ref, k_ref, v_ref, o_ref, g_ref, gb_ref, gh_ref, c0_ref, n0_ref, m0_ref,
                y_ref, c_ref, n_ref, m_ref, c_scr, n_scr, m_scr, *, chunk, n_batch):
    j = pl.program_id(1)
    nh, dk, dv = MLSTM_HEADS, MLSTM_DQK, MLSTM_DV

    @pl.when(j == 0)
    def _():
        c_scr[...] = c0_ref[...]
        n_scr[...] = n0_ref[...]
        m_scr[...] = m0_ref[...]

    lane = _iota((chunk, GATE_PAD), 1)
    r_i = _iota((chunk, chunk), 0)
    c_i = _iota((chunk, chunk), 1)
    causal = r_i >= c_i
    eye = r_i == c_i
    tril = causal.astype(F32)
    eye_b = (_iota((dk, dk), 0) == _iota((dk, dk), 1)).astype(BF16)

    def to_row(col):
        return jnp.sum(jnp.where(eye, col, 0.0), axis=0, keepdims=True)

    chains = [(b, h) for b in range(n_batch) for h in range(nh)]
    each = lambda fn: [fn(n, b, h) for n, (b, h) in enumerate(chains)]

    acts, cums = [], []
    for b in range(n_batch):
        gpre = g_ref[b] + gb_ref[...]
        acts.append(jnp.where(lane < nh, gpre, _log_sigmoid(gpre)))
        cums.append(_dot(tril, acts[b], precision=HIGHEST))
    q = each(lambda n, b, h: q_ref[b, :, h * dk:(h + 1) * dk] * (dk ** -0.5))
    k = each(lambda n, b, h: k_ref[b, :, h * dk:(h + 1) * dk])
    vb = each(lambda n, b, h: v_ref[b, :, h * dv:(h + 1) * dv].astype(BF16))
    qb = each(lambda n, b, h: q[n].astype(BF16))
    ig_c = each(lambda n, b, h: acts[b][:, h:h + 1])
    b_c = each(lambda n, b, h: cums[b][:, nh + h:nh + h + 1])
    m_prev = each(lambda n, b, h: m_scr[b, h])
    c_prev = each(lambda n, b, h: c_scr[b, h])
    n_prev = each(lambda n, b, h: n_scr[b, h])
    dmat = each(lambda n, b, h: jnp.where(causal, b_c[n] - to_row(b_c[n]) + to_row(ig_c[n]), NEG_INF))
    inter = each(lambda n, b, h: b_c[n] + m_prev[n])
    mt = each(lambda n, b, h: jnp.maximum(inter[n], jnp.max(dmat[n], axis=-1, keepdims=True)))
    w_inter = each(lambda n, b, h: jnp.exp(inter[n] - mt[n]))
    qk = each(lambda n, b, h: _dot_nt(qb[n], k[n].astype(BF16)) * jnp.exp(dmat[n] - mt[n]))
    num = each(lambda n, b, h: w_inter[n] * _dot(qb[n], c_prev[n].astype(BF16)) + _dot(qk[n].astype(BF16), vb[n]))
    den = each(lambda n, b, h: w_inter[n] * jnp.sum(q[n] * n_prev[n], axis=-1, keepdims=True)
               + jnp.sum(qk[n], axis=-1, keepdims=True))
    hh = each(lambda n, b, h: num[n] / jnp.maximum(jnp.abs(den[n]), jnp.exp(-mt[n])))
    m_new = each(lambda n, b, h: mt[n][chunk - 1:chunk, :])
    b_last = each(lambda n, b, h: b_c[n][chunk - 1:chunk, :])
    decay = each(lambda n, b, h: jnp.exp(b_last[n] + m_prev[n] - m_new[n]))
    kw = each(lambda n, b, h: k[n] * jnp.exp(b_last[n] - b_c[n] + ig_c[n] - m_new[n]))
    kw_t = each(lambda n, b, h: _dot_nt(eye_b, kw[n].astype(BF16)).astype(BF16))
    for n, (b, h) in enumerate(chains):
        c_scr[b, h] = decay[n] * c_prev[n] + _dot(kw_t[n], vb[n])
        n_scr[b, h] = decay[n] * n_prev[n] + jnp.sum(kw[n], axis=0, keepdims=True)
        m_scr[b, h] = m_new[n]
    for n, (b, h) in enumerate(chains):
        hn = hh[n] * lax.rsqrt(jnp.mean(hh[n] * hh[n], axis=-1, keepdims=True) + RMS_EPS)
        hn = hn * gh_ref[:, h * dv:(h + 1) * dv]
        y_ref[b, :, h * dv:(h + 1) * dv] = jax.nn.sigmoid(o_ref[b, :, h * dv:(h + 1) * dv]) * hn

    @pl.when(j == pl.num_programs(1) - 1)
    def _():
        c_ref[...] = c_scr[...]
        n_ref[...] = n_scr[...]
        m_ref[...] = m_scr[...]


def mlstm(q, k, v, o, gates, gate_bias, g_head, c0, n0, m0, batch, seq, n_batch):
    nh, dk, dv = MLSTM_HEADS, MLSTM_DQK, MLSTM_DV
    chunk = MLSTM_CHUNK if seq % MLSTM_CHUNK == 0 else seq
    assert chunk % SUBLANES == 0 and batch % n_batch == 0
    steps = seq // chunk
    row_spec = lambda w: pl.BlockSpec((n_batch, chunk, w), lambda b, j: (b, j, 0))
    st4 = lambda s: pl.BlockSpec((n_batch,) + s, lambda b, j: (b, 0, 0, 0))
    rows3 = lambda x: x.reshape(batch, seq, x.shape[-1])
    y, c, n, m = pl.pallas_call(
        functools.partial(_mlstm_body, chunk=chunk, n_batch=n_batch),
        grid=(batch // n_batch, steps),
        in_specs=[row_spec(nh * dk), row_spec(nh * dk), row_spec(nh * dv), row_spec(nh * dv), row_spec(GATE_PAD),
                  pl.BlockSpec((1, GATE_PAD), lambda b, j: (0, 0)),
                  pl.BlockSpec((1, nh * dv), lambda b, j: (0, 0)),
                  st4((nh, dk, dv)), st4((nh, 1, dk)), st4((nh, 1, 1))],
        out_specs=[row_spec(nh * dv), st4((nh, dk, dv)), st4((nh, 1, dk)), st4((nh, 1, 1))],
        out_shape=[jax.ShapeDtypeStruct((batch, seq, nh * dv), F32),
                   jax.ShapeDtypeStruct((batch, nh, dk, dv), F32),
                   jax.ShapeDtypeStruct((batch, nh, 1, dk), F32),
                   jax.ShapeDtypeStruct((batch, nh, 1, 1), F32)],
        scratch_shapes=[pltpu.VMEM((n_batch, nh, dk, dv), F32), pltpu.VMEM((n_batch, nh, 1, dk), F32),
                        pltpu.VMEM((n_batch, nh, 1, 1), F32)],
        compiler_params=pltpu.CompilerParams(dimension_semantics=("arbitrary", "arbitrary")),
        name="mlstm",
    )(rows3(q), rows3(k), rows3(v), rows3(o), rows3(gates), gate_bias, g_head.reshape(1, nh * dv),
      c0, n0.reshape(batch, nh, 1, dk), m0.reshape(batch, nh, 1, 1))
    return y.reshape(batch * seq, nh * dv), c, n.reshape(batch, nh, dk), m.reshape(batch, nh)


def _topk_blocks(gate, n_blocks, n_valid, axis):
    idx = _iota(gate.shape, axis)
    rank = jnp.zeros(gate.shape, F32)
    for c in range(n_blocks):
        other = gate[c:c + 1, :] if axis == 0 else gate[:, c:c + 1]
        beats = (other > gate) | ((other == gate) & (c < idx))
        candidate = jnp.where(c < n_valid, 1.0, 0.0)
        rank = rank + jnp.where(beats, candidate, 0.0)
    return (rank < MOBA_TOPK) & (idx < n_valid)


ATT_TILE = 512


def _data_lanes(lane, hh):
    return (lane >= hh * HEAD_DIM) & (lane < (hh + 1) * HEAD_DIM)


def _extra_lane0(hh):
    return (HEADS_PER_TILE - 1 - hh) * HEAD_DIM


def _values_with_ones(v, lane, hh):
    return jnp.where(_data_lanes(lane, hh), v, jnp.where(lane == _extra_lane0(hh), 1.0, 0.0)).astype(BF16)


def _flash_pair(qa, ka_scr, va_scr, i, tile):
    heads = range(HEADS_PER_TILE)
    causal = _iota((tile, tile), 0) >= _iota((tile, tile), 1)

    def step(key_tiles, carry, first_is_diagonal=False):
        starts = [pl.multiple_of(j * tile, tile) for j in key_tiles]
        chains = [(hh, t) for t in range(len(starts)) for hh in heads]
        s = {c: _dot_nt(qa[c[0]], ka_scr[c[0], pl.ds(starts[c[1]], tile), :]) for c in chains}
        if first_is_diagonal:
            s.update({(hh, 0): jnp.where(causal, s[hh, 0], NEG_INF) for hh in heads})
        s_max = {c: jnp.max(s[c], axis=-1, keepdims=True) for c in chains}
        m = {(hh, -1): carry[hh][0] for hh in heads}
        for hh, t in chains:
            m[hh, t] = jnp.maximum(m[hh, t - 1], s_max[hh, t])
        p = {c: jnp.exp(s[c] - m[c]).astype(BF16) for c in chains}
        pv = {c: _dot(p[c], va_scr[c[0], pl.ds(starts[c[1]], tile), :]) for c in chains}
        acc = [carry[hh][1] for hh in heads]
        for hh, t in chains:
            acc[hh] = jnp.exp(m[hh, t - 1] - m[hh, t]) * acc[hh] + pv[hh, t]
        return tuple((m[hh, len(starts) - 1], acc[hh]) for hh in heads)

    carry = tuple((jnp.full((tile, 1), NEG_INF, F32), jnp.zeros((tile, LANES), F32)) for _ in heads)
    n_group = lax.shift_right_logical(i, 2)
    done = n_group * 4
    carry = step([i], carry, True)
    carry = lax.fori_loop(0, n_group, lambda j, c: step([4 * j + t for t in range(4)], c), carry)
    carry = lax.cond(i - done >= 2, lambda c: step([done, done + 1], c), lambda c: c, carry)
    carry = lax.cond((i & 1) == 1, lambda c: step([i - 1], c), lambda c: c, carry)
    lane = _iota((tile, LANES), 1)
    out = jnp.zeros((tile, LANES), F32)
    for hh in heads:
        acc = carry[hh][1]
        e0 = _extra_lane0(hh)
        out = jnp.where(_data_lanes(lane, hh), acc / acc[:, e0:e0 + 1], out)
    return out


def _floor_div_pow2(x, n):
    assert n & (n - 1) == 0
    return lax.shift_right_logical(x, n.bit_length() - 1)


def _alibi_digits(pos):
    assert HEAD_DIM & (HEAD_DIM - 1) == 0
    lo = pos & (HEAD_DIM - 1)
    return (pos - lo).astype(F32), lo.astype(F32)


ALIBI_ROWS = 8


def _alibi_table(n_heads, n_blocks):
    rest = np.array([2.0 ** (-8.0 * (i + 1) / n_heads) for i in range(n_heads)], dtype=np.float32)
    table = np.zeros((n_heads, ALIBI_ROWS, LANES), np.float32)
    for n in range(SPLIT_PIECES):
        piece = rest.astype(ml_dtypes.bfloat16).astype(np.float32)
        rest = rest - piece
        for h in range(n_heads):
            base = _extra_lane0(h % HEADS_PER_TILE) + n_blocks + 4 * n
            table[h, 0, base + 2:base + 4] = piece[h]
            table[h, 1, base] = 1.0
            table[h, 2, base + 1] = 1.0
            table[h, 3, base:base + 2] = piece[h]
            table[h, 4, base + 2] = 1.0
            table[h, 5, base + 3] = 1.0
    return jnp.asarray(table)


def _moba_body(q_ref, k_ref, v_ref, al_ref, o_ref, ka_scr, vb_scr, km_scr, *, n_blocks, tile):
    i = pl.program_id(2)
    seq = k_ref.shape[0]
    d = HEAD_DIM

    def alibi_cols(hh, side, pos_hi, pos_lo):
        rows = al_ref[hh]
        if side == "query":
            return rows[0:1] - pos_hi * rows[1:2] - pos_lo * rows[2:3]
        return rows[3:4] + pos_hi * rows[4:5] + pos_lo * rows[5:6]

    @pl.when(i == 0)
    def _():
        k = k_ref[...]
        v = v_ref[...]
        lane = _iota((seq, LANES), 1)
        pos = _iota((seq, LANES), 0)
        pos_hi, pos_lo = _alibi_digits(pos)
        km_scr[...] = jnp.zeros_like(km_scr)
        for n in range(n_blocks):
            km_scr[n:n + 1, :] = jnp.mean(k[n * MOBA_BLOCK:(n + 1) * MOBA_BLOCK], axis=0, keepdims=True)
        for hh in range(HEADS_PER_TILE):
            idx = lane - _extra_lane0(hh)
            extras = jnp.where(idx == _floor_div_pow2(pos, MOBA_BLOCK), 1.0, alibi_cols(hh, "key", pos_hi, pos_lo))
            ka_scr[hh] = jnp.where(_data_lanes(lane, hh), k, extras).astype(BF16)
            vb_scr[hh] = _values_with_ones(v, lane, hh)

    q = q_ref[...]
    lane = _iota((tile, LANES), 1)
    t_pos = i * tile + _iota((tile, LANES), 0)
    t_hi, t_lo = _alibi_digits(t_pos)
    blk_rows = -(-n_blocks // SUBLANES) * SUBLANES
    own_t = _floor_div_pow2(i * tile + _iota((1, tile), 1), MOBA_BLOCK)
    qa = []
    for hh in range(HEADS_PER_TILE):
        e0 = _extra_lane0(hh)
        idx = lane - e0
        data = _data_lanes(lane, hh)
        gate_t = _dot_nt(km_scr[...], jnp.where(data, q, 0.0), precision=HIGHEST)
        keep_t = _topk_blocks(gate_t, n_blocks, own_t, axis=0) | (_iota((blk_rows, tile), 0) == own_t)
        above, below = e0, LANES - e0 - blk_rows
        bias_t = ([jnp.zeros((above, tile), F32)] if above else []) + [jnp.where(keep_t, 0.0, MASK_BIAS)]
        bias_t += [jnp.zeros((below, tile), F32)] if below else []
        block_bias = jnp.concatenate(bias_t, axis=0).T
        extras = jnp.where((idx >= 0) & (idx < n_blocks), block_bias, alibi_cols(hh, "query", t_hi, t_lo))
        qa.append(jnp.where(data, q * (d ** -0.5), extras).astype(BF16))
    o_ref[...] = _flash_pair(qa, ka_scr, vb_scr, i, tile)


def moba_prompt(q, k, v, batch, seq):
    width = MOBA_HEADS * HEAD_DIM
    tile = min(ATT_TILE, seq)
    assert seq % tile == 0 and tile % MOBA_BLOCK == 0
    n_blocks = seq // MOBA_BLOCK
    assert n_blocks + 4 * SPLIT_PIECES <= HEAD_DIM
    nq = seq // tile
    return pl.pallas_call(
        functools.partial(_moba_body, n_blocks=n_blocks, tile=tile),
        grid=(batch, width // LANES, nq),
        in_specs=[pl.BlockSpec((tile, LANES), lambda b, hp, i: (b * nq + i, hp)),
                  pl.BlockSpec((seq, LANES), lambda b, hp, i: (b, hp)),
                  pl.BlockSpec((seq, LANES), lambda b, hp, i: (b, hp)),
                  pl.BlockSpec((HEADS_PER_TILE, ALIBI_ROWS, LANES), lambda b, hp, i: (hp, 0, 0))],
        out_specs=pl.BlockSpec((tile, LANES), lambda b, hp, i: (b * nq + i, hp)),
        out_shape=jax.ShapeDtypeStruct((batch * seq, width), F32),
        scratch_shapes=[pltpu.VMEM((HEADS_PER_TILE, seq, LANES), BF16), pltpu.VMEM((HEADS_PER_TILE, seq, LANES), BF16),
                        pltpu.VMEM((-(-n_blocks // SUBLANES) * SUBLANES, LANES), F32)],
        compiler_params=pltpu.CompilerParams(dimension_semantics=("arbitrary", "arbitrary", "arbitrary")),
        name="moba_prompt",
    )(q, k, v, _alibi_table(MOBA_HEADS, n_blocks))


def _fox_gate_body(f_ref, b_ref, lf_ref, cc_ref, carry_scr, *, tiles_per_seq):
    i = pl.program_id(0)
    tm = f_ref.shape[0]

    @pl.when(i % tiles_per_seq == 0)
    def _():
        carry_scr[...] = jnp.zeros_like(carry_scr)

    lf = _log_sigmoid(f_ref[...] + b_ref[...])
    lf_ref[...] = lf
    tril = (_iota((tm, tm), 0) >= _iota((tm, tm), 1)).astype(F32)
    c = _dot(tril, lf, precision=HIGHEST) + carry_scr[...]
    carry_scr[...] = c[tm - 1:tm, :]
    cc_ref[...] = c


def fox_gates(f, bias, seq, tm):
    m = f.shape[0]
    assert seq % tm == 0 and m % seq == 0
    return pl.pallas_call(
        functools.partial(_fox_gate_body, tiles_per_seq=seq // tm),
        grid=(m // tm,),
        in_specs=[pl.BlockSpec((tm, GATE_PAD), lambda i: (i, 0)), pl.BlockSpec((1, GATE_PAD), lambda i: (0, 0))],
        out_specs=[pl.BlockSpec((tm, GATE_PAD), lambda i: (i, 0)), pl.BlockSpec((tm, GATE_PAD), lambda i: (i, 0))],
        out_shape=[jax.ShapeDtypeStruct((m, GATE_PAD), F32), jax.ShapeDtypeStruct((m, GATE_PAD), F32)],
        scratch_shapes=[pltpu.VMEM((1, GATE_PAD), F32)],
        compiler_params=pltpu.CompilerParams(dimension_semantics=("arbitrary",)),
        name="fox_gates",
    )(f, bias)


def _fox_body(q_ref, k_ref, v_ref, cq_ref, ck_ref, o_ref, ka_scr, vb_scr, *, tile):
    hp = pl.program_id(1)
    i = pl.program_id(2)
    seq = k_ref.shape[0]
    d = HEAD_DIM

    def head_column(c_all, hh):
        lane_g = _iota(c_all.shape, 1)
        return jnp.sum(jnp.where(lane_g == hp * HEADS_PER_TILE + hh, c_all, 0.0), axis=-1, keepdims=True)

    def sum_cols(idx, c_col, on_pos):
        cols = jnp.zeros(idx.shape, F32)
        for n, p in enumerate(_bf16_pieces(c_col)):
            key_side = on_pos == "key"
            cols = jnp.where(idx == n, 1.0 if key_side else p, cols)
            cols = jnp.where(idx == SPLIT_PIECES + n, -p if key_side else 1.0, cols)
        return cols

    @pl.when(i == 0)
    def _():
        k = k_ref[...]
        v = v_ref[...]
        lane = _iota((seq, LANES), 1)
        c_all = ck_ref[...]
        for hh in range(HEADS_PER_TILE):
            extras = sum_cols(lane - _extra_lane0(hh), head_column(c_all, hh), "key")
            ka_scr[hh] = jnp.where(_data_lanes(lane, hh), k, extras).astype(BF16)
            vb_scr[hh] = _values_with_ones(v, lane, hh)

    q = q_ref[...]
    lane = _iota((tile, LANES), 1)
    c_all = cq_ref[...]
    qa = []
    for hh in range(HEADS_PER_TILE):
        extras = sum_cols(lane - _extra_lane0(hh), head_column(c_all, hh), "query")
        qa.append(jnp.where(_data_lanes(lane, hh), q * (d ** -0.5), extras).astype(BF16))
    o_ref[...] = _flash_pair(qa, ka_scr, vb_scr, i, tile)


def fox_prompt(q, k, v, c_col, batch, seq):
    width = FOX_HEADS * HEAD_DIM
    tile = min(ATT_TILE, seq)
    assert seq % tile == 0 and 2 * SPLIT_PIECES <= HEAD_DIM
    nq = seq // tile
    return pl.pallas_call(
        functools.partial(_fox_body, tile=tile),
        grid=(batch, width // LANES, nq),
        in_specs=[pl.BlockSpec((tile, LANES), lambda b, hp, i: (b * nq + i, hp)),
                  pl.BlockSpec((seq, LANES), lambda b, hp, i: (b, hp)),
                  pl.BlockSpec((seq, LANES), lambda b, hp, i: (b, hp)),
                  pl.BlockSpec((tile, GATE_PAD), lambda b, hp, i: (b * nq + i, 0)),
                  pl.BlockSpec((seq, GATE_PAD), lambda b, hp, i: (b, 0))],
        out_specs=pl.BlockSpec((tile, LANES), lambda b, hp, i: (b * nq + i, hp)),
        out_shape=jax.ShapeDtypeStruct((batch * seq, width), F32),
        scratch_shapes=[pltpu.VMEM((HEADS_PER_TILE, seq, LANES), BF16), pltpu.VMEM((HEADS_PER_TILE, seq, LANES), BF16)],
        compiler_params=pltpu.CompilerParams(dimension_semantics=("arbitrary", "arbitrary", "arbitrary")),
        name="fox_prompt",
    )(q, k, v, c_col, c_col)


def _block_diag_queries(q, n_heads, n_tok):
    rows, width = n_heads * n_tok, n_heads * HEAD_DIM
    tiled = jnp.concatenate([q] * n_heads, axis=0)
    same = (_iota((rows, width), 0) // n_tok) == (_iota((rows, width), 1) // HEAD_DIM)
    return jnp.where(same, tiled, 0.0)


def _head_diag(acc, n_heads, n_tok):
    width = n_heads * HEAD_DIM
    lane_head = _iota((n_tok, width), 1) // HEAD_DIM
    out = jnp.zeros((n_tok, width), F32)
    for h in range(n_heads):
        out = jnp.where(lane_head == h, acc[h * n_tok:(h + 1) * n_tok, :], out)
    return out


def _new_token_partial(qb, kn, vn, bias, n_tok):
    rows = qb.shape[0]
    s = _dot_nt(qb, kn.astype(BF16)) + bias
    tok = _iota((rows, n_tok), 0) % n_tok
    s = jnp.where(tok >= _iota((rows, n_tok), 1), s, NEG_INF)
    m = jnp.max(s, axis=-1, keepdims=True)
    p = jnp.exp(s - m)
    l = jnp.sum(p, axis=-1, keepdims=True)
    return m, l, _dot(p.astype(BF16), vn.astype(BF16))


def _moba_dec_body(pt_ref, q_ref, kn_ref, vn_ref, sl_ref, *refs, pages_per_step, page, n_tok, past_len):
    pp = pages_per_step
    k_refs, v_refs = refs[:pp], refs[pp:2 * pp]
    o_ref = refs[2 * pp]
    qb_scr, g_scr, m_scr, l_scr, acc_scr = refs[2 * pp + 1:]
    s_id = pl.program_id(1)
    nh, d = MOBA_HEADS, HEAD_DIM
    rows = nh * n_tok
    ppb = MOBA_BLOCK // page
    n_past = past_len // MOBA_BLOCK
    scale = d ** -0.5
    slope = sl_ref[...]
    tok = (_iota((rows, 1), 0) % n_tok).astype(F32)
    lane = _iota((rows, LANES), 1)

    @pl.when(s_id == 0)
    def _():
        qb_scr[...] = (_block_diag_queries(q_ref[...], nh, n_tok) * scale).astype(BF16)
        g_scr[...] = jnp.zeros_like(g_scr)
        m_scr[...] = jnp.zeros_like(m_scr)
        l_scr[...] = jnp.zeros_like(l_scr)

    qb = qb_scr[...]
    blocks = range(pp // ppb)
    pages = range(pp)
    blk0 = s_id * (pp // ppb)
    qk = [_dot(qb, k_refs[u][...].astype(BF16)) for u in pages]
    qk_sum = [jnp.sum(qk[u], axis=-1, keepdims=True) for u in pages]
    pos0 = (blk0 * MOBA_BLOCK).astype(F32) + _iota((rows, page), 1).astype(F32)
    s = [qk[u] - slope * ((past_len + tok) - (pos0 + float(u * page))) for u in pages]
    s_max = [jnp.max(s[u], axis=-1, keepdims=True) for u in pages]
    m = [functools.reduce(jnp.maximum, s_max[g * ppb:(g + 1) * ppb]) for g in blocks]
    p = [jnp.exp(s[u] - m[u // ppb]) for u in pages]
    p_sum = [jnp.sum(p[u], axis=-1, keepdims=True) for u in pages]
    pv = [_dot_nt(p[u].astype(BF16), v_refs[u][...].astype(BF16)) for u in pages]
    g_all, m_all, l_all = g_scr[...], m_scr[...], l_scr[...]
    for g in blocks:
        mine = slice(g * ppb, (g + 1) * ppb)
        gate = _add_all(qk_sum[mine]) * (1.0 / (scale * MOBA_BLOCK))
        g_all = jnp.where(lane == blk0 + g, gate, g_all)
        m_all = jnp.where(lane == blk0 + g, m[g], m_all)
        l_all = jnp.where(lane == blk0 + g, _add_all(p_sum[mine]), l_all)
        acc_scr[blk0 + g] = _add_all(pv[mine])
    g_scr[...], m_scr[...], l_scr[...] = g_all, m_all, l_all

    @pl.when(s_id == pl.num_programs(1) - 1)
    def _():
        sel = _topk_blocks(g_scr[...], n_past, n_past, axis=1)
        m_all = m_scr[...]
        t_k = _iota((rows, n_tok), 1).astype(F32)
        m_own, l_own, acc_own = _new_token_partial(qb, kn_ref[...], vn_ref[...], -slope * (tok - t_k), n_tok)
        m_tot = jnp.maximum(m_own, jnp.max(jnp.where(sel, m_all, NEG_INF), axis=-1, keepdims=True))
        w = jnp.where(sel, jnp.exp(m_all - m_tot), 0.0)
        w_own = jnp.exp(m_own - m_tot)
        l_tot = w_own * l_own + jnp.sum(w * l_scr[...], axis=-1, keepdims=True)
        acc = w_own * acc_own
        for n in range(n_past):
            acc = acc + w[:, n:n + 1] * acc_scr[n]
        o_ref[...] = _head_diag(acc / l_tot, nh, n_tok)


def _pages_position_minor(cache):
    n_phys, page, heads, d = cache.shape
    return jnp.transpose(cache, (0, 2, 3, 1)).reshape(n_phys, heads * d, page)


def moba_decode(q, k_new, v_new, cache_k, cache_v, page_table, n_tok, pages_per_step):
    n_seq, n_pages = page_table.shape
    page = cache_k.shape[1]
    nh, d = MOBA_HEADS, HEAD_DIM
    width = nh * d
    past_len = n_pages * page
    rows = nh * n_tok
    assert MOBA_BLOCK % page == 0 and past_len % MOBA_BLOCK == 0 and n_tok <= MOBA_BLOCK
    assert pages_per_step % (MOBA_BLOCK // page) == 0 and n_pages % pages_per_step == 0
    n_past = past_len // MOBA_BLOCK
    ck = _pages_position_minor(cache_k)
    cv = _pages_position_minor(cache_v)
    slope_rows = jnp.asarray(np.repeat(
        np.array([2.0 ** (-8.0 * (i + 1) / nh) for i in range(nh)], dtype=np.float32), n_tok).reshape(rows, 1))
    tok_spec = pl.BlockSpec((n_tok, width), lambda b, s, pt: (b, 0))

    assert n_past <= LANES

    def page_spec(u):
        return pl.BlockSpec((None, width, page), lambda b, s, pt: (pt[b, s * pages_per_step + u], 0, 0))

    grid_spec = pltpu.PrefetchScalarGridSpec(
        num_scalar_prefetch=1,
        grid=(n_seq, n_pages // pages_per_step),
        in_specs=[tok_spec, tok_spec, tok_spec, pl.BlockSpec((rows, 1), lambda b, s, pt: (0, 0))]
        + [page_spec(u) for u in range(pages_per_step)] * 2,
        out_specs=tok_spec,
        scratch_shapes=[pltpu.VMEM((rows, width), BF16), pltpu.VMEM((rows, LANES), F32),
                        pltpu.VMEM((rows, LANES), F32), pltpu.VMEM((rows, LANES), F32),
                        pltpu.VMEM((n_past, rows, width), F32)],
    )
    return pl.pallas_call(
        functools.partial(_moba_dec_body, pages_per_step=pages_per_step, page=page, n_tok=n_tok, past_len=past_len),
        grid_spec=grid_spec,
        out_shape=jax.ShapeDtypeStruct((n_seq * n_tok, width), F32),
        compiler_params=pltpu.CompilerParams(dimension_semantics=("arbitrary", "arbitrary")),
        name="moba_decode",
    )(page_table, q, k_new, v_new, slope_rows, *([ck] * pages_per_step), *([cv] * pages_per_step))


def _fox_dec_body(pt_ref, q_ref, kn_ref, vn_ref, f_ref, fb_ref, *refs, pages_per_step, page, n_tok):
    pp = pages_per_step
    k_refs, v_refs, lf_refs = refs[:pp], refs[pp:2 * pp], refs[2 * pp:3 * pp]
    o_ref, lfn_ref = refs[3 * pp], refs[3 * pp + 1]
    qb_scr, carry_scr, ctot_scr, m_scr, l_scr, acc_scr = refs[3 * pp + 2:]
    s_id = pl.program_id(1)
    nh, d = FOX_HEADS, HEAD_DIM
    rows = nh * n_tok
    expand = ((_iota((rows, nh), 0) // n_tok) == _iota((rows, nh), 1)).astype(F32)
    triu = (_iota((page, page), 0) <= _iota((page, page), 1)).astype(BF16)

    @pl.when(s_id == 0)
    def _():
        qb_scr[...] = (_block_diag_queries(q_ref[...], nh, n_tok) * (d ** -0.5)).astype(BF16)
        carry_scr[...] = jnp.zeros_like(carry_scr)
        m_scr[...] = jnp.full_like(m_scr, NEG_INF)
        l_scr[...] = jnp.zeros_like(l_scr)
        acc_scr[...] = jnp.zeros_like(acc_scr)

    qb = qb_scr[...]
    ss = []
    offset = carry_scr[...]
    for u in range(pp):
        pieces = jnp.concatenate([p.astype(BF16) for p in _bf16_pieces(lf_refs[u][...])], axis=0)
        sums = _dot(pieces, triu)
        c_heads = offset + _add_all([sums[n * nh:(n + 1) * nh] for n in range(SPLIT_PIECES)])
        offset = c_heads[:, page - 1:page]
        c_rows = jnp.broadcast_to(c_heads[:, None, :], (nh, n_tok, page)).reshape(rows, page)
        ss.append(_dot(qb, k_refs[u][...].astype(BF16)) - c_rows)
    carry_scr[...] = offset
    ctot_scr[...] = c_rows[:, page - 1:page]
    m_prev = m_scr[...]
    m_new = m_prev
    for s in ss:
        m_new = jnp.maximum(m_new, jnp.max(s, axis=-1, keepdims=True))
    alpha = jnp.exp(m_prev - m_new)
    l_new = alpha * l_scr[...]
    pv = None
    for u, s in enumerate(ss):
        p = jnp.exp(s - m_new)
        l_new = l_new + jnp.sum(p, axis=-1, keepdims=True)
        part = _dot_nt(p.astype(BF16), v_refs[u][...].astype(BF16))
        pv = part if pv is None else pv + part
    acc_scr[...] = alpha * acc_scr[...] + pv
    l_scr[...] = l_new
    m_scr[...] = m_new

    @pl.when(s_id == pl.num_programs(1) - 1)
    def _():
        lf_new = _log_sigmoid(f_ref[...] + fb_ref[...])
        lfn_ref[...] = lf_new
        tril_t = (_iota((n_tok, n_tok), 0) >= _iota((n_tok, n_tok), 1)).astype(F32)
        c_new = _dot(tril_t, lf_new, precision=HIGHEST)[:, :nh]
        c_new_rows = jnp.concatenate([c_new] * nh, axis=0)
        c_q = jnp.sum(expand * c_new_rows, axis=-1, keepdims=True)
        m_past = m_scr[...] + (ctot_scr[...] + c_q)
        c_k = _dot_nt(expand, c_new, precision=HIGHEST)
        m_own, l_own, acc_own = _new_token_partial(qb, kn_ref[...], vn_ref[...], c_q - c_k, n_tok)
        m_tot = jnp.maximum(m_past, m_own)
        w_past = jnp.exp(m_past - m_tot)
        w_own = jnp.exp(m_own - m_tot)
        l_tot = w_past * l_scr[...] + w_own * l_own
        acc = w_past * acc_scr[...] + w_own * acc_own
        o_ref[...] = _head_diag(acc / l_tot, nh, n_tok)


def fox_decode(q, k_new, v_new, f_new, f_bias, cache_k, cache_v, cache_lf, page_table, n_tok, pages_per_step):
    n_seq, n_pages = page_table.shape
    page = cache_k.shape[1]
    nh, d = FOX_HEADS, HEAD_DIM
    width = nh * d
    rows = nh * n_tok
    assert n_pages % pages_per_step == 0
    assert n_tok % SUBLANES == 0
    ck = _pages_position_minor(cache_k)
    cv = _pages_position_minor(cache_v)
    clf =jnp.transpose(cache_lf, (0, 2, 1))
    tok_spec = pl.BlockSpec((n_tok, width), lambda b, s, pt: (b, 0))
    gate_spec = pl.BlockSpec((n_tok, GATE_PAD), lambda b, s, pt: (b, 0))

    def page_spec(u, dims):
        return pl.BlockSpec((None,) + dims, lambda b, s, pt: (pt[b, s * pages_per_step + u],) + (0,) * len(dims))

    grid_spec = pltpu.PrefetchScalarGridSpec(
        num_scalar_prefetch=1,
        grid=(n_seq, n_pages // pages_per_step),
        in_specs=[tok_spec, tok_spec, tok_spec, gate_spec, pl.BlockSpec((1, GATE_PAD), lambda b, s, pt: (0, 0))]
        + [page_spec(u, (width, page)) for u in range(pages_per_step)] * 2
        + [page_spec(u, (nh, page)) for u in range(pages_per_step)],
        out_specs=[tok_spec, gate_spec],
        scratch_shapes=[pltpu.VMEM((rows, width), BF16), pltpu.VMEM((nh, 1), F32), pltpu.VMEM((rows, 1), F32),
                        pltpu.VMEM((rows, 1), F32), pltpu.VMEM((rows, 1), F32), pltpu.VMEM((rows, width), F32)],
    )
    return pl.pallas_call(
        functools.partial(_fox_dec_body, pages_per_step=pages_per_step, page=page, n_tok=n_tok),
        grid_spec=grid_spec,
        out_shape=[jax.ShapeDtypeStruct((n_seq * n_tok, width), F32),
                   jax.ShapeDtypeStruct((n_seq * n_tok, GATE_PAD), F32)],
        compiler_params=pltpu.CompilerParams(dimension_semantics=("arbitrary", "arbitrary")),
        name="fox_decode",
    )(page_table, q, k_new, v_new, f_new, f_bias,
      *([ck] * pages_per_step), *([cv] * pages_per_step), *([clf] * pages_per_step))


def _pad_lanes(w, width):
    return jnp.pad(w, ((0, 0), (0, width - w.shape[1])))


def _even_pieces(w_in):
    nq = MLSTM_HEADS * MLSTM_DQK
    nv = MLSTM_HEADS * MLSTM_DV
    nm = MOBA_HEADS * HEAD_DIM
    sizes = (nq, nq, nv, nv, 2 * MLSTM_HEADS, nm, nm, nm)
    pieces, start = [], 0
    for s in sizes:
        pieces.append(w_in[:, start:start + s])
        start += s
    assert start == w_in.shape[1]
    pieces[4] = _pad_lanes(pieces[4], GATE_PAD)
    return [p.astype(BF16) for p in pieces]


def _odd_pieces(w_in):
    nw = FOX_HEADS * HEAD_DIM
    pieces = [w_in[:, 0:nw], w_in[:, nw:2 * nw], w_in[:, 2 * nw:3 * nw], _pad_lanes(w_in[:, 3 * nw:], GATE_PAD)]
    assert w_in.shape[1] == 3 * nw + FOX_HEADS
    return [p.astype(BF16) for p in pieces]


def _token_major_view(x_t, heads):
    batch, _, seq = x_t.shape
    return jnp.transpose(x_t.reshape(batch, heads, HEAD_DIM, seq), (0, 3, 1, 2))


PROMPT_TM = 512
FFN_TM = 512
MLSTM_PROMPT_BATCH = 4
MLSTM_SAMPLE_BATCH = 8
FOX_GATE_TM = 512
MOBA_PAGES_PER_STEP = 16
FOX_PAGES_PER_STEP = 16


def kernel(x_prompt, x_sample, cache_moba_k, cache_moba_v, state_mlstm_C, state_mlstm_n, state_mlstm_m, cache_fox_k, cache_fox_v, cache_fox_logf, page_table, norm_mix_g, norm_ffn_g, norm_final_g, even_w_in, even_b_ig, even_b_fg, even_head_norm_g, even_w_out, odd_w_in, odd_b_f, odd_w_out, ffn_w_gate, ffn_w_up, ffn_w_down):
    bp, tp, d = x_prompt.shape
    bs, ts, _ = x_sample.shape
    depth = norm_mix_g.shape[0]
    mp, ms = bp * tp, bs * ts
    hp = x_prompt.reshape(mp, d)
    hs = x_sample.reshape(ms, d)
    tm_p = min(PROMPT_TM, mp)
    tm_s = ms
    ffn_tm_p = min(FFN_TM, mp)
    nv = MLSTM_HEADS * MLSTM_DV
    n_pages = page_table.shape[1]
    pe, se, po, so = [], [], [], []
    for layer in range(depth):
        li = layer // 2
        final = layer == depth - 1
        if layer % 2 == 0:
            ws = _even_pieces(even_w_in[li])
            gate_bias = _pad_lanes(jnp.concatenate([even_b_ig[li], even_b_fg[li]]).reshape(1, -1), GATE_PAD)
            w_out = even_w_out[li].astype(BF16)
            w_out_a, w_out_b = w_out[:nv], w_out[nv:]
            zeros = lambda *s: jnp.zeros(s, F32)
            q1, k1, v1, o1, gt, q2, k2, v2, k2_t, v2_t = norm_proj(
                hp, norm_mix_g[layer], ws, tm_p, ws_t=(ws[6].T, ws[7].T), seq=tp)
            y1, c_p, n_p, m_p = mlstm(q1, k1, v1, o1, gt, gate_bias, even_head_norm_g[li],
                                      zeros(bp, MLSTM_HEADS, MLSTM_DQK, MLSTM_DV), zeros(bp, MLSTM_HEADS, MLSTM_DQK),
                                      zeros(bp, MLSTM_HEADS), bp, tp, math.gcd(bp, MLSTM_PROMPT_BATCH))
            y2 = moba_prompt(q2, k2, v2, bp, tp)
            mixed_p, ws_out = [y1, y2], [w_out_a, w_out_b]
            pe.append((_token_major_view(k2_t, MOBA_HEADS), _token_major_view(v2_t, MOBA_HEADS), c_p, n_p, m_p))
            q1, k1, v1, o1, gt, q2, k2, v2 = norm_proj(hs, norm_mix_g[layer], ws, tm_s)
            y1, c_s, n_s, m_s = mlstm(q1, k1, v1, o1, gt, gate_bias, even_head_norm_g[li],
                                      state_mlstm_C[li], state_mlstm_n[li], state_mlstm_m[li], bs, ts,
                                      math.gcd(bs, MLSTM_SAMPLE_BATCH))
            y2 = moba_decode(q2, k2, v2, cache_moba_k[li], cache_moba_v[li], page_table, ts,
                             min(MOBA_PAGES_PER_STEP, n_pages))
            mixed_s = [y1, y2]
            se.append((k2.reshape(bs, ts, MOBA_HEADS, HEAD_DIM), v2.reshape(bs, ts, MOBA_HEADS, HEAD_DIM), c_s, n_s, m_s))
        else:
            ws = _odd_pieces(odd_w_in[li])
            f_bias = _pad_lanes(odd_b_f[li].reshape(1, -1), GATE_PAD)
            w_out = odd_w_out[li].astype(BF16)
            q, k, v, f, k_t, v_t = norm_proj(hp, norm_mix_g[layer], ws, tm_p, ws_t=(ws[1].T, ws[2].T), seq=tp)
            lf, c_col = fox_gates(f, f_bias, tp, min(FOX_GATE_TM, tp))
            mixed_p, ws_out = [fox_prompt(q, k, v, c_col, bp, tp)], [w_out]
            po.append((_token_major_view(k_t, FOX_HEADS), _token_major_view(v_t, FOX_HEADS),
                       lf[:, :FOX_HEADS].reshape(bp, tp, FOX_HEADS)))
            q, k, v, f = norm_proj(hs, norm_mix_g[layer], ws, tm_s)
            y, lf = fox_decode(q, k, v, f, f_bias, cache_fox_k[li], cache_fox_v[li], cache_fox_logf[li],
                               page_table, ts, min(FOX_PAGES_PER_STEP, n_pages))
            mixed_s = [y]
            so.append((k.reshape(bs, ts, FOX_HEADS, HEAD_DIM), v.reshape(bs, ts, FOX_HEADS, HEAD_DIM),
                       lf[:, :FOX_HEADS].reshape(bs, ts, FOX_HEADS)))
        wg, wu, wd = ffn_w_gate[layer].astype(BF16), ffn_w_up[layer].astype(BF16), ffn_w_down[layer].astype(BF16)
        hp = mix_ffn(hp, mixed_p, ws_out, norm_ffn_g[layer], norm_final_g, wg, wu, wd, ffn_tm_p, final)
        hs = mix_ffn(hs, mixed_s, ws_out, norm_ffn_g[layer], norm_final_g, wg, wu, wd, tm_s, final)
    stack = lambda group, idx: jnp.stack([s[idx] for s in group])
    return (hp.reshape(bp, tp, d), hs.reshape(bs, ts, d),
            stack(pe, 0), stack(pe, 1), stack(pe, 2), stack(pe, 3), stack(pe, 4),
            stack(po, 0), stack(po, 1), stack(po, 2),
            stack(se, 0), stack(se, 1), stack(se, 2), stack(se, 3), stack(se, 4),
            stack(so, 0), stack(so, 1), stack(so, 2))
```
